```python
import jax, jax.numpy as jnp
from jax import lax
import numpy as np

D_MODEL = 2048
BATCH = 4
SEQ = 2048
DEPTH = 2

ROPE_THETA = 10000.0
NORM_EPS = 1e-6
MASK_VALUE = -1e30
F_MIN = 1e-12
CHUNK = 64
GLA_HEADS = 4
GLA_DK = D_MODEL // 2 // GLA_HEADS
GLA_DV = D_MODEL // GLA_HEADS
GLA_GATE_RANK = 16
GLA_GATE_NORMALIZER = 16.0
DSA_HEADS = 16
DSA_HEAD_DIM = 128
IDX_HEADS = 8
IDX_HEAD_DIM = 64
DSA_TOPK_MAX = 256
Q_BLOCK = 128
HGRN_EXPAND = 128
HGRN_HEADS = D_MODEL // HGRN_EXPAND
HGRN_DK = HGRN_EXPAND
HGRN_DV = D_MODEL // HGRN_HEADS
D_FF = 4 * D_MODEL

IN_SIZES = (
    GLA_HEADS * GLA_DK, GLA_HEADS * GLA_DK, GLA_HEADS * GLA_DV, GLA_HEADS * GLA_DV, GLA_GATE_RANK,
    DSA_HEADS * DSA_HEAD_DIM, DSA_HEAD_DIM, DSA_HEAD_DIM, IDX_HEADS * IDX_HEAD_DIM, IDX_HEAD_DIM, IDX_HEADS,
    HGRN_HEADS * HGRN_DK, HGRN_HEADS * HGRN_DK, HGRN_HEADS * HGRN_DV, HGRN_HEADS * HGRN_DV,
    D_MODEL, D_MODEL, D_MODEL,
)
N_IN = sum(IN_SIZES)

kernel_name = "hybrid_gla_dsa_hgrn2_gated_block"


def rmsnorm(x, w):
    xf = x.astype(jnp.float32)
    y = xf * lax.rsqrt(jnp.mean(xf * xf, axis=-1, keepdims=True) + NORM_EPS)
    return (y * w.astype(jnp.float32)).astype(x.dtype)


def rotary(x, positions):
    d = x.shape[-1]
    inv_freq = ROPE_THETA ** (-jnp.arange(0, d, 2, dtype=jnp.float32) / d)
    ang = positions.astype(jnp.float32)[..., None] * inv_freq
    cos = jnp.cos(ang)[:, :, None, :]
    sin = jnp.sin(ang)[:, :, None, :]
    xf = x.astype(jnp.float32)
    x1, x2 = xf[..., : d // 2], xf[..., d // 2:]
    return jnp.concatenate([x1 * cos - x2 * sin, x2 * cos + x1 * sin], axis=-1).astype(x.dtype)


def chunk_gated_linear_attn(q, k, v, log_a):
    B, T, H, K = q.shape
    V = v.shape[-1]
    n = T // CHUNK

    def to_chunks(a):
        return a.astype(jnp.float32).reshape(B, n, CHUNK, H, a.shape[-1]).transpose(1, 0, 3, 2, 4)

    qc, kc, vc, gc = to_chunks(q), to_chunks(k), to_chunks(v), to_chunks(log_a)
    causal = jnp.tril(jnp.ones((CHUNK, CHUNK), dtype=bool))[:, :, None]

    def step(S, inp):
        qb, kb, vb, gb = inp
        b = jnp.cumsum(gb, axis=-2)
        o_inter = jnp.einsum('bhtk,bhkv->bhtv', qb * jnp.exp(b), S)
        diff = jnp.where(causal, b[:, :, :, None, :] - b[:, :, None, :, :], 0.0)
        decay = jnp.where(causal, jnp.exp(diff), 0.0)
        att = jnp.einsum('bhtk,bhsk,bhtsk->bhts', qb, kb, decay)
        o = o_inter + jnp.einsum('bhts,bhsv->bhtv', att, vb)
        b_last = b[:, :, -1:, :]
        S = jnp.exp(b_last[:, :, 0, :, None]) * S + jnp.einsum('bhsk,bhsv->bhkv', kb * jnp.exp(b_last - b), vb)
        return S, o

    S0 = jnp.zeros((B, H, K, V), jnp.float32)
    _, o = lax.scan(step, S0, (qc, kc, vc, gc))
    return o.transpose(1, 0, 3, 2, 4).reshape(B, T, H, V).astype(v.dtype)


def gla_branch(q, k, v, g, a_low, w2, b2, onorm_w):
    B, T, _ = q.shape
    q = q.reshape(B, T, GLA_HEADS, GLA_DK) * (GLA_DK ** -0.5)
    k = k.reshape(B, T, GLA_HEADS, GLA_DK)
    v = v.reshape(B, T, GLA_HEADS, GLA_DV)
    log_a = jax.nn.log_sigmoid((a_low @ w2 + b2).astype(jnp.float32)) / GLA_GATE_NORMALIZER
    log_a = log_a.reshape(B, T, GLA_HEADS, GLA_DK)
    o = chunk_gated_linear_attn(q, k, v, log_a)
    o = rmsnorm(o, onorm_w) * jax.nn.silu(g.reshape(B, T, GLA_HEADS, GLA_DV))
    return o.reshape(B, T, GLA_HEADS * GLA_DV)


def hgrn2_branch(q, f, i, g, lb, onorm_w):
    B, T, _ = q.shape
    q = jax.nn.silu(q).reshape(B, T, HGRN_HEADS, HGRN_DK) * (HGRN_DK ** -0.5)
    fl = f.astype(jnp.float32).reshape(B, T, HGRN_HEADS, HGRN_DK)
    lb = lb.reshape(HGRN_HEADS, HGRN_DK)
    f_gate = lb + (1.0 - lb) * jax.nn.sigmoid(fl)
    log_f = jnp.log(jnp.maximum(f_gate, F_MIN))
    k = (1.0 - lb) * jax.nn.sigmoid(-fl)
    v = i.reshape(B, T, HGRN_HEADS, HGRN_DV)
    o = chunk_gated_linear_attn(q, k, v, log_f)
    o = rmsnorm(o, onorm_w) * jax.nn.sigmoid(g.reshape(B, T, HGRN_HEADS, HGRN_DV))
    return o.reshape(B, T, HGRN_HEADS * HGRN_DV)


def dsa_branch(q, k, v, iq, ik, iw, positions):
    B, T, _ = q.shape
    topk = min(DSA_TOPK_MAX, T // 4)
    nb = T // Q_BLOCK
    q = rotary(q.reshape(B, T, DSA_HEADS, DSA_HEAD_DIM), positions)
    k = rotary(k[:, :, None, :], positions)[:, :, 0]
    iq = rotary(iq.reshape(B, T, IDX_HEADS, IDX_HEAD_DIM), positions) * (IDX_HEAD_DIM ** -0.5)
    ik = rotary(ik[:, :, None, :], positions)[:, :, 0]
    iw = iw * (IDX_HEADS ** -0.5)
    key_pos = jnp.arange(T)

    def blocks(a):
        return jnp.moveaxis(a.reshape(B, nb, Q_BLOCK, *a.shape[2:]), 1, 0)

    def one_block(inp):
        qb, iqb, iwb, start = inp
        qpos = start + jnp.arange(Q_BLOCK)
        rel = jax.nn.relu(jnp.einsum('bqhd,bsd->bqhs', iqb, ik))
        score = jnp.einsum('bqh,bqhs->bqs', iwb, rel).astype(jnp.float32)
        allowed = key_pos[None, :] <= qpos[:, None]
        score = jnp.where(allowed[None], score, MASK_VALUE)
        _, idx = lax.top_k(score, topk)
        valid = idx <= qpos[None, :, None]
        ks = jax.vmap(lambda kk, ii: kk[ii])(k, idx)
        vs = jax.vmap(lambda vv, ii: vv[ii])(v, idx)
        s = jnp.einsum('bqhd,bqkd->bqhk', qb, ks).astype(jnp.float32) * (DSA_HEAD_DIM ** -0.5)
        s = jnp.where(valid[:, :, None, :], s, MASK_VALUE)
        p = jax.nn.softmax(s, axis=-1).astype(vs.dtype)
        return jnp.einsum('bqhk,bqkd->bqhd', p, vs)

    starts = jnp.arange(nb) * Q_BLOCK
    o = lax.map(one_block, (blocks(q), blocks(iq), blocks(iw), starts))
    return jnp.moveaxis(o, 0, 1).reshape(B, T, DSA_HEADS * DSA_HEAD_DIM)


def setup_inputs(seed: int = 0) -> dict:
    key = jax.random.key(seed)
    ks = jax.random.split(key, 17)
    f32 = jnp.float32

    def nrm(k, shape, fan_in):
        return jax.random.normal(k, shape, f32) * (fan_in ** -0.5)

    return {
        "x": jax.random.normal(ks[0], (BATCH, SEQ, D_MODEL), f32),
        "positions": jnp.broadcast_to(jnp.arange(SEQ, dtype=jnp.int32), (BATCH, SEQ)),
        "norm1_w": 1.0 + 0.01 * jax.random.normal(ks[1], (DEPTH, D_MODEL), f32),
        "w_in": nrm(ks[2], (DEPTH, D_MODEL, N_IN), D_MODEL),
        "gla_gate_w2": nrm(ks[3], (DEPTH, GLA_GATE_RANK, GLA_HEADS * GLA_DK), GLA_GATE_RANK),
        "gla_gate_b": 0.1 * jax.random.normal(ks[4], (DEPTH, GLA_HEADS * GLA_DK), f32),
        "gla_onorm_w": 1.0 + 0.01 * jax.random.normal(ks[5], (DEPTH, GLA_DV), f32),
        "hgrn_lower_bounds": 0.1 * jax.random.normal(ks[6], (DEPTH, HGRN_HEADS * HGRN_DK), f32),
        "hgrn_onorm_w": 1.0 + 0.01 * jax.random.normal(ks[7], (DEPTH, HGRN_DV), f32),
        "w_branch_gla": nrm(ks[8], (DEPTH, GLA_HEADS * GLA_DV, D_MODEL), GLA_HEADS * GLA_DV),
        "w_branch_dsa": nrm(ks[9], (DEPTH, DSA_HEADS * DSA_HEAD_DIM, D_MODEL), DSA_HEADS * DSA_HEAD_DIM),
        "w_branch_hgrn": nrm(ks[10], (DEPTH, HGRN_HEADS * HGRN_DV, D_MODEL), HGRN_HEADS * HGRN_DV),
        "w_out": nrm(ks[11], (DEPTH, D_MODEL, D_MODEL), D_MODEL),
        "norm2_w": 1.0 + 0.01 * jax.random.normal(ks[12], (DEPTH, D_MODEL), f32),
        "w_mlp_up": nrm(ks[13], (DEPTH, D_MODEL, D_FF), D_MODEL),
        "w_mlp_down": nrm(ks[14], (DEPTH, D_FF, D_MODEL), D_FF),
        "final_norm_w": 1.0 + 0.01 * jax.random.normal(ks[15], (D_MODEL,), f32),
    }


def reference(x, positions, norm1_w, w_in, gla_gate_w2, gla_gate_b, gla_onorm_w, hgrn_lower_bounds,
              hgrn_onorm_w, w_branch_gla, w_branch_dsa, w_branch_hgrn, w_out, norm2_w, w_mlp_up,
              w_mlp_down, final_norm_w):
    split_points = np.cumsum(IN_SIZES)[:-1].tolist()
    p_lb = jax.nn.softmax(hgrn_lower_bounds.astype(jnp.float32), axis=0)
    lb_all = jnp.cumsum(p_lb, axis=0) - p_lb[0]
    h = x
    for l in range(DEPTH):
        u = rmsnorm(h, norm1_w[l])
        proj = u @ w_in[l]
        (gla_q, gla_k, gla_v, gla_g, gla_a,
         dsa_q, dsa_k, dsa_v, idx_q, idx_k, idx_w,
         hg_q, hg_f, hg_i, hg_g,
         gate_a, gate_b, gate_c) = jnp.split(proj, split_points, axis=-1)
        o_gla = gla_branch(gla_q, gla_k, gla_v, gla_g, gla_a, gla_gate_w2[l], gla_gate_b[l], gla_onorm_w[l])
        o_dsa = dsa_branch(dsa_q, dsa_k, dsa_v, idx_q, idx_k, idx_w, positions)
        o_hg = hgrn2_branch(hg_q, hg_f, hg_i, hg_g, lb_all[l], hgrn_onorm_w[l])
        merged = (jax.nn.sigmoid(gate_a) * (o_gla @ w_branch_gla[l])
                  + jax.nn.sigmoid(gate_b) * (o_dsa @ w_branch_dsa[l])
                  + jax.nn.sigmoid(gate_c) * (o_hg @ w_branch_hgrn[l]))
        h = h + merged @ w_out[l]
        u = rmsnorm(h, norm2_w[l])
        h = h + jnp.square(jax.nn.relu(u @ w_mlp_up[l])) @ w_mlp_down[l]
    return rmsnorm(h, final_norm_w)
```

```python
import functools
from typing import NamedTuple

import jax
import jax.numpy as jnp
from jax import lax
from jax.experimental import pallas as pl
from jax.experimental.pallas import tpu as pltpu

F32 = jnp.float32
BF16 = jnp.bfloat16
HIGHEST = lax.Precision.HIGHEST

ROPE_THETA = 10000.0
NORM_EPS = 1e-6
MASK_VALUE = -1e30
F_MIN = 1e-12
GLA_GATE_NORMALIZER = 16.0
LANES = 128
INT_MIN = -2 ** 31
SAFE_CHUNK_DECAY = 150.0
VMEM_LIMIT = 48 * 1024 * 1024


class _Dims(NamedTuple):
    d_model: int
    batch: int
    seq: int
    depth: int
    gla_heads: int
    gla_dk: int
    gla_dv: int
    gla_rank: int
    dsa_heads: int
    dsa_dim: int
    idx_heads: int
    idx_dim: int
    topk: int
    hg_heads: int
    hg_dk: int
    hg_dv: int
    d_ff: int
    chunk: int
    q_block: int


def _prod_dims():
    d = 2048
    return _Dims(d_model=d, batch=4, seq=2048, depth=2,
                 gla_heads=4, gla_dk=d // 2 // 4, gla_dv=d // 4, gla_rank=16,
                 dsa_heads=16, dsa_dim=128, idx_heads=8, idx_dim=64, topk=min(256, 2048 // 4),
                 hg_heads=d // 128, hg_dk=128, hg_dv=128, d_ff=4 * d, chunk=64, q_block=128)


def _in_sizes(dm):
    return (dm.gla_heads * dm.gla_dk, dm.gla_heads * dm.gla_dk, dm.gla_heads * dm.gla_dv,
            dm.gla_heads * dm.gla_dv, dm.gla_rank,
            dm.dsa_heads * dm.dsa_dim, dm.dsa_dim, dm.dsa_dim, dm.idx_heads * dm.idx_dim, dm.idx_dim,
            dm.idx_heads,
            dm.hg_heads * dm.hg_dk, dm.hg_heads * dm.hg_dk, dm.hg_heads * dm.hg_dv, dm.hg_heads * dm.hg_dv,
            dm.d_model, dm.d_model, dm.d_model)


_IN_NAMES = ("gla_q", "gla_k", "gla_v", "gla_g", "gla_a", "dsa_q", "dsa_k", "dsa_v", "idx_q", "idx_k",
             "idx_w", "hg_q", "hg_f", "hg_i", "hg_g", "gate_a", "gate_b", "gate_c")
_BIG_ORDER = ("hg_q", "hg_f", "hg_i", "hg_g", "gate_a", "gate_b", "gate_c", "dsa_q", "gla_v", "gla_g",
              "gla_q", "gla_k", "dsa_k", "dsa_v")
_SMALL_A = 0
_SMALL_IQ = LANES


def _tile(n, pref):
    t = min(n, pref)
    while n % t:
        t //= 2
    return t


def _layout(dm):
    sizes = dict(zip(_IN_NAMES, _in_sizes(dm)))
    src, off = {}, 0
    for name in _IN_NAMES:
        src[name] = (off, sizes[name])
        off += sizes[name]
    big, off = {}, 0
    for name in _BIG_ORDER:
        assert off % sizes[name] == 0, name
        big[name] = off
        off += sizes[name]
    return sizes, src, big, off


def _cparams(sem, vmem=VMEM_LIMIT):
    return pltpu.CompilerParams(dimension_semantics=sem, vmem_limit_bytes=vmem)


def _rmsnorm_kernel(x_ref, w_ref, o_ref):
    x = x_ref[...]
    ms = jnp.mean(x * x, axis=-1, keepdims=True)
    o_ref[...] = (x * lax.rsqrt(ms + NORM_EPS) * w_ref[...]).astype(o_ref.dtype)


def _rmsnorm(x, w, out_dtype):
    m, d = x.shape
    tm = _tile(m, 512)
    return pl.pallas_call(
        _rmsnorm_kernel,
        grid=(m // tm,),
        in_specs=[pl.BlockSpec((tm, d), lambda i: (i, 0)), pl.BlockSpec((1, d), lambda i: (0, 0))],
        out_specs=pl.BlockSpec((tm, d), lambda i: (i, 0)),
        out_shape=jax.ShapeDtypeStruct((m, d), out_dtype),
        compiler_params=_cparams(("parallel",)),
        name="rmsnorm",
    )(x, w.reshape(1, d))


def _matmul_kernel(x_ref, w_ref, o_ref):
    o_ref[...] = jnp.dot(x_ref[...], w_ref[...], preferred_element_type=F32).astype(o_ref.dtype)


def _matmul(x, w, out_dtype, tm_pref=1024, tn_pref=512, name="matmul"):
    m, k = x.shape
    n = w.shape[1]
    tm, tn = _tile(m, tm_pref), _tile(n, tn_pref)
    return pl.pallas_call(
        _matmul_kernel,
        grid=(m // tm, n // tn),
        in_specs=[pl.BlockSpec((tm, k), lambda i, j: (i, 0)), pl.BlockSpec((k, tn), lambda i, j: (0, j))],
        out_specs=pl.BlockSpec((tm, tn), lambda i, j: (i, j)),
        out_shape=jax.ShapeDtypeStruct((m, n), out_dtype),
        compiler_params=_cparams(("parallel", "arbitrary")),
        name=name,
    )(x, w)


def _matmul_residual_kernel(x_ref, w_ref, r_ref, o_ref):
    o_ref[...] = r_ref[...] + jnp.dot(x_ref[...], w_ref[...], preferred_element_type=F32)


def _matmul_residual(x, w, res, tm_pref=1024, tn_pref=512):
    m, k = x.shape
    n = w.shape[1]
    tm, tn = _tile(m, tm_pref), _tile(n, tn_pref)
    return pl.pallas_call(
        _matmul_residual_kernel,
        grid=(m // tm, n // tn),
        in_specs=[pl.BlockSpec((tm, k), lambda i, j: (i, 0)), pl.BlockSpec((k, tn), lambda i, j: (0, j)),
                  pl.BlockSpec((tm, tn), lambda i, j: (i, j))],
        out_specs=pl.BlockSpec((tm, tn), lambda i, j: (i, j)),
        out_shape=jax.ShapeDtypeStruct((m, n), F32),
        compiler_params=_cparams(("parallel", "arbitrary")),
        name="out_proj",
    )(x, w, res)


def _sigmoid(x):
    return 1.0 / (1.0 + jnp.exp(-x))


def _merge_kernel(oa_ref, ob_ref, oc_ref, wa_ref, wb_ref, wc_ref, ga_ref, gb_ref, gc_ref, o_ref):
    acc = _sigmoid(ga_ref[...].astype(F32)) * jnp.dot(oa_ref[...], wa_ref[...], preferred_element_type=F32)
    acc += _sigmoid(gb_ref[...].astype(F32)) * jnp.dot(ob_ref[...], wb_ref[...], preferred_element_type=F32)
    acc += _sigmoid(gc_ref[...].astype(F32)) * jnp.dot(oc_ref[...], wc_ref[...], preferred_element_type=F32)
    o_ref[...] = acc.astype(o_ref.dtype)


def _merge(o_gla, o_dsa, o_hg, w_gla, w_dsa, w_hg, big, gate_offs, d_model):
    m = o_gla.shape[0]
    tm, tn = _tile(m, 512), _tile(d_model, 512)
    o_spec = lambda a: pl.BlockSpec((tm, a.shape[1]), lambda i, j: (i, 0))
    w_spec = lambda a: pl.BlockSpec((a.shape[0], tn), lambda i, j: (0, j))

    def g_spec(off):
        assert off % tn == 0
        return pl.BlockSpec((tm, tn), lambda i, j: (i, off // tn + j))

    return pl.pallas_call(
        _merge_kernel,
        grid=(m // tm, d_model // tn),
        in_specs=[o_spec(o_gla), o_spec(o_dsa), o_spec(o_hg), w_spec(w_gla), w_spec(w_dsa), w_spec(w_hg),
                  g_spec(gate_offs[0]), g_spec(gate_offs[1]), g_spec(gate_offs[2])],
        out_specs=pl.BlockSpec((tm, tn), lambda i, j: (i, j)),
        out_shape=jax.ShapeDtypeStruct((m, d_model), BF16),
        compiler_params=_cparams(("parallel", "arbitrary")),
        name="merge",
    )(o_gla, o_dsa, o_hg, w_gla, w_dsa, w_hg, big, big, big)


def _mlp_kernel(h_ref, nw_ref, wu_ref, wd_ref, o_ref, u_ref):
    j = pl.program_id(1)

    @pl.when(j == 0)
    def _():
        x = h_ref[...]
        ms = jnp.mean(x * x, axis=-1, keepdims=True)
        u_ref[...] = (x * lax.rsqrt(ms + NORM_EPS) * nw_ref[...]).astype(BF16)
        o_ref[...] = x

    a = jnp.dot(u_ref[...], wu_ref[...], preferred_element_type=F32)
    a = jnp.square(jnp.maximum(a, 0.0)).astype(BF16)
    o_ref[...] += jnp.dot(a, wd_ref[...], preferred_element_type=F32)


def _mlp(h, norm_w, w_up, w_down):
    m, d = h.shape
    f = w_up.shape[1]
    tm, tf = _tile(m, 512), _tile(f, 512)
    return pl.pallas_call(
        _mlp_kernel,
        grid=(m // tm, f // tf),
        in_specs=[pl.BlockSpec((tm, d), lambda i, j: (i, 0)), pl.BlockSpec((1, d), lambda i, j: (0, 0)),
                  pl.BlockSpec((d, tf), lambda i, j: (0, j)), pl.BlockSpec((tf, d), lambda i, j: (j, 0))],
        out_specs=pl.BlockSpec((tm, d), lambda i, j: (i, 0)),
        out_shape=jax.ShapeDtypeStruct((m, d), F32),
        scratch_shapes=[pltpu.VMEM((tm, d), BF16)],
        compiler_params=_cparams(("parallel", "arbitrary")),
        name="mlp",
    )(h, norm_w.reshape(1, d), w_up, w_down)


def _chunk_head(qs, kk, vv, b, st_ref, head, safe, kb_ref):
    c, kdim = qs.shape
    row = lax.broadcasted_iota(jnp.int32, (c, c), 0)
    col = lax.broadcasted_iota(jnp.int32, (c, c), 1)
    b_last = b[c - 1:c, :]
    st = st_ref[head]
    o = lax.dot_general((qs * jnp.exp(b)).astype(BF16), st.astype(BF16), (((1,), (1,)), ((), ())),
                        preferred_element_type=F32)

    def factored(_):
        ref_row = b[c // 2 - 1:c // 2, :]
        qd = (qs * jnp.exp(b - ref_row)).astype(BF16)
        kd = (kk * jnp.exp(ref_row - b)).astype(BF16)
        return lax.dot_general(qd, kd, (((1,), (1,)), ((), ())), preferred_element_type=F32)

    def direct(_):
        kb_ref[0, :, 0:kdim] = kk
        kb_ref[1, :, 0:kdim] = b

        def body(s, att):
            k_row = kb_ref[0, pl.ds(s, 1), 0:kdim]
            b_row = kb_ref[1, pl.ds(s, 1), 0:kdim]
            w = jnp.sum(qs * k_row * jnp.exp(jnp.minimum(b - b_row, 0.0)), axis=-1, keepdims=True)
            return jnp.where(col == s, w, att)

        return lax.fori_loop(0, c, body, jnp.zeros((c, c), F32))

    att = lax.cond(safe, factored, direct, 0)
    att = jnp.where(col <= row, att, 0.0)
    o = o + jnp.dot(att.astype(BF16), vv.astype(BF16), preferred_element_type=F32)
    k_dec = (kk * jnp.exp(b_last - b)).astype(BF16)
    upd = lax.dot_general(vv.astype(BF16), k_dec, (((0,), (0,)), ((), ())), preferred_element_type=F32)
    st_ref[head] = st * jnp.exp(b_last) + upd
    return o


def _cumsum_rows(x):
    c = x.shape[0]
    row = lax.broadcasted_iota(jnp.int32, (c, c), 0)
    col = lax.broadcasted_iota(jnp.int32, (c, c), 1)
    tri = jnp.where(col <= row, 1.0, 0.0).astype(F32)
    return jnp.dot(tri, x, preferred_element_type=F32, precision=HIGHEST)


def _log_sigmoid(x):
    return jnp.minimum(x, 0.0) - jnp.log(1.0 + jnp.exp(-jnp.abs(x)))


def _head_rmsnorm(o, w):
    ms = jnp.mean(o * o, axis=-1, keepdims=True)
    return o * lax.rsqrt(ms + NORM_EPS) * w


def _gla_kernel(q_ref, k_ref, v_ref, g_ref, a_ref, w2_ref, b2_ref, nw_ref, o_ref, st_ref, kb_ref, *,
                heads, dk, dv):
    @pl.when(pl.program_id(1) == 0)
    def _():
        st_ref[...] = jnp.zeros_like(st_ref)

    z = jnp.dot(a_ref[...], w2_ref[...], preferred_element_type=F32, precision=HIGHEST) + b2_ref[...]
    b_all = _cumsum_rows(_log_sigmoid(z) * (1.0 / GLA_GATE_NORMALIZER))
    c = b_all.shape[0]
    safe = jnp.max(-b_all[c - 1:c, :]) <= SAFE_CHUNK_DECAY
    for h in range(heads):
        qs = q_ref[:, h * dk:(h + 1) * dk].astype(F32) * (dk ** -0.5)
        kk = k_ref[:, h * dk:(h + 1) * dk].astype(F32)
        vv = v_ref[:, h * dv:(h + 1) * dv]
        o = _chunk_head(qs, kk, vv, b_all[:, h * dk:(h + 1) * dk], st_ref, h, safe, kb_ref)
        g = g_ref[:, h * dv:(h + 1) * dv].astype(F32)
        o = _head_rmsnorm(o, nw_ref[...]) * (g * _sigmoid(g))
        o_ref[:, h * dv:(h + 1) * dv] = o.astype(o_ref.dtype)


def _gla(big, small, w2p, b2, onorm_w, dm, offs):
    m = big.shape[0]
    c = dm.chunk
    nc = dm.seq // c
    hk, hv = dm.gla_heads * dm.gla_dk, dm.gla_heads * dm.gla_dv
    row = lambda bi, ci: bi * nc + ci

    def seg(name, width):
        assert offs[name] % width == 0
        blk = offs[name] // width
        return pl.BlockSpec((c, width), lambda bi, ci: (row(bi, ci), blk))

    const = lambda shape: pl.BlockSpec(shape, lambda bi, ci: (0, 0))
    return pl.pallas_call(
        functools.partial(_gla_kernel, heads=dm.gla_heads, dk=dm.gla_dk, dv=dm.gla_dv),
        grid=(dm.batch, nc),
        in_specs=[seg("gla_q", hk), seg("gla_k", hk), seg("gla_v", hv), seg("gla_g", hv),
                  pl.BlockSpec((c, LANES), lambda bi, ci: (row(bi, ci), _SMALL_A // LANES)),
                  const((LANES, hk)), const((1, hk)), const((1, dm.gla_dv))],
        out_specs=pl.BlockSpec((c, hv), lambda bi, ci: (row(bi, ci), 0)),
        out_shape=jax.ShapeDtypeStruct((m, hv), BF16),
        scratch_shapes=[pltpu.VMEM((dm.gla_heads, dm.gla_dv, dm.gla_dk), F32),
                        pltpu.VMEM((2, c, dm.gla_dk), F32)],
        compiler_params=_cparams(("parallel", "arbitrary")),
        name="gla",
    )(big, big, big, big, small, w2p, b2.reshape(1, hk), onorm_w.reshape(1, dm.gla_dv))


def _hgrn_kernel(q_ref, f_ref, i_ref, g_ref, lbp_ref, nw_ref, o_ref, st_ref, kb_ref, *, layer, heads, dk, dv):
    @pl.when(pl.program_id(1) == 0)
    def _():
        st_ref[...] = jnp.zeros_like(st_ref)

    lbp = lbp_ref[...]
    e = jnp.exp(lbp - jnp.max(lbp, axis=0, keepdims=True))
    p = e / jnp.sum(e, axis=0, keepdims=True)
    lb = jnp.zeros_like(p[0:1, :])
    for j in range(1, layer + 1):
        lb = lb + p[j:j + 1, :]
    fl = f_ref[...].astype(F32)
    f_gate = lb + (1.0 - lb) * _sigmoid(fl)
    b_all = _cumsum_rows(jnp.log(jnp.maximum(f_gate, F_MIN)))
    k_all = (1.0 - lb) * _sigmoid(-fl)
    c = b_all.shape[0]
    safe = jnp.max(-b_all[c - 1:c, :]) <= SAFE_CHUNK_DECAY
    for h in range(heads):
        q = q_ref[:, h * dk:(h + 1) * dk].astype(F32)
        qs = q * _sigmoid(q) * (dk ** -0.5)
        vv = i_ref[:, h * dv:(h + 1) * dv]
        o = _chunk_head(qs, k_all[:, h * dk:(h + 1) * dk], vv, b_all[:, h * dk:(h + 1) * dk], st_ref, h, safe,
                        kb_ref)
        g = g_ref[:, h * dv:(h + 1) * dv].astype(F32)
        o = _head_rmsnorm(o, nw_ref[...]) * _sigmoid(g)
        o_ref[:, h * dv:(h + 1) * dv] = o.astype(o_ref.dtype)


def _hgrn(big, lower_bounds, onorm_w, layer, dm, offs):
    m = big.shape[0]
    c = dm.chunk
    nc = dm.seq // c
    hk, hv = dm.hg_heads * dm.hg_dk, dm.hg_heads * dm.hg_dv
    row = lambda bi, ci: bi * nc + ci

    def seg(name, width):
        assert offs[name] % width == 0
        blk = offs[name] // width
        return pl.BlockSpec((c, width), lambda bi, ci: (row(bi, ci), blk))

    const = lambda shape: pl.BlockSpec(shape, lambda bi, ci: (0, 0))
    return pl.pallas_call(
        functools.partial(_hgrn_kernel, layer=layer, heads=dm.hg_heads, dk=dm.hg_dk, dv=dm.hg_dv),
        grid=(dm.batch, nc),
        in_specs=[seg("hg_q", hk), seg("hg_f", hk), seg("hg_i", hv), seg("hg_g", hv),
                  const((dm.depth, hk)), const((1, dm.hg_dv))],
        out_specs=pl.BlockSpec((c, hv), lambda bi, ci: (row(bi, ci), 0)),
        out_shape=jax.ShapeDtypeStruct((m, hv), BF16),
        scratch_shapes=[pltpu.VMEM((dm.hg_heads, dm.hg_dv, dm.hg_dk), F32),
                        pltpu.VMEM((2, c, dm.hg_dk), F32)],
        compiler_params=_cparams(("parallel", "arbitrary")),
        name="hgrn",
    )(big, big, big, big, lower_bounds, onorm_w.reshape(1, dm.hg_dv))


def _rope_table_kernel(pos_ref, f_ref, sgn_ref, cos_ref, sin_ref):
    ang = pos_ref[...].astype(F32) * f_ref[...]
    cos_ref[...] = jnp.cos(ang)
    sin_ref[...] = jnp.sin(ang) * sgn_ref[...]


def _rope_tables(pos_col, inv_freq_lanes, sign_lanes):
    m = pos_col.shape[0]
    tm = _tile(m, 512)
    lane = pl.BlockSpec((1, LANES), lambda i: (0, 0))
    tab = pl.BlockSpec((tm, LANES), lambda i: (i, 0))
    return pl.pallas_call(
        _rope_table_kernel,
        grid=(m // tm,),
        in_specs=[pl.BlockSpec((tm, 1), lambda i: (i, 0)), lane, lane],
        out_specs=[tab, tab],
        out_shape=[jax.ShapeDtypeStruct((m, LANES), F32)] * 2,
        compiler_params=_cparams(("parallel",)),
        name="rope_tables",
    )(pos_col, inv_freq_lanes, sign_lanes)


def _swap_halves(x, half):
    n = x.shape[-1]
    if 2 * half == n:
        return pltpu.roll(x, half, axis=1)
    lane = lax.broadcasted_iota(jnp.int32, x.shape, 1)
    return jnp.where((lane & half) == 0, pltpu.roll(x, n - half, axis=1), pltpu.roll(x, half, axis=1))


def _rope_apply_kernel(q_ref, k_ref, s_ref, c1_ref, s1_ref, c2_ref, s2_ref, qo_ref, ko_ref, iqo_ref,
                       iko_ref, iwo_ref, *, heads, dim, idx_heads, idx_dim):
    c1, s1, c2, s2 = c1_ref[...], s1_ref[...], c2_ref[...], s2_ref[...]
    q_scale = dim ** -0.5
    for h in range(heads):
        x = q_ref[:, h * dim:(h + 1) * dim].astype(F32)
        qo_ref[:, h * dim:(h + 1) * dim] = ((x * c1 + _swap_halves(x, dim // 2) * s1) * q_scale).astype(BF16)
    x = k_ref[...].astype(F32)
    ko_ref[...] = (x * c1 + _swap_halves(x, dim // 2) * s1).astype(BF16)
    iq_scale = idx_dim ** -0.5
    for g in range(idx_heads * idx_dim // LANES):
        x = s_ref[:, _SMALL_IQ + g * LANES:_SMALL_IQ + (g + 1) * LANES]
        iqo_ref[:, g * LANES:(g + 1) * LANES] = (x * c2 + _swap_halves(x, idx_dim // 2) * s2) * iq_scale
    off = _SMALL_IQ + idx_heads * idx_dim
    x = s_ref[:, off:off + LANES]
    r = x * c2 + _swap_halves(x, idx_dim // 2) * s2
    lane = lax.broadcasted_iota(jnp.int32, x.shape, 1)
    iko_ref[...] = jnp.where(lane < idx_dim, r, pltpu.roll(r, idx_dim, axis=1))
    iwo_ref[...] = jnp.where(lane < idx_heads, pltpu.roll(x, LANES - idx_dim, axis=1), 0.0) * (idx_heads ** -0.5)


def _rope_apply(big, small, tabs, dm, offs):
    m = big.shape[0]
    tm = _tile(m, 256)
    hd = dm.dsa_heads * dm.dsa_dim
    iq = dm.idx_heads * dm.idx_dim
    assert dm.dsa_dim == LANES and 2 * dm.idx_dim == LANES and dm.idx_heads <= dm.idx_dim
    assert offs["dsa_q"] % hd == 0 and offs["dsa_k"] % dm.dsa_dim == 0
    rows = lambda w, blk=0: pl.BlockSpec((tm, w), lambda i: (i, blk))
    return pl.pallas_call(
        functools.partial(_rope_apply_kernel, heads=dm.dsa_heads, dim=dm.dsa_dim, idx_heads=dm.idx_heads,
                          idx_dim=dm.idx_dim),
        grid=(m // tm,),
        in_specs=[rows(hd, offs["dsa_q"] // hd), rows(dm.dsa_dim, offs["dsa_k"] // dm.dsa_dim),
                  rows(small.shape[1]), rows(LANES), rows(LANES), rows(LANES), rows(LANES)],
        out_specs=[rows(hd), rows(dm.dsa_dim), rows(iq), rows(LANES), rows(LANES)],
        out_shape=[jax.ShapeDtypeStruct((m, hd), BF16), jax.ShapeDtypeStruct((m, dm.dsa_dim), BF16),
                   jax.ShapeDtypeStruct((m, iq), F32), jax.ShapeDtypeStruct((m, LANES), F32),
                   jax.ShapeDtypeStruct((m, LANES), F32)],
        compiler_params=_cparams(("parallel",)),
        name="rope_apply",
    )(big, big, small, *tabs)


def _ind(mask):
    return jnp.where(mask, 1.0, 0.0)


def _row_sum(x):
    return jnp.sum(x, axis=-1, keepdims=True)


def _dsa_kernel(q_ref, iq_ref, iw_ref, k_ref, v_ref, ik_ref, o_ref, *, heads, dim, idx_heads, idx_dim, topk):
    tq = q_ref.shape[0]
    t = k_ref.shape[0]
    ik = ik_ref[...]
    lane = lax.broadcasted_iota(jnp.int32, (tq, LANES), 1)
    score = jnp.zeros((tq, t), F32)
    for h in range(idx_heads):
        g = (h * idx_dim) // LANES
        lo = (h * idx_dim) % LANES
        x = iq_ref[:, g * LANES:(g + 1) * LANES]
        x = jnp.where(lane >= lo, jnp.where(lane < lo + idx_dim, x, 0.0), 0.0)
        rel = lax.dot_general(x, ik, (((1,), (1,)), ((), ())), preferred_element_type=F32, precision=HIGHEST)
        score = score + iw_ref[:, h:h + 1] * jnp.maximum(rel, 0.0)
    qpos = pl.program_id(1) * tq + lax.broadcasted_iota(jnp.int32, (tq, 1), 0)
    kpos = lax.broadcasted_iota(jnp.int32, (1, t), 1)
    allowed = kpos <= qpos
    score = jnp.where(allowed, score + 0.0, MASK_VALUE)
    key = pltpu.bitcast(score, jnp.int32)
    key = jnp.where(key < 0, key ^ jnp.int32(0x7FFFFFFF), key)

    def thr_body(it, prefix):
        trial = prefix | lax.shift_left(jnp.int32(1), 31 - it)
        cnt = _row_sum(_ind(key >= (trial ^ jnp.int32(INT_MIN))))
        return jnp.where(cnt >= topk, trial, prefix)

    thr = lax.fori_loop(0, 32, thr_body, jnp.zeros((tq, 1), jnp.int32)) ^ jnp.int32(INT_MIN)
    above = _ind(key > thr)
    need = topk - _row_sum(above)
    tie = key == thr

    nbits = (t - 1).bit_length()

    def tie_body(it, j):
        trial = j | lax.shift_left(jnp.int32(1), nbits - 1 - it)
        cnt = _row_sum(jnp.where(tie, _ind(kpos < trial), 0.0))
        return jnp.where(cnt < need, trial, j)

    j_last = lax.fori_loop(0, nbits, tie_body, jnp.zeros((tq, 1), jnp.int32))
    selected = jnp.where(tie, _ind(kpos <= j_last), above)
    valid = jnp.where(allowed, selected, 0.0) > 0.0

    k = k_ref[...]
    v = v_ref[...]
    for h in range(heads):
        s = lax.dot_general(q_ref[:, h * dim:(h + 1) * dim], k, (((1,), (1,)), ((), ())),
                            preferred_element_type=F32)
        s = jnp.where(valid, s, MASK_VALUE)
        p = jnp.exp(s - jnp.max(s, axis=-1, keepdims=True))
        inv = 1.0 / jnp.sum(p, axis=-1, keepdims=True)
        o = jnp.dot(p.astype(BF16), v, preferred_element_type=F32) * inv
        o_ref[:, h * dim:(h + 1) * dim] = o.astype(o_ref.dtype)


def _dsa(q_rot, k_rot, big, iq_rot, ik_rot, iw, dm, offs):
    m = q_rot.shape[0]
    tq, t = dm.q_block, dm.seq
    nq = t // tq
    hd = dm.dsa_heads * dm.dsa_dim
    assert offs["dsa_v"] % dm.dsa_dim == 0
    v_blk = offs["dsa_v"] // dm.dsa_dim
    qrow = lambda w: pl.BlockSpec((tq, w), lambda bi, qi: (bi * nq + qi, 0))
    krow = lambda w, blk=0: pl.BlockSpec((t, w), lambda bi, qi: (bi, blk))
    return pl.pallas_call(
        functools.partial(_dsa_kernel, heads=dm.dsa_heads, dim=dm.dsa_dim, idx_heads=dm.idx_heads,
                          idx_dim=dm.idx_dim, topk=dm.topk),
        grid=(dm.batch, nq),
        in_specs=[qrow(hd), qrow(iq_rot.shape[1]), qrow(LANES), krow(dm.dsa_dim), krow(dm.dsa_dim, v_blk),
                  krow(LANES)],
        out_specs=qrow(hd),
        out_shape=jax.ShapeDtypeStruct((m, hd), BF16),
        compiler_params=_cparams(("parallel", "arbitrary")),
        name="dsa",
    )(q_rot, iq_rot, iw, k_rot, big, ik_rot)


def _layer_weights(l, dm, w_in, gla_gate_w2):
    _, src, big_off, big_n = _layout(dm)
    w = w_in[l]
    col = lambda name: w[:, src[name][0]:src[name][0] + src[name][1]]
    n_pad = -big_n % 512
    parts = [col(name) for name in _BIG_ORDER]
    if n_pad:
        parts.append(jnp.zeros((dm.d_model, n_pad), w.dtype))
    w_big = jnp.concatenate(parts, axis=1).astype(BF16)
    zeros = lambda n: jnp.zeros((dm.d_model, n), w.dtype)
    w_small = jnp.concatenate(
        [col("gla_a"), zeros(LANES - dm.gla_rank), col("idx_q"), col("idx_k"), col("idx_w"),
         zeros(LANES - dm.idx_dim - dm.idx_heads)], axis=1).astype(BF16)
    w2p = jnp.concatenate([gla_gate_w2[l], jnp.zeros((LANES - dm.gla_rank, gla_gate_w2.shape[2]), F32)], axis=0)
    return w_big, w_small, w2p, big_off


def _forward(dm, x, positions, norm1_w, w_in, gla_gate_w2, gla_gate_b, gla_onorm_w, hgrn_lower_bounds,
             hgrn_onorm_w, w_branch_gla, w_branch_dsa, w_branch_hgrn, w_out, norm2_w, w_mlp_up, w_mlp_down,
             final_norm_w):
    m = dm.batch * dm.seq
    h = x.reshape(m, dm.d_model)
    def lanes(d):
        inv = ROPE_THETA ** (-jnp.arange(0, d, 2, dtype=F32) / d)
        reps = LANES // d
        f = jnp.tile(jnp.concatenate([inv, inv]), reps).reshape(1, LANES)
        sgn = jnp.tile(jnp.concatenate([-jnp.ones(d // 2, F32), jnp.ones(d // 2, F32)]), reps).reshape(1, LANES)
        return f, sgn

    pos_col = positions.reshape(m, 1)
    tabs = _rope_tables(pos_col, *lanes(dm.dsa_dim)) + _rope_tables(pos_col, *lanes(dm.idx_dim))

    for l in range(dm.depth):
        w_big, w_small, w2p, offs = _layer_weights(l, dm, w_in, gla_gate_w2)
        u = _rmsnorm(h, norm1_w[l], BF16)
        big = _matmul(u, w_big, BF16, name="in_proj")
        small = _matmul(u, w_small, F32, tn_pref=256, name="in_proj_small")
        o_gla = _gla(big, small, w2p, gla_gate_b[l], gla_onorm_w[l], dm, offs)
        o_hg = _hgrn(big, hgrn_lower_bounds, hgrn_onorm_w[l], l, dm, offs)
        q_rot, k_rot, iq_rot, ik_rot, iw = _rope_apply(big, small, tabs, dm, offs)
        o_dsa = _dsa(q_rot, k_rot, big, iq_rot, ik_rot, iw, dm, offs)
        merged = _merge(o_gla, o_dsa, o_hg, w_branch_gla[l].astype(BF16), w_branch_dsa[l].astype(BF16),
                        w_branch_hgrn[l].astype(BF16), big,
                        (offs["gate_a"], offs["gate_b"], offs["gate_c"]), dm.d_model)
        h = _matmul_residual(merged, w_out[l].astype(BF16), h)
        h = _mlp(h, norm2_w[l], w_mlp_up[l].astype(BF16), w_mlp_down[l].astype(BF16))
    out = _rmsnorm(h, final_norm_w, F32)
    return out.reshape(dm.batch, dm.seq, dm.d_model)


def kernel(x, positions, norm1_w, w_in, gla_gate_w2, gla_gate_b, gla_onorm_w, hgrn_lower_bounds, hgrn_onorm_w,
           w_branch_gla, w_branch_dsa, w_branch_hgrn, w_out, norm2_w, w_mlp_up, w_mlp_down, final_norm_w):
    return _forward(_prod_dims(), x, positions, norm1_w, w_in, gla_gate_w2, gla_gate_b, gla_onorm_w,
                    hgrn_lower_bounds, hgrn_onorm_w, w_branch_gla, w_branch_dsa, w_branch_hgrn, w_out, norm2_w,
                    w_mlp_up, w_mlp_down, final_norm_w)
```

```python
import functools
from typing import NamedTuple

import jax
import jax.numpy as jnp
from jax import lax
from jax.experimental import pallas as pl
from jax.experimental.pallas import tpu as pltpu

F32 = jnp.float32
BF16 = jnp.bfloat16
HIGHEST = lax.Precision.HIGHEST

ROPE_THETA = 10000.0
NORM_EPS = 1e-6
MASK_VALUE = -1e30
F_MIN = 1e-12
GLA_GATE_NORMALIZER = 16.0
LANES = 128
INT_MIN = -2 ** 31
SAFE_CHUNK_DECAY = 150.0
VMEM_LIMIT = 48 * 1024 * 1024


class _Dims(NamedTuple):
    d_model: int
    batch: int
    seq: int
    depth: int
    gla_heads: int
    gla_dk: int
    gla_dv: int
    gla_rank: int
    dsa_heads: int
    dsa_dim: int
    idx_heads: int
    idx_dim: int
    topk: int
    hg_heads: int
    hg_dk: int
    hg_dv: int
    d_ff: int
    chunk: int
    q_block: int


def _prod_dims():
    d = 2048
    return _Dims(d_model=d, batch=4, seq=2048, depth=2,
                 gla_heads=4, gla_dk=d // 2 // 4, gla_dv=d // 4, gla_rank=16,
                 dsa_heads=16, dsa_dim=128, idx_heads=8, idx_dim=64, topk=min(256, 2048 // 4),
                 hg_heads=d // 128, hg_dk=128, hg_dv=128, d_ff=4 * d, chunk=64, q_block=256)


def _in_sizes(dm):
    return (dm.gla_heads * dm.gla_dk, dm.gla_heads * dm.gla_dk, dm.gla_heads * dm.gla_dv,
            dm.gla_heads * dm.gla_dv, dm.gla_rank,
            dm.dsa_heads * dm.dsa_dim, dm.dsa_dim, dm.dsa_dim, dm.idx_heads * dm.idx_dim, dm.idx_dim,
            dm.idx_heads,
            dm.hg_heads * dm.hg_dk, dm.hg_heads * dm.hg_dk, dm.hg_heads * dm.hg_dv, dm.hg_heads * dm.hg_dv,
            dm.d_model, dm.d_model, dm.d_model)


_IN_NAMES = ("gla_q", "gla_k", "gla_v", "gla_g", "gla_a", "dsa_q", "dsa_k", "dsa_v", "idx_q", "idx_k",
             "idx_w", "hg_q", "hg_f", "hg_i", "hg_g", "gate_a", "gate_b", "gate_c")
_BIG_ORDER = ("hg_q", "hg_f", "hg_i", "hg_g", "gate_a", "gate_b", "gate_c", "dsa_q", "gla_v", "gla_g",
              "gla_q", "gla_k", "dsa_k", "dsa_v")
_SMALL_A = 0
_SMALL_IQ = LANES


def _tile(n, pref):
    t = min(n, pref)
    while n % t:
        t //= 2
    return t


def _layout(dm):
    sizes = dict(zip(_IN_NAMES, _in_sizes(dm)))
    src, off = {}, 0
    for name in _IN_NAMES:
        src[name] = (off, sizes[name])
        off += sizes[name]
    big, off = {}, 0
    for name in _BIG_ORDER:
        assert off % sizes[name] == 0, name
        big[name] = off
        off += sizes[name]
    return sizes, src, big, off


def _cparams(sem, vmem=VMEM_LIMIT):
    return pltpu.CompilerParams(dimension_semantics=sem, vmem_limit_bytes=vmem)


def _rmsnorm_kernel(x_ref, w_ref, o_ref):
    x = x_ref[...]
    ms = jnp.mean(x * x, axis=-1, keepdims=True)
    o_ref[...] = (x * lax.rsqrt(ms + NORM_EPS) * w_ref[...]).astype(o_ref.dtype)


def _rmsnorm(x, w, out_dtype):
    m, d = x.shape
    tm = _tile(m, 512)
    return pl.pallas_call(
        _rmsnorm_kernel,
        grid=(m // tm,),
        in_specs=[pl.BlockSpec((tm, d), lambda i: (i, 0)), pl.BlockSpec((1, d), lambda i: (0, 0))],
        out_specs=pl.BlockSpec((tm, d), lambda i: (i, 0)),
        out_shape=jax.ShapeDtypeStruct((m, d), out_dtype),
        compiler_params=_cparams(("parallel",)),
        name="rmsnorm",
    )(x, w.reshape(1, d))


def _matmul_kernel(x_ref, w_ref, o_ref):
    o_ref[...] = jnp.dot(x_ref[...], w_ref[...], preferred_element_type=F32).astype(o_ref.dtype)


def _matmul(x, w, out_dtype, tm_pref=1024, tn_pref=512, name="matmul"):
    m, k = x.shape
    n = w.shape[1]
    tm, tn = _tile(m, tm_pref), _tile(n, tn_pref)
    return pl.pallas_call(
        _matmul_kernel,
        grid=(m // tm, n // tn),
        in_specs=[pl.BlockSpec((tm, k), lambda i, j: (i, 0)), pl.BlockSpec((k, tn), lambda i, j: (0, j))],
        out_specs=pl.BlockSpec((tm, tn), lambda i, j: (i, j)),
        out_shape=jax.ShapeDtypeStruct((m, n), out_dtype),
        compiler_params=_cparams(("parallel", "arbitrary")),
        name=name,
    )(x, w)


def _matmul_residual_kernel(x_ref, w_ref, r_ref, o_ref):
    o_ref[...] = r_ref[...] + jnp.dot(x_ref[...], w_ref[...], preferred_element_type=F32)


def _matmul_residual(x, w, res, tm_pref=1024, tn_pref=512):
    m, k = x.shape
    n = w.shape[1]
    tm, tn = _tile(m, tm_pref), _tile(n, tn_pref)
    return pl.pallas_call(
        _matmul_residual_kernel,
        grid=(m // tm, n // tn),
        in_specs=[pl.BlockSpec((tm, k), lambda i, j: (i, 0)), pl.BlockSpec((k, tn), lambda i, j: (0, j)),
                  pl.BlockSpec((tm, tn), lambda i, j: (i, j))],
        out_specs=pl.BlockSpec((tm, tn), lambda i, j: (i, j)),
        out_shape=jax.ShapeDtypeStruct((m, n), F32),
        compiler_params=_cparams(("parallel", "arbitrary")),
        name="out_proj",
    )(x, w, res)


def _sigmoid(x):
    return 1.0 / (1.0 + jnp.exp(-x))


def _merge_kernel(oa_ref, ob_ref, oc_ref, wa_ref, wb_ref, wc_ref, ga_ref, gb_ref, gc_ref, o_ref):
    acc = _sigmoid(ga_ref[...].astype(F32)) * jnp.dot(oa_ref[...], wa_ref[...], preferred_element_type=F32)
    acc += _sigmoid(gb_ref[...].astype(F32)) * jnp.dot(ob_ref[...], wb_ref[...], preferred_element_type=F32)
    acc += _sigmoid(gc_ref[...].astype(F32)) * jnp.dot(oc_ref[...], wc_ref[...], preferred_element_type=F32)
    o_ref[...] = acc.astype(o_ref.dtype)


def _merge(o_gla, o_dsa, o_hg, w_gla, w_dsa, w_hg, big, gate_offs, d_model):
    m = o_gla.shape[0]
    tm, tn = _tile(m, 512), _tile(d_model, 512)
    o_spec = lambda a: pl.BlockSpec((tm, a.shape[1]), lambda i, j: (i, 0))
    w_spec = lambda a: pl.BlockSpec((a.shape[0], tn), lambda i, j: (0, j))

    def g_spec(off):
        assert off % tn == 0
        return pl.BlockSpec((tm, tn), lambda i, j: (i, off // tn + j))

    return pl.pallas_call(
        _merge_kernel,
        grid=(m // tm, d_model // tn),
        in_specs=[o_spec(o_gla), o_spec(o_dsa), o_spec(o_hg), w_spec(w_gla), w_spec(w_dsa), w_spec(w_hg),
                  g_spec(gate_offs[0]), g_spec(gate_offs[1]), g_spec(gate_offs[2])],
        out_specs=pl.BlockSpec((tm, tn), lambda i, j: (i, j)),
        out_shape=jax.ShapeDtypeStruct((m, d_model), BF16),
        compiler_params=_cparams(("parallel", "arbitrary")),
        name="merge",
    )(o_gla, o_dsa, o_hg, w_gla, w_dsa, w_hg, big, big, big)


def _mlp_kernel(h_ref, nw_ref, wu_ref, wd_ref, o_ref, u_ref):
    j = pl.program_id(1)

    @pl.when(j == 0)
    def _():
        x = h_ref[...]
        ms = jnp.mean(x * x, axis=-1, keepdims=True)
        u_ref[...] = (x * lax.rsqrt(ms + NORM_EPS) * nw_ref[...]).astype(BF16)
        o_ref[...] = x

    a = jnp.dot(u_ref[...], wu_ref[...], preferred_element_type=F32)
    a = jnp.square(jnp.maximum(a, 0.0)).astype(BF16)
    o_ref[...] += jnp.dot(a, wd_ref[...], preferred_element_type=F32)


def _mlp(h, norm_w, w_up, w_down):
    m, d = h.shape
    f = w_up.shape[1]
    tm, tf = _tile(m, 512), _tile(f, 512)
    return pl.pallas_call(
        _mlp_kernel,
        grid=(m // tm, f // tf),
        in_specs=[pl.BlockSpec((tm, d), lambda i, j: (i, 0)), pl.BlockSpec((1, d), lambda i, j: (0, 0)),
                  pl.BlockSpec((d, tf), lambda i, j: (0, j)), pl.BlockSpec((tf, d), lambda i, j: (j, 0))],
        out_specs=pl.BlockSpec((tm, d), lambda i, j: (i, 0)),
        out_shape=jax.ShapeDtypeStruct((m, d), F32),
        scratch_shapes=[pltpu.VMEM((tm, d), BF16)],
        compiler_params=_cparams(("parallel", "arbitrary")),
        name="mlp",
    )(h, norm_w.reshape(1, d), w_up, w_down)


def _chunk_head(qs, kk, vv, b, st_ref, head, factored, kb_ref):
    c, kdim = qs.shape
    row = lax.broadcasted_iota(jnp.int32, (c, c), 0)
    col = lax.broadcasted_iota(jnp.int32, (c, c), 1)
    b_last = b[c - 1:c, :]
    st = st_ref[head]
    nt = (((1,), (1,)), ((), ()))
    if factored:
        ref_row = b[c // 2 - 1:c // 2, :]
        qd = qs * jnp.exp(b - ref_row)
        kd = kk * jnp.exp(ref_row - b)
        q_in = qd * jnp.exp(ref_row)
        k_dec = kd * jnp.exp(b_last - ref_row)
        att = lax.dot_general(qd.astype(BF16), kd.astype(BF16), nt, preferred_element_type=F32)
    else:
        q_in = qs * jnp.exp(b)
        k_dec = kk * jnp.exp(b_last - b)
        kb_ref[0, :, 0:kdim] = kk
        kb_ref[1, :, 0:kdim] = b

        def body(s, att):
            k_row = kb_ref[0, pl.ds(s, 1), 0:kdim]
            b_row = kb_ref[1, pl.ds(s, 1), 0:kdim]
            w = jnp.sum(qs * k_row * jnp.exp(jnp.minimum(b - b_row, 0.0)), axis=-1, keepdims=True)
            return jnp.where(col == s, w, att)

        att = lax.fori_loop(0, c, body, jnp.zeros((c, c), F32))
    att = jnp.where(col <= row, att, 0.0)
    o = lax.dot_general(q_in.astype(BF16), st.astype(BF16), nt, preferred_element_type=F32)
    o = o + jnp.dot(att.astype(BF16), vv.astype(BF16), preferred_element_type=F32)
    upd = lax.dot_general(vv.astype(BF16), k_dec.astype(BF16), (((0,), (0,)), ((), ())),
                          preferred_element_type=F32)
    st_ref[head] = st * jnp.exp(b_last) + upd
    return o


def _cumsum_rows(x):
    c = x.shape[0]
    row = lax.broadcasted_iota(jnp.int32, (c, c), 0)
    col = lax.broadcasted_iota(jnp.int32, (c, c), 1)
    tri = jnp.where(col <= row, 1.0, 0.0).astype(F32)
    return jnp.dot(tri, x, preferred_element_type=F32, precision=HIGHEST)


def _log_sigmoid(x):
    return jnp.minimum(x, 0.0) - jnp.log(1.0 + jnp.exp(-jnp.abs(x)))


def _head_rmsnorm(o, w):
    ms = jnp.mean(o * o, axis=-1, keepdims=True)
    return o * lax.rsqrt(ms + NORM_EPS) * w


def _gla_kernel(q_ref, k_ref, v_ref, g_ref, a_ref, w2_ref, b2_ref, nw_ref, o_ref, st_ref, kb_ref, *,
                heads, dk, dv):
    @pl.when(pl.program_id(1) == 0)
    def _():
        st_ref[...] = jnp.zeros_like(st_ref)

    z = jnp.dot(a_ref[...], w2_ref[...], preferred_element_type=F32, precision=HIGHEST) + b2_ref[...]
    b_all = _cumsum_rows(_log_sigmoid(z) * (1.0 / GLA_GATE_NORMALIZER))
    c = b_all.shape[0]

    def step(factored):
        for h in range(heads):
            qs = q_ref[:, h * dk:(h + 1) * dk].astype(F32) * (dk ** -0.5)
            kk = k_ref[:, h * dk:(h + 1) * dk].astype(F32)
            vv = v_ref[:, h * dv:(h + 1) * dv]
            o = _chunk_head(qs, kk, vv, b_all[:, h * dk:(h + 1) * dk], st_ref, h, factored, kb_ref)
            g = g_ref[:, h * dv:(h + 1) * dv].astype(F32)
            o = _head_rmsnorm(o, nw_ref[...]) * (g * _sigmoid(g))
            o_ref[:, h * dv:(h + 1) * dv] = o.astype(o_ref.dtype)

    safe = jnp.max(-b_all[c - 1:c, :]) <= SAFE_CHUNK_DECAY
    lax.cond(safe, functools.partial(step, True), functools.partial(step, False))


def _gla(big, small, w2p, b2, onorm_w, dm, offs):
    m = big.shape[0]
    c = dm.chunk
    nc = dm.seq // c
    hk, hv = dm.gla_heads * dm.gla_dk, dm.gla_heads * dm.gla_dv
    row = lambda bi, ci: bi * nc + ci

    def seg(name, width):
        assert offs[name] % width == 0
        blk = offs[name] // width
        return pl.BlockSpec((c, width), lambda bi, ci: (row(bi, ci), blk))

    const = lambda shape: pl.BlockSpec(shape, lambda bi, ci: (0, 0))
    return pl.pallas_call(
        functools.partial(_gla_kernel, heads=dm.gla_heads, dk=dm.gla_dk, dv=dm.gla_dv),
        grid=(dm.batch, nc),
        in_specs=[seg("gla_q", hk), seg("gla_k", hk), seg("gla_v", hv), seg("gla_g", hv),
                  pl.BlockSpec((c, LANES), lambda bi, ci: (row(bi, ci), _SMALL_A // LANES)),
                  const((LANES, hk)), const((1, hk)), const((1, dm.gla_dv))],
        out_specs=pl.BlockSpec((c, hv), lambda bi, ci: (row(bi, ci), 0)),
        out_shape=jax.ShapeDtypeStruct((m, hv), BF16),
        scratch_shapes=[pltpu.VMEM((dm.gla_heads, dm.gla_dv, dm.gla_dk), F32),
                        pltpu.VMEM((2, c, dm.gla_dk), F32)],
        compiler_params=_cparams(("parallel", "arbitrary")),
        name="gla",
    )(big, big, big, big, small, w2p, b2.reshape(1, hk), onorm_w.reshape(1, dm.gla_dv))


def _hgrn_kernel(q_ref, f_ref, i_ref, g_ref, lbp_ref, nw_ref, o_ref, st_ref, kb_ref, *, layer, heads, dk, dv):
    @pl.when(pl.program_id(1) == 0)
    def _():
        st_ref[...] = jnp.zeros_like(st_ref)

    lbp = lbp_ref[...]
    e = jnp.exp(lbp - jnp.max(lbp, axis=0, keepdims=True))
    p = e / jnp.sum(e, axis=0, keepdims=True)
    lb = jnp.zeros_like(p[0:1, :])
    for j in range(1, layer + 1):
        lb = lb + p[j:j + 1, :]
    sig_f = _sigmoid(f_ref[...].astype(F32))
    f_gate = lb + (1.0 - lb) * sig_f
    b_all = _cumsum_rows(jnp.log(jnp.maximum(f_gate, F_MIN)))
    k_all = (1.0 - lb) * (1.0 - sig_f)
    c = b_all.shape[0]

    def step(factored):
        for h in range(heads):
            q = q_ref[:, h * dk:(h + 1) * dk].astype(F32)
            qs = q * _sigmoid(q) * (dk ** -0.5)
            vv = i_ref[:, h * dv:(h + 1) * dv]
            o = _chunk_head(qs, k_all[:, h * dk:(h + 1) * dk], vv, b_all[:, h * dk:(h + 1) * dk], st_ref, h,
                            factored, kb_ref)
            g = g_ref[:, h * dv:(h + 1) * dv].astype(F32)
            o = _head_rmsnorm(o, nw_ref[...]) * _sigmoid(g)
            o_ref[:, h * dv:(h + 1) * dv] = o.astype(o_ref.dtype)

    safe = jnp.max(-b_all[c - 1:c, :]) <= SAFE_CHUNK_DECAY
    lax.cond(safe, functools.partial(step, True), functools.partial(step, False))


def _hgrn(big, lower_bounds, onorm_w, layer, dm, offs):
    m = big.shape[0]
    c = dm.chunk
    nc = dm.seq // c
    hk, hv = dm.hg_heads * dm.hg_dk, dm.hg_heads * dm.hg_dv
    row = lambda bi, ci: bi * nc + ci

    def seg(name, width):
        assert offs[name] % width == 0
        blk = offs[name] // width
        return pl.BlockSpec((c, width), lambda bi, ci: (row(bi, ci), blk))

    const = lambda shape: pl.BlockSpec(shape, lambda bi, ci: (0, 0))
    return pl.pallas_call(
        functools.partial(_hgrn_kernel, layer=layer, heads=dm.hg_heads, dk=dm.hg_dk, dv=dm.hg_dv),
        grid=(dm.batch, nc),
        in_specs=[seg("hg_q", hk), seg("hg_f", hk), seg("hg_i", hv), seg("hg_g", hv),
                  const((dm.depth, hk)), const((1, dm.hg_dv))],
        out_specs=pl.BlockSpec((c, hv), lambda bi, ci: (row(bi, ci), 0)),
        out_shape=jax.ShapeDtypeStruct((m, hv), BF16),
        scratch_shapes=[pltpu.VMEM((dm.hg_heads, dm.hg_dv, dm.hg_dk), F32),
                        pltpu.VMEM((2, c, dm.hg_dk), F32)],
        compiler_params=_cparams(("parallel", "arbitrary")),
        name="hgrn",
    )(big, big, big, big, lower_bounds, onorm_w.reshape(1, dm.hg_dv))


def _rope_table_kernel(pos_ref, f_ref, sgn_ref, cos_ref, sin_ref):
    ang = pos_ref[...].astype(F32) * f_ref[...]
    cos_ref[...] = jnp.cos(ang)
    sin_ref[...] = jnp.sin(ang) * sgn_ref[...]


def _rope_tables(pos_col, inv_freq_lanes, sign_lanes):
    m = pos_col.shape[0]
    tm = _tile(m, 512)
    lane = pl.BlockSpec((1, LANES), lambda i: (0, 0))
    tab = pl.BlockSpec((tm, LANES), lambda i: (i, 0))
    return pl.pallas_call(
        _rope_table_kernel,
        grid=(m // tm,),
        in_specs=[pl.BlockSpec((tm, 1), lambda i: (i, 0)), lane, lane],
        out_specs=[tab, tab],
        out_shape=[jax.ShapeDtypeStruct((m, LANES), F32)] * 2,
        compiler_params=_cparams(("parallel",)),
        name="rope_tables",
    )(pos_col, inv_freq_lanes, sign_lanes)


def _swap_halves(x, half):
    n = x.shape[-1]
    if 2 * half == n:
        return pltpu.roll(x, half, axis=1)
    lane = lax.broadcasted_iota(jnp.int32, x.shape, 1)
    return jnp.where((lane & half) == 0, pltpu.roll(x, n - half, axis=1), pltpu.roll(x, half, axis=1))


def _rope_apply_kernel(q_ref, k_ref, s_ref, c1_ref, s1_ref, c2_ref, s2_ref, qo_ref, ko_ref, iqo_ref,
                       iko_ref, iwo_ref, *, heads, dim, idx_heads, idx_dim):
    c1, s1, c2, s2 = c1_ref[...], s1_ref[...], c2_ref[...], s2_ref[...]
    q_scale = dim ** -0.5
    for h in range(heads):
        x = q_ref[:, h * dim:(h + 1) * dim].astype(F32)
        qo_ref[:, h * dim:(h + 1) * dim] = ((x * c1 + _swap_halves(x, dim // 2) * s1) * q_scale).astype(BF16)
    x = k_ref[...].astype(F32)
    ko_ref[...] = (x * c1 + _swap_halves(x, dim // 2) * s1).astype(BF16)
    iq_scale = idx_dim ** -0.5
    for g in range(idx_heads * idx_dim // LANES):
        x = s_ref[:, _SMALL_IQ + g * LANES:_SMALL_IQ + (g + 1) * LANES]
        r = (x * c2 + _swap_halves(x, idx_dim // 2) * s2) * iq_scale
        iqo_ref[:, g * LANES:(g + 1) * LANES] = r.astype(BF16)
    off = _SMALL_IQ + idx_heads * idx_dim
    x = s_ref[:, off:off + LANES]
    r = x * c2 + _swap_halves(x, idx_dim // 2) * s2
    lane = lax.broadcasted_iota(jnp.int32, x.shape, 1)
    iko_ref[...] = jnp.where(lane < idx_dim, r, pltpu.roll(r, idx_dim, axis=1)).astype(BF16)
    iwo_ref[...] = jnp.where(lane < idx_heads, pltpu.roll(x, LANES - idx_dim, axis=1), 0.0) * (idx_heads ** -0.5)


def _rope_apply(big, small, tabs, dm, offs):
    m = big.shape[0]
    tm = _tile(m, 256)
    hd = dm.dsa_heads * dm.dsa_dim
    iq = dm.idx_heads * dm.idx_dim
    assert dm.dsa_dim == LANES and 2 * dm.idx_dim == LANES and dm.idx_heads <= dm.idx_dim
    assert offs["dsa_q"] % hd == 0 and offs["dsa_k"] % dm.dsa_dim == 0
    rows = lambda w, blk=0: pl.BlockSpec((tm, w), lambda i: (i, blk))
    return pl.pallas_call(
        functools.partial(_rope_apply_kernel, heads=dm.dsa_heads, dim=dm.dsa_dim, idx_heads=dm.idx_heads,
                          idx_dim=dm.idx_dim),
        grid=(m // tm,),
        in_specs=[rows(hd, offs["dsa_q"] // hd), rows(dm.dsa_dim, offs["dsa_k"] // dm.dsa_dim),
                  rows(small.shape[1]), rows(LANES), rows(LANES), rows(LANES), rows(LANES)],
        out_specs=[rows(hd), rows(dm.dsa_dim), rows(iq), rows(LANES), rows(LANES)],
        out_shape=[jax.ShapeDtypeStruct((m, hd), BF16), jax.ShapeDtypeStruct((m, dm.dsa_dim), BF16),
                   jax.ShapeDtypeStruct((m, iq), BF16), jax.ShapeDtypeStruct((m, LANES), BF16),
                   jax.ShapeDtypeStruct((m, LANES), F32)],
        compiler_params=_cparams(("parallel",)),
        name="rope_apply",
    )(big, big, small, *tabs)


def _ind(mask):
    return jnp.where(mask, 1.0, 0.0)


def _row_sum(x):
    return jnp.sum(x, axis=-1, keepdims=True)


def _float_of_ordered(u):
    k = u ^ jnp.int32(INT_MIN)
    return pltpu.bitcast(jnp.where(k < 0, k ^ jnp.int32(0x7FFFFFFF), k), F32)


def _dsa_kernel(q_ref, iq_ref, iw_ref, k_ref, v_ref, ik_ref, o_ref, *, heads, dim, idx_heads, idx_dim, topk,
                q_start, n_hidden):
    tq = q_ref.shape[0]
    tk = k_ref.shape[0]
    nt = (((1,), (1,)), ((), ()))
    ik = ik_ref[...]
    lane = lax.broadcasted_iota(jnp.int32, (tq, LANES), 1)
    score = jnp.zeros((tq, tk), F32)
    for h in range(idx_heads):
        g = (h * idx_dim) // LANES
        lo = (h * idx_dim) % LANES
        x = iq_ref[:, g * LANES:(g + 1) * LANES]
        x = jnp.where(lane >= lo, jnp.where(lane < lo + idx_dim, x, jnp.zeros_like(x)), jnp.zeros_like(x))
        rel = lax.dot_general(x, ik, nt, preferred_element_type=F32)
        score = score + iw_ref[:, h:h + 1] * jnp.maximum(rel, 0.0)
    qpos = q_start + lax.broadcasted_iota(jnp.int32, (tq, 1), 0)
    kpos = lax.broadcasted_iota(jnp.int32, (1, tk), 1)
    allowed = kpos <= qpos
    score = jnp.where(allowed, score, MASK_VALUE)
    hidden = float(n_hidden)

    def thr_body(it, prefix):
        trial = prefix | lax.shift_left(jnp.int32(1), 31 - it)
        cand = _float_of_ordered(trial)
        cnt = _row_sum(_ind(score >= cand)) + jnp.where(MASK_VALUE >= cand, hidden, 0.0)
        return jnp.where(cnt >= topk, trial, prefix)

    thr = _float_of_ordered(lax.fori_loop(0, 32, thr_body, jnp.zeros((tq, 1), jnp.int32)))
    above = _ind(score > thr)
    need = topk - _row_sum(above) - jnp.where(MASK_VALUE > thr, hidden, 0.0)
    tie = jnp.where(allowed, _ind(score == thr), 0.0)

    def all_ties():
        return jnp.where(allowed, _ind(score >= thr), 0.0)

    def ordered_ties():
        nbits = (tk - 1).bit_length()

        def tie_body(it, j):
            trial = j | lax.shift_left(jnp.int32(1), nbits - 1 - it)
            cnt = _row_sum(jnp.where(kpos < trial, tie, 0.0))
            return jnp.where(cnt < need, trial, j)

        j_last = lax.fori_loop(0, nbits, tie_body, jnp.zeros((tq, 1), jnp.int32))
        return jnp.where(allowed, above, 0.0) + jnp.where(kpos <= j_last, tie, 0.0)

    valid = lax.cond(jnp.max(_row_sum(tie) - need) > 0.0, ordered_ties, all_ties) > 0.0

    k = k_ref[...]
    v_ones = jnp.concatenate([v_ref[...], jnp.ones((tk, dim), BF16)], axis=1)
    for h in range(heads):
        s = lax.dot_general(q_ref[:, h * dim:(h + 1) * dim], k, nt, preferred_element_type=F32)
        s = jnp.where(valid, s, MASK_VALUE)
        p = jnp.exp(s - jnp.max(s, axis=-1, keepdims=True)).astype(BF16)
        o = jnp.dot(p, v_ones, preferred_element_type=F32)
        o_ref[:, h * dim:(h + 1) * dim] = (o[:, 0:dim] / o[:, dim:dim + 1]).astype(o_ref.dtype)


def _dsa(q_rot, k_rot, big, iq_rot, ik_rot, iw, dm, offs):
    tq, t, nb = dm.q_block, dm.seq, dm.batch
    hd = dm.dsa_heads * dm.dsa_dim
    assert offs["dsa_v"] % dm.dsa_dim == 0 and t % tq == 0
    v_blk = offs["dsa_v"] // dm.dsa_dim
    per_batch = lambda a: a.reshape(nb, t, a.shape[-1])
    q3, k3, big3, iq3, ik3, iw3 = map(per_batch, (q_rot, k_rot, big, iq_rot, ik_rot, iw))
    outs = []
    for g in range(t // tq):
        tk = (g + 1) * tq
        qrow = lambda w, g=g: pl.BlockSpec((None, tq, w), lambda bi: (bi, g, 0))
        krow = lambda w, blk=0, tk=tk: pl.BlockSpec((None, tk, w), lambda bi: (bi, 0, blk))
        outs.append(pl.pallas_call(
            functools.partial(_dsa_kernel, heads=dm.dsa_heads, dim=dm.dsa_dim, idx_heads=dm.idx_heads,
                              idx_dim=dm.idx_dim, topk=dm.topk, q_start=g * tq, n_hidden=t - tk),
            grid=(nb,),
            in_specs=[qrow(hd), qrow(iq3.shape[-1]), qrow(LANES), krow(dm.dsa_dim), krow(dm.dsa_dim, v_blk),
                      krow(LANES)],
            out_specs=pl.BlockSpec((None, tq, hd), lambda bi: (bi, 0, 0)),
            out_shape=jax.ShapeDtypeStruct((nb, tq, hd), BF16),
            compiler_params=_cparams(("parallel",)),
            name=f"dsa_q{g}",
        )(q3, iq3, iw3, k3, big3, ik3))
    return jnp.concatenate(outs, axis=1).reshape(nb * t, hd)


def _layer_weights(l, dm, w_in, gla_gate_w2):
    _, src, big_off, big_n = _layout(dm)
    w = w_in[l]
    col = lambda name: w[:, src[name][0]:src[name][0] + src[name][1]]
    n_pad = -big_n % 512
    parts = [col(name) for name in _BIG_ORDER]
    if n_pad:
        parts.append(jnp.zeros((dm.d_model, n_pad), w.dtype))
    w_big = jnp.concatenate(parts, axis=1).astype(BF16)
    zeros = lambda n: jnp.zeros((dm.d_model, n), w.dtype)
    w_small = jnp.concatenate(
        [col("gla_a"), zeros(LANES - dm.gla_rank), col("idx_q"), col("idx_k"), col("idx_w"),
         zeros(LANES - dm.idx_dim - dm.idx_heads)], axis=1).astype(BF16)
    w2p = jnp.concatenate([gla_gate_w2[l], jnp.zeros((LANES - dm.gla_rank, gla_gate_w2.shape[2]), F32)], axis=0)
    return w_big, w_small, w2p, big_off


def _forward(dm, x, positions, norm1_w, w_in, gla_gate_w2, gla_gate_b, gla_onorm_w, hgrn_lower_bounds,
             hgrn_onorm_w, w_branch_gla, w_branch_dsa, w_branch_hgrn, w_out, norm2_w, w_mlp_up, w_mlp_down,
             final_norm_w):
    m = dm.batch * dm.seq
    h = x.reshape(m, dm.d_model)
    def lanes(d):
        inv = ROPE_THETA ** (-jnp.arange(0, d, 2, dtype=F32) / d)
        reps = LANES // d
        f = jnp.tile(jnp.concatenate([inv, inv]), reps).reshape(1, LANES)
        sgn = jnp.tile(jnp.concatenate([-jnp.ones(d // 2, F32), jnp.ones(d // 2, F32)]), reps).reshape(1, LANES)
        return f, sgn

    pos_col = positions.reshape(m, 1)
    tabs = _rope_tables(pos_col, *lanes(dm.dsa_dim)) + _rope_tables(pos_col, *lanes(dm.idx_dim))

    for l in range(dm.depth):
        w_big, w_small, w2p, offs = _layer_weights(l, dm, w_in, gla_gate_w2)
        u = _rmsnorm(h, norm1_w[l], BF16)
        big = _matmul(u, w_big, BF16, name="in_proj")
        small = _matmul(u, w_small, F32, tn_pref=256, name="in_proj_small")
        o_gla = _gla(big, small, w2p, gla_gate_b[l], gla_onorm_w[l], dm, offs)
        o_hg = _hgrn(big, hgrn_lower_bounds, hgrn_onorm_w[l], l, dm, offs)
        q_rot, k_rot, iq_rot, ik_rot, iw = _rope_apply(big, small, tabs, dm, offs)
        o_dsa = _dsa(q_rot, k_rot, big, iq_rot, ik_rot, iw, dm, offs)
        merged = _merge(o_gla, o_dsa, o_hg, w_branch_gla[l].astype(BF16), w_branch_dsa[l].astype(BF16),
                        w_branch_hgrn[l].astype(BF16), big,
                        (offs["gate_a"], offs["gate_b"], offs["gate_c"]), dm.d_model)
        h = _matmul_residual(merged, w_out[l].astype(BF16), h)
        h = _mlp(h, norm2_w[l], w_mlp_up[l].astype(BF16), w_mlp_down[l].astype(BF16))
    out = _rmsnorm(h, final_norm_w, F32)
    return out.reshape(dm.batch, dm.seq, dm.d_model)


def kernel(x, positions, norm1_w, w_in, gla_gate_w2, gla_gate_b, gla_onorm_w, hgrn_lower_bounds, hgrn_onorm_w,
           w_branch_gla, w_branch_dsa, w_branch_hgrn, w_out, norm2_w, w_mlp_up, w_mlp_down, final_norm_w):
    return _forward(_prod_dims(), x, positions, norm1_w, w_in, gla_gate_w2, gla_gate_b, gla_onorm_w,
                    hgrn_lower_bounds, hgrn_onorm_w, w_branch_gla, w_branch_dsa, w_branch_hgrn, w_out, norm2_w,
                    w_mlp_up, w_mlp_down, final_norm_w)
```

```python
import functools
from typing import NamedTuple

import jax
import jax.numpy as jnp
from jax import lax
from jax.experimental import pallas as pl
from jax.experimental.pallas import tpu as pltpu

F32 = jnp.float32
BF16 = jnp.bfloat16
HIGHEST = lax.Precision.HIGHEST

ROPE_THETA = 10000.0
NORM_EPS = 1e-6
MASK_VALUE = -1e30
F_MIN = 1e-12
GLA_GATE_NORMALIZER = 16.0
LANES = 128
INT_MIN = -2 ** 31
SAFE_CHUNK_DECAY = 150.0
VMEM_LIMIT = 48 * 1024 * 1024
VMEM_LIMIT_LARGE = 56 * 1024 * 1024


class _Dims(NamedTuple):
    d_model: int
    batch: int
    seq: int
    depth: int
    gla_heads: int
    gla_dk: int
    gla_dv: int
    gla_rank: int
    dsa_heads: int
    dsa_dim: int
    idx_heads: int
    idx_dim: int
    topk: int
    hg_heads: int
    hg_dk: int
    hg_dv: int
    d_ff: int
    chunk: int
    q_block: int


def _prod_dims():
    d = 2048
    return _Dims(d_model=d, batch=4, seq=2048, depth=2,
                 gla_heads=4, gla_dk=d // 2 // 4, gla_dv=d // 4, gla_rank=16,
                 dsa_heads=16, dsa_dim=128, idx_heads=8, idx_dim=64, topk=min(256, 2048 // 4),
                 hg_heads=d // 128, hg_dk=128, hg_dv=128, d_ff=4 * d, chunk=64, q_block=256)


def _in_sizes(dm):
    return (dm.gla_heads * dm.gla_dk, dm.gla_heads * dm.gla_dk, dm.gla_heads * dm.gla_dv,
            dm.gla_heads * dm.gla_dv, dm.gla_rank,
            dm.dsa_heads * dm.dsa_dim, dm.dsa_dim, dm.dsa_dim, dm.idx_heads * dm.idx_dim, dm.idx_dim,
            dm.idx_heads,
            dm.hg_heads * dm.hg_dk, dm.hg_heads * dm.hg_dk, dm.hg_heads * dm.hg_dv, dm.hg_heads * dm.hg_dv,
            dm.d_model, dm.d_model, dm.d_model)


_IN_NAMES = ("gla_q", "gla_k", "gla_v", "gla_g", "gla_a", "dsa_q", "dsa_k", "dsa_v", "idx_q", "idx_k",
             "idx_w", "hg_q", "hg_f", "hg_i", "hg_g", "gate_a", "gate_b", "gate_c")
_GROUPS = (("gla_q", "gla_k", "gla_v", "gla_g"), ("dsa_q", "dsa_k", "dsa_v"),
           ("hg_q", "hg_f", "hg_i", "hg_g", "gate_a", "gate_b", "gate_c"))
_SMALL_A = 0
_SMALL_IQ = LANES


def _tile(n, pref):
    t = min(n, pref)
    while n % t:
        t //= 2
    return t


def _layout(dm):
    sizes = dict(zip(_IN_NAMES, _in_sizes(dm)))
    src, off = {}, 0
    for name in _IN_NAMES:
        src[name] = (off, sizes[name])
        off += sizes[name]
    groups = []
    for names in _GROUPS:
        start = src[names[0]][0]
        offs = {n: src[n][0] - start for n in names}
        width = src[names[-1]][0] + src[names[-1]][1] - start
        for n in names:
            assert offs[n] % min(sizes[n], 2 * LANES) == 0, n
        groups.append((start, width, offs))
    return src, groups


def _cparams(sem, vmem=VMEM_LIMIT):
    return pltpu.CompilerParams(dimension_semantics=sem, vmem_limit_bytes=vmem)


def _rms(x, w):
    ms = jnp.mean(x * x, axis=-1, keepdims=True)
    return x * lax.rsqrt(ms + NORM_EPS) * w


def _rmsnorm_kernel(x_ref, w_ref, o_ref):
    o_ref[...] = _rms(x_ref[...], w_ref[...]).astype(o_ref.dtype)


def _rmsnorm(x, w, out_dtype):
    m, d = x.shape
    tm = _tile(m, 512)
    return pl.pallas_call(
        _rmsnorm_kernel,
        grid=(m // tm,),
        in_specs=[pl.BlockSpec((tm, d), lambda i: (i, 0)), pl.BlockSpec((1, d), lambda i: (0, 0))],
        out_specs=pl.BlockSpec((tm, d), lambda i: (i, 0)),
        out_shape=jax.ShapeDtypeStruct((m, d), out_dtype),
        compiler_params=_cparams(("parallel",)),
        name="rmsnorm",
    )(x, w.reshape(1, d))


def _w_spec(w, layer, k, tn):
    if w.ndim == 3:
        return pl.BlockSpec((None, k, tn), lambda i, j: (layer, 0, j))
    return pl.BlockSpec((k, tn), lambda i, j: (0, j))


def _matmul_kernel(x_ref, w_ref, o_ref):
    o_ref[...] = jnp.dot(x_ref[...], w_ref[...].astype(BF16), preferred_element_type=F32).astype(o_ref.dtype)


def _matmul(x, w, out_dtype, n=None, layer=0, tm_pref=2048, tn_pref=512, name="matmul"):
    m, k = x.shape
    n = w.shape[-1] if n is None else n
    tm, tn = _tile(m, tm_pref), _tile(n, tn_pref)
    return pl.pallas_call(
        _matmul_kernel,
        grid=(m // tm, n // tn),
        in_specs=[pl.BlockSpec((tm, k), lambda i, j: (i, 0)), _w_spec(w, layer, k, tn)],
        out_specs=pl.BlockSpec((tm, tn), lambda i, j: (i, j)),
        out_shape=jax.ShapeDtypeStruct((m, n), out_dtype),
        compiler_params=_cparams(("parallel", "arbitrary")),
        name=name,
    )(x, w)


def _matmul_residual_kernel(x_ref, w_ref, r_ref, o_ref):
    o_ref[...] = r_ref[...] + jnp.dot(x_ref[...], w_ref[...].astype(BF16), preferred_element_type=F32)


def _matmul_residual(x, w, layer, res, tm_pref=1024, tn_pref=512):
    m, k = x.shape
    n = w.shape[-1]
    tm, tn = _tile(m, tm_pref), _tile(n, tn_pref)
    return pl.pallas_call(
        _matmul_residual_kernel,
        grid=(m // tm, n // tn),
        in_specs=[pl.BlockSpec((tm, k), lambda i, j: (i, 0)), _w_spec(w, layer, k, tn),
                  pl.BlockSpec((tm, tn), lambda i, j: (i, j))],
        out_specs=pl.BlockSpec((tm, tn), lambda i, j: (i, j)),
        out_shape=jax.ShapeDtypeStruct((m, n), F32),
        compiler_params=_cparams(("parallel", "arbitrary")),
        name="out_proj",
    )(x, w, res)


def _sigmoid(x):
    return 1.0 / (1.0 + jnp.exp(-x))


def _merge_kernel(oa_ref, ob_ref, oc_ref, wa_ref, wb_ref, wc_ref, ga_ref, gb_ref, gc_ref, o_ref):
    def branch(o_ref_, w_ref_, g_ref_):
        y = jnp.dot(o_ref_[...], w_ref_[...].astype(BF16), preferred_element_type=F32)
        return _sigmoid(g_ref_[...].astype(F32)) * y

    acc = branch(oa_ref, wa_ref, ga_ref) + branch(ob_ref, wb_ref, gb_ref) + branch(oc_ref, wc_ref, gc_ref)
    o_ref[...] = acc.astype(o_ref.dtype)


def _merge(o_gla, o_dsa, o_hg, w_gla, w_dsa, w_hg, layer, gates, gate_offs, d_model):
    m = o_gla.shape[0]
    tm, tn = _tile(m, 1024), _tile(d_model, 256)
    o_spec = lambda a: pl.BlockSpec((tm, a.shape[1]), lambda i, j: (i, 0))

    def g_spec(off):
        assert off % tn == 0
        return pl.BlockSpec((tm, tn), lambda i, j: (i, off // tn + j))

    return pl.pallas_call(
        _merge_kernel,
        grid=(m // tm, d_model // tn),
        in_specs=[o_spec(o_gla), o_spec(o_dsa), o_spec(o_hg),
                  _w_spec(w_gla, layer, w_gla.shape[-2], tn), _w_spec(w_dsa, layer, w_dsa.shape[-2], tn),
                  _w_spec(w_hg, layer, w_hg.shape[-2], tn),
                  g_spec(gate_offs[0]), g_spec(gate_offs[1]), g_spec(gate_offs[2])],
        out_specs=pl.BlockSpec((tm, tn), lambda i, j: (i, j)),
        out_shape=jax.ShapeDtypeStruct((m, d_model), BF16),
        compiler_params=_cparams(("parallel", "arbitrary"), VMEM_LIMIT_LARGE),
        name="merge",
    )(o_gla, o_dsa, o_hg, w_gla, w_dsa, w_hg, gates, gates, gates)


def _mlp_kernel(h_ref, nw_ref, wu_ref, wd_ref, nnw_ref, o_ref, *rest, last):
    u_ref = rest[-1]
    j = pl.program_id(1)

    @pl.when(j == 0)
    def _():
        x = h_ref[...]
        u_ref[...] = _rms(x, nw_ref[...]).astype(BF16)
        o_ref[...] = x

    a = jnp.dot(u_ref[...], wu_ref[...].astype(BF16), preferred_element_type=F32)
    a = jnp.square(jnp.maximum(a, 0.0)).astype(BF16)
    o_ref[...] += jnp.dot(a, wd_ref[...].astype(BF16), preferred_element_type=F32)

    @pl.when(j == pl.num_programs(1) - 1)
    def _():
        y = _rms(o_ref[...], nnw_ref[...])
        if last:
            o_ref[...] = y
        else:
            rest[0][...] = y.astype(BF16)


def _mlp(h, norm_w, w_up, w_down, layer, next_norm_w, last):
    m, d = h.shape
    f = w_up.shape[-1]
    tm, tf = _tile(m, 1024), _tile(f, 512)
    row = pl.BlockSpec((tm, d), lambda i, j: (i, 0), pipeline_mode=pl.Buffered(1))
    vec = pl.BlockSpec((1, d), lambda i, j: (0, 0))
    out_shape = [jax.ShapeDtypeStruct((m, d), F32)] + ([] if last else [jax.ShapeDtypeStruct((m, d), BF16)])
    res = pl.pallas_call(
        functools.partial(_mlp_kernel, last=last),
        grid=(m // tm, f // tf),
        in_specs=[row, vec, pl.BlockSpec((None, d, tf), lambda i, j: (layer, 0, j)),
                  pl.BlockSpec((None, tf, d), lambda i, j: (layer, j, 0)), vec],
        out_specs=[row] * len(out_shape),
        out_shape=out_shape,
        scratch_shapes=[pltpu.VMEM((tm, d), BF16)],
        compiler_params=_cparams(("parallel", "arbitrary"), VMEM_LIMIT_LARGE),
        name="mlp",
    )(h, norm_w.reshape(1, d), w_up, w_down, next_norm_w.reshape(1, d))
    return res[0] if last else res


def _chunk_head(qs, kk, vv, b, st_ref, head, factored, kb_ref):
    c, kdim = qs.shape
    row = lax.broadcasted_iota(jnp.int32, (c, c), 0)
    col = lax.broadcasted_iota(jnp.int32, (c, c), 1)
    b_last = b[c - 1:c, :]
    st = st_ref[head]
    nt = (((1,), (1,)), ((), ()))
    if factored:
        ref_row = b[c // 2 - 1:c // 2, :]
        qd = qs * jnp.exp(b - ref_row)
        kd = kk * jnp.exp(ref_row - b)
        q_in = qd * jnp.exp(ref_row)
        k_dec = kd * jnp.exp(b_last - ref_row)
        att = lax.dot_general(qd.astype(BF16), kd.astype(BF16), nt, preferred_element_type=F32)
    else:
        q_in = qs * jnp.exp(b)
        k_dec = kk * jnp.exp(b_last - b)
        kb_ref[0, :, 0:kdim] = kk
        kb_ref[1, :, 0:kdim] = b

        def body(s, att):
            k_row = kb_ref[0, pl.ds(s, 1), 0:kdim]
            b_row = kb_ref[1, pl.ds(s, 1), 0:kdim]
            w = jnp.sum(qs * k_row * jnp.exp(jnp.minimum(b - b_row, 0.0)), axis=-1, keepdims=True)
            return jnp.where(col == s, w, att)

        att = lax.fori_loop(0, c, body, jnp.zeros((c, c), F32))
    att = jnp.where(col <= row, att, 0.0)
    o = lax.dot_general(q_in.astype(BF16), st.astype(BF16), nt, preferred_element_type=F32)
    o = o + jnp.dot(att.astype(BF16), vv.astype(BF16), preferred_element_type=F32)
    upd = lax.dot_general(vv.astype(BF16), k_dec.astype(BF16), (((0,), (0,)), ((), ())),
                          preferred_element_type=F32)
    st_ref[head] = st * jnp.exp(b_last) + upd
    return o


def _cumsum_chunks(x, nb, c):
    row = lax.broadcasted_iota(jnp.int32, (c, c), 0)
    col = lax.broadcasted_iota(jnp.int32, (c, c), 1)
    tri = jnp.where(col <= row, 1.0, 0.0).astype(F32)
    return [jnp.dot(tri, x[bi * c:(bi + 1) * c, :], preferred_element_type=F32, precision=HIGHEST)
            for bi in range(nb)]


def _chunk_is_safe(b_all):
    c = b_all[0].shape[0]
    worst = functools.reduce(jnp.minimum, [b[c - 1:c, :] for b in b_all])
    return jnp.max(-worst) <= SAFE_CHUNK_DECAY


def _log_sigmoid(x):
    return jnp.minimum(x, 0.0) - jnp.log(1.0 + jnp.exp(-jnp.abs(x)))


def _gla_kernel(q_ref, k_ref, v_ref, g_ref, a_ref, w2_ref, b2_ref, nw_ref, o_ref, st_ref, kb_ref, *,
                heads, dk, dv):
    nb, c = q_ref.shape[0], q_ref.shape[1]

    @pl.when(pl.program_id(0) == 0)
    def _():
        st_ref[...] = jnp.zeros_like(st_ref)

    a = a_ref[...].reshape(nb * c, a_ref.shape[2])
    z = jnp.dot(a, w2_ref[...], preferred_element_type=F32, precision=HIGHEST) + b2_ref[...]
    b_all = _cumsum_chunks(_log_sigmoid(z) * (1.0 / GLA_GATE_NORMALIZER), nb, c)

    def step(factored):
        for bi in range(nb):
            for h in range(heads):
                qs = q_ref[bi, :, h * dk:(h + 1) * dk].astype(F32) * (dk ** -0.5)
                kk = k_ref[bi, :, h * dk:(h + 1) * dk].astype(F32)
                vv = v_ref[bi, :, h * dv:(h + 1) * dv]
                o = _chunk_head(qs, kk, vv, b_all[bi][:, h * dk:(h + 1) * dk], st_ref, bi * heads + h, factored,
                                kb_ref)
                g = g_ref[bi, :, h * dv:(h + 1) * dv].astype(F32)
                o = _rms(o, nw_ref[...]) * (g * _sigmoid(g))
                o_ref[bi, :, h * dv:(h + 1) * dv] = o.astype(o_ref.dtype)

    lax.cond(_chunk_is_safe(b_all), functools.partial(step, True), functools.partial(step, False))


def _chunk_specs(nb, c, arr, offs):
    def seg(name, width):
        assert offs[name] % width == 0
        blk = offs[name] // width
        return pl.BlockSpec((nb, c, width), lambda ci: (0, ci, blk))

    return seg, arr.reshape(nb, -1, arr.shape[-1])


def _gla(p_gla, small, w2p, b2, onorm_w, dm, offs):
    nb, c = dm.batch, dm.chunk
    hk, hv = dm.gla_heads * dm.gla_dk, dm.gla_heads * dm.gla_dv
    seg, p3 = _chunk_specs(nb, c, p_gla, offs)
    small3 = small.reshape(nb, dm.seq, small.shape[-1])
    const = lambda shape: pl.BlockSpec(shape, lambda ci: (0, 0))
    out = pl.pallas_call(
        functools.partial(_gla_kernel, heads=dm.gla_heads, dk=dm.gla_dk, dv=dm.gla_dv),
        grid=(dm.seq // c,),
        in_specs=[seg("gla_q", hk), seg("gla_k", hk), seg("gla_v", hv), seg("gla_g", hv),
                  pl.BlockSpec((nb, c, LANES), lambda ci: (0, ci, _SMALL_A // LANES)),
                  const((LANES, hk)), const((1, hk)), const((1, dm.gla_dv))],
        out_specs=pl.BlockSpec((nb, c, hv), lambda ci: (0, ci, 0)),
        out_shape=jax.ShapeDtypeStruct((nb, dm.seq, hv), BF16),
        scratch_shapes=[pltpu.VMEM((nb * dm.gla_heads, dm.gla_dv, dm.gla_dk), F32),
                        pltpu.VMEM((2, c, dm.gla_dk), F32)],
        compiler_params=_cparams(("arbitrary",)),
        name="gla",
    )(p3, p3, p3, p3, small3, w2p, b2.reshape(1, hk), onorm_w.reshape(1, dm.gla_dv))
    return out.reshape(nb * dm.seq, hv)


def _hgrn_kernel(q_ref, f_ref, i_ref, g_ref, lbp_ref, nw_ref, o_ref, st_ref, kb_ref, *, layer, heads, dk, dv):
    nb, c = q_ref.shape[0], q_ref.shape[1]

    @pl.when(pl.program_id(0) == 0)
    def _():
        st_ref[...] = jnp.zeros_like(st_ref)

    lbp = lbp_ref[...]
    e = jnp.exp(lbp - jnp.max(lbp, axis=0, keepdims=True))
    p = e / jnp.sum(e, axis=0, keepdims=True)
    lb = jnp.zeros_like(p[0:1, :])
    for j in range(1, layer + 1):
        lb = lb + p[j:j + 1, :]
    sig_f = _sigmoid(f_ref[...].reshape(nb * c, f_ref.shape[2]).astype(F32))
    f_gate = lb + (1.0 - lb) * sig_f
    b_all = _cumsum_chunks(jnp.log(jnp.maximum(f_gate, F_MIN)), nb, c)
    k_all = (1.0 - lb) * (1.0 - sig_f)

    def step(factored):
        for bi in range(nb):
            for h in range(heads):
                q = q_ref[bi, :, h * dk:(h + 1) * dk].astype(F32)
                qs = q * _sigmoid(q) * (dk ** -0.5)
                vv = i_ref[bi, :, h * dv:(h + 1) * dv]
                o = _chunk_head(qs, k_all[bi * c:(bi + 1) * c, h * dk:(h + 1) * dk], vv,
                                b_all[bi][:, h * dk:(h + 1) * dk], st_ref, bi * heads + h, factored, kb_ref)
                g = g_ref[bi, :, h * dv:(h + 1) * dv].astype(F32)
                o = _rms(o, nw_ref[...]) * _sigmoid(g)
                o_ref[bi, :, h * dv:(h + 1) * dv] = o.astype(o_ref.dtype)

    lax.cond(_chunk_is_safe(b_all), functools.partial(step, True), functools.partial(step, False))


def _hgrn(p_hg, lower_bounds, onorm_w, layer, dm, offs):
    nb, c = dm.batch, dm.chunk
    hk, hv = dm.hg_heads * dm.hg_dk, dm.hg_heads * dm.hg_dv
    seg, p3 = _chunk_specs(nb, c, p_hg, offs)
    const = lambda shape: pl.BlockSpec(shape, lambda ci: (0, 0))
    out = pl.pallas_call(
        functools.partial(_hgrn_kernel, layer=layer, heads=dm.hg_heads, dk=dm.hg_dk, dv=dm.hg_dv),
        grid=(dm.seq // c,),
        in_specs=[seg("hg_q", hk), seg("hg_f", hk), seg("hg_i", hv), seg("hg_g", hv),
                  const((dm.depth, hk)), const((1, dm.hg_dv))],
        out_specs=pl.BlockSpec((nb, c, hv), lambda ci: (0, ci, 0)),
        out_shape=jax.ShapeDtypeStruct((nb, dm.seq, hv), BF16),
        scratch_shapes=[pltpu.VMEM((nb * dm.hg_heads, dm.hg_dv, dm.hg_dk), F32),
                        pltpu.VMEM((2, c, dm.hg_dk), F32)],
        compiler_params=_cparams(("arbitrary",)),
        name="hgrn",
    )(p3, p3, p3, p3, lower_bounds, onorm_w.reshape(1, dm.hg_dv))
    return out.reshape(nb * dm.seq, hv)


def _rope_table_kernel(pos_ref, f_ref, sgn_ref, cos_ref, sin_ref):
    ang = pos_ref[...].astype(F32) * f_ref[...]
    cos_ref[...] = jnp.cos(ang)
    sin_ref[...] = jnp.sin(ang) * sgn_ref[...]


def _rope_tables(pos_col, inv_freq_lanes, sign_lanes):
    m = pos_col.shape[0]
    tm = _tile(m, 512)
    lane = pl.BlockSpec((1, LANES), lambda i: (0, 0))
    tab = pl.BlockSpec((tm, LANES), lambda i: (i, 0))
    return pl.pallas_call(
        _rope_table_kernel,
        grid=(m // tm,),
        in_specs=[pl.BlockSpec((tm, 1), lambda i: (i, 0)), lane, lane],
        out_specs=[tab, tab],
        out_shape=[jax.ShapeDtypeStruct((m, LANES), F32)] * 2,
        compiler_params=_cparams(("parallel",)),
        name="rope_tables",
    )(pos_col, inv_freq_lanes, sign_lanes)


def _swap_halves(x, half):
    n = x.shape[-1]
    if 2 * half == n:
        return pltpu.roll(x, half, axis=1)
    lane = lax.broadcasted_iota(jnp.int32, x.shape, 1)
    return jnp.where((lane & half) == 0, pltpu.roll(x, n - half, axis=1), pltpu.roll(x, half, axis=1))


def _rope_apply_kernel(q_ref, k_ref, s_ref, c1_ref, s1_ref, c2_ref, s2_ref, qo_ref, ko_ref, iqo_ref,
                       iko_ref, iwo_ref, *, heads, dim, idx_heads, idx_dim):
    c1, s1, c2, s2 = c1_ref[...], s1_ref[...], c2_ref[...], s2_ref[...]
    q_scale = dim ** -0.5
    for h in range(heads):
        x = q_ref[:, h * dim:(h + 1) * dim].astype(F32)
        qo_ref[:, h * dim:(h + 1) * dim] = ((x * c1 + _swap_halves(x, dim // 2) * s1) * q_scale).astype(BF16)
    x = k_ref[...].astype(F32)
    ko_ref[...] = (x * c1 + _swap_halves(x, dim // 2) * s1).astype(BF16)
    iq_scale = idx_dim ** -0.5
    for g in range(idx_heads * idx_dim // LANES):
        x = s_ref[:, _SMALL_IQ + g * LANES:_SMALL_IQ + (g + 1) * LANES]
        r = (x * c2 + _swap_halves(x, idx_dim // 2) * s2) * iq_scale
        iqo_ref[:, g * LANES:(g + 1) * LANES] = r.astype(BF16)
    off = _SMALL_IQ + idx_heads * idx_dim
    x = s_ref[:, off:off + LANES]
    r = x * c2 + _swap_halves(x, idx_dim // 2) * s2
    lane = lax.broadcasted_iota(jnp.int32, x.shape, 1)
    iko_ref[...] = jnp.where(lane < idx_dim, r, pltpu.roll(r, idx_dim, axis=1)).astype(BF16)
    iwo_ref[...] = jnp.where(lane < idx_heads, pltpu.roll(x, LANES - idx_dim, axis=1), 0.0) * (idx_heads ** -0.5)


def _rope_apply(p_dsa, small, tabs, dm, offs):
    m = p_dsa.shape[0]
    tm = _tile(m, 256)
    hd = dm.dsa_heads * dm.dsa_dim
    iq = dm.idx_heads * dm.idx_dim
    assert dm.dsa_dim == LANES and 2 * dm.idx_dim == LANES and dm.idx_heads <= dm.idx_dim
    assert offs["dsa_q"] % hd == 0 and offs["dsa_k"] % dm.dsa_dim == 0
    rows = lambda w, blk=0: pl.BlockSpec((tm, w), lambda i: (i, blk))
    return pl.pallas_call(
        functools.partial(_rope_apply_kernel, heads=dm.dsa_heads, dim=dm.dsa_dim, idx_heads=dm.idx_heads,
                          idx_dim=dm.idx_dim),
        grid=(m // tm,),
        in_specs=[rows(hd, offs["dsa_q"] // hd), rows(dm.dsa_dim, offs["dsa_k"] // dm.dsa_dim),
                  rows(small.shape[1]), rows(LANES), rows(LANES), rows(LANES), rows(LANES)],
        out_specs=[rows(hd), rows(dm.dsa_dim), rows(iq), rows(LANES), rows(LANES)],
        out_shape=[jax.ShapeDtypeStruct((m, hd), BF16), jax.ShapeDtypeStruct((m, dm.dsa_dim), BF16),
                   jax.ShapeDtypeStruct((m, iq), BF16), jax.ShapeDtypeStruct((m, LANES), BF16),
                   jax.ShapeDtypeStruct((m, LANES), F32)],
        compiler_params=_cparams(("parallel",)),
        name="rope_apply",
    )(p_dsa, p_dsa, small, *tabs)


def _ind(mask):
    return jnp.where(mask, 1.0, 0.0)


def _row_sum(x):
    return jnp.sum(x, axis=-1, keepdims=True)


def _float_of_ordered(u):
    k = u ^ jnp.int32(INT_MIN)
    return pltpu.bitcast(jnp.where(k < 0, k ^ jnp.int32(0x7FFFFFFF), k), F32)


def _dsa_kernel(q_ref, iq_ref, iw_ref, k_ref, v_ref, ik_ref, o_ref, *, heads, dim, idx_heads, idx_dim, topk,
                q_start, n_hidden):
    tq = q_ref.shape[0]
    tk = k_ref.shape[0]
    nt = (((1,), (1,)), ((), ()))
    ik = ik_ref[...]
    lane = lax.broadcasted_iota(jnp.int32, (tq, LANES), 1)
    score = jnp.zeros((tq, tk), F32)
    for h in range(idx_heads):
        g = (h * idx_dim) // LANES
        lo = (h * idx_dim) % LANES
        x = iq_ref[:, g * LANES:(g + 1) * LANES]
        x = jnp.where(lane >= lo, jnp.where(lane < lo + idx_dim, x, jnp.zeros_like(x)), jnp.zeros_like(x))
        rel = lax.dot_general(x, ik, nt, preferred_element_type=F32)
        score = score + iw_ref[:, h:h + 1] * jnp.maximum(rel, 0.0)
    qpos = q_start + lax.broadcasted_iota(jnp.int32, (tq, 1), 0)
    kpos = lax.broadcasted_iota(jnp.int32, (1, tk), 1)
    allowed = kpos <= qpos
    score = jnp.where(allowed, score, MASK_VALUE)
    hidden = float(n_hidden)

    def thr_body(it, prefix):
        trial = prefix | lax.shift_left(jnp.int32(1), 31 - it)
        cand = _float_of_ordered(trial)
        cnt = _row_sum(_ind(score >= cand)) + jnp.where(MASK_VALUE >= cand, hidden, 0.0)
        return jnp.where(cnt >= topk, trial, prefix)

    thr = _float_of_ordered(lax.fori_loop(0, 32, thr_body, jnp.zeros((tq, 1), jnp.int32)))
    above = _ind(score > thr)
    need = topk - _row_sum(above) - jnp.where(MASK_VALUE > thr, hidden, 0.0)
    tie = jnp.where(allowed, _ind(score == thr), 0.0)

    def all_ties():
        return jnp.where(allowed, _ind(score >= thr), 0.0)

    def ordered_ties():
        nbits = (tk - 1).bit_length()

        def tie_body(it, j):
            trial = j | lax.shift_left(jnp.int32(1), nbits - 1 - it)
            cnt = _row_sum(jnp.where(kpos < trial, tie, 0.0))
            return jnp.where(cnt < need, trial, j)

        j_last = lax.fori_loop(0, nbits, tie_body, jnp.zeros((tq, 1), jnp.int32))
        return jnp.where(allowed, above, 0.0) + jnp.where(kpos <= j_last, tie, 0.0)

    valid = lax.cond(jnp.max(_row_sum(tie) - need) > 0.0, ordered_ties, all_ties) > 0.0

    k = k_ref[...]
    v_ones = jnp.concatenate([v_ref[...], jnp.ones((tk, dim), BF16)], axis=1)
    for h in range(heads):
        s = lax.dot_general(q_ref[:, h * dim:(h + 1) * dim], k, nt, preferred_element_type=F32)
        s = jnp.where(valid, s, MASK_VALUE)
        p = jnp.exp(s - jnp.max(s, axis=-1, keepdims=True)).astype(BF16)
        o = jnp.dot(p, v_ones, preferred_element_type=F32)
        o_ref[:, h * dim:(h + 1) * dim] = (o[:, 0:dim] / o[:, dim:dim + 1]).astype(o_ref.dtype)


def _dsa(q_rot, k_rot, p_dsa, iq_rot, ik_rot, iw, dm, offs):
    tq, t, nb = dm.q_block, dm.seq, dm.batch
    hd = dm.dsa_heads * dm.dsa_dim
    assert offs["dsa_v"] % dm.dsa_dim == 0 and t % tq == 0
    v_blk = offs["dsa_v"] // dm.dsa_dim
    per_batch = lambda a: a.reshape(nb, t, a.shape[-1])
    q3, k3, p3, iq3, ik3, iw3 = map(per_batch, (q_rot, k_rot, p_dsa, iq_rot, ik_rot, iw))
    outs = []
    for g in range(t // tq):
        tk = (g + 1) * tq
        qrow = lambda w, g=g: pl.BlockSpec((None, tq, w), lambda bi: (bi, g, 0))
        krow = lambda w, blk=0, tk=tk: pl.BlockSpec((None, tk, w), lambda bi: (bi, 0, blk))
        outs.append(pl.pallas_call(
            functools.partial(_dsa_kernel, heads=dm.dsa_heads, dim=dm.dsa_dim, idx_heads=dm.idx_heads,
                              idx_dim=dm.idx_dim, topk=dm.topk, q_start=g * tq, n_hidden=t - tk),
            grid=(nb,),
            in_specs=[qrow(hd), qrow(iq3.shape[-1]), qrow(LANES), krow(dm.dsa_dim), krow(dm.dsa_dim, v_blk),
                      krow(LANES)],
            out_specs=pl.BlockSpec((None, tq, hd), lambda bi: (bi, 0, 0)),
            out_shape=jax.ShapeDtypeStruct((nb, tq, hd), BF16),
            compiler_params=_cparams(("parallel",)),
            name=f"dsa_q{g}",
        )(q3, iq3, iw3, k3, p3, ik3))
    return jnp.concatenate(outs, axis=1).reshape(nb * t, hd)


def _layer_weights(l, dm, w_in, gla_gate_w2):
    src, groups = _layout(dm)
    w = w_in[l]
    col = lambda name: w[:, src[name][0]:src[name][0] + src[name][1]]
    cut = lambda start, width: w[:, start:start + width].astype(BF16)
    zeros = lambda n: jnp.zeros((dm.d_model, n), w.dtype)
    w_small = jnp.concatenate(
        [col("gla_a"), zeros(LANES - dm.gla_rank), col("idx_q"), col("idx_k"), col("idx_w"),
         zeros(LANES - dm.idx_dim - dm.idx_heads)], axis=1).astype(BF16)
    w2p = jnp.concatenate([gla_gate_w2[l], jnp.zeros((LANES - dm.gla_rank, gla_gate_w2.shape[2]), F32)], axis=0)
    return cut(*groups[1][:2]), cut(*groups[2][:2]), w_small, w2p


def _forward(dm, x, positions, norm1_w, w_in, gla_gate_w2, gla_gate_b, gla_onorm_w, hgrn_lower_bounds,
             hgrn_onorm_w, w_branch_gla, w_branch_dsa, w_branch_hgrn, w_out, norm2_w, w_mlp_up, w_mlp_down,
             final_norm_w):
    m = dm.batch * dm.seq
    h = x.reshape(m, dm.d_model)
    _, ((gla_start, gla_width, gla_offs), (_, _, dsa_offs), (_, _, hg_offs)) = _layout(dm)
    assert gla_start == 0

    def lanes(d):
        inv = ROPE_THETA ** (-jnp.arange(0, d, 2, dtype=F32) / d)
        reps = LANES // d
        f = jnp.tile(jnp.concatenate([inv, inv]), reps).reshape(1, LANES)
        sgn = jnp.tile(jnp.concatenate([-jnp.ones(d // 2, F32), jnp.ones(d // 2, F32)]), reps).reshape(1, LANES)
        return f, sgn

    pos_col = positions.reshape(m, 1)
    tabs = _rope_tables(pos_col, *lanes(dm.dsa_dim)) + _rope_tables(pos_col, *lanes(dm.idx_dim))

    w_up_bf16, w_down_bf16 = w_mlp_up.astype(BF16), w_mlp_down.astype(BF16)
    u = _rmsnorm(h, norm1_w[0], BF16)
    for l in range(dm.depth):
        w_dsa, w_hg, w_small, w2p = _layer_weights(l, dm, w_in, gla_gate_w2)
        p_gla = _matmul(u, w_in, BF16, n=gla_width, layer=l, name="in_proj_gla")
        p_hg = _matmul(u, w_hg, BF16, name="in_proj_hg")
        p_dsa = _matmul(u, w_dsa, BF16, tn_pref=256, name="in_proj_dsa")
        small = _matmul(u, w_small, F32, tn_pref=256, name="in_proj_small")
        o_gla = _gla(p_gla, small, w2p, gla_gate_b[l], gla_onorm_w[l], dm, gla_offs)
        o_hg = _hgrn(p_hg, hgrn_lower_bounds, hgrn_onorm_w[l], l, dm, hg_offs)
        q_rot, k_rot, iq_rot, ik_rot, iw = _rope_apply(p_dsa, small, tabs, dm, dsa_offs)
        o_dsa = _dsa(q_rot, k_rot, p_dsa, iq_rot, ik_rot, iw, dm, dsa_offs)
        merged = _merge(o_gla, o_dsa, o_hg, w_branch_gla, w_branch_dsa, w_branch_hgrn, l, p_hg,
                        (hg_offs["gate_a"], hg_offs["gate_b"], hg_offs["gate_c"]), dm.d_model)
        h = _matmul_residual(merged, w_out, l, h)
        last = l == dm.depth - 1
        res = _mlp(h, norm2_w[l], w_up_bf16, w_down_bf16, l, final_norm_w if last else norm1_w[l + 1], last)
        if not last:
            h, u = res
    return res.reshape(dm.batch, dm.seq, dm.d_model)


def kernel(x, positions, norm1_w, w_in, gla_gate_w2, gla_gate_b, gla_onorm_w, hgrn_lower_bounds, hgrn_onorm_w,
           w_branch_gla, w_branch_dsa, w_branch_hgrn, w_out, norm2_w, w_mlp_up, w_mlp_down, final_norm_w):
    return _forward(_prod_dims(), x, positions, norm1_w, w_in, gla_gate_w2, gla_gate_b, gla_onorm_w,
                    hgrn_lower_bounds, hgrn_onorm_w, w_branch_gla, w_branch_dsa, w_branch_hgrn, w_out, norm2_w,
                    w_mlp_up, w_mlp_down, final_norm_w)
```

```python
import functools
from typing import NamedTuple

import jax
import jax.numpy as jnp
from jax import lax
from jax.experimental import pallas as pl
from jax.experimental.pallas import tpu as pltpu

F32 = jnp.float32
BF16 = jnp.bfloat16
HIGHEST = lax.Precision.HIGHEST

ROPE_THETA = 10000.0
NORM_EPS = 1e-6
MASK_VALUE = -1e30
F_MIN = 1e-12
GLA_GATE_NORMALIZER = 16.0
LANES = 128
INT_MIN = -2 ** 31
SAFE_CHUNK_DECAY = 150.0
VMEM_LIMIT = 48 * 1024 * 1024
VMEM_LIMIT_LARGE = 56 * 1024 * 1024


class _Dims(NamedTuple):
    d_model: int
    batch: int
    seq: int
    depth: int
    gla_heads: int
    gla_dk: int
    gla_dv: int
    gla_rank: int
    dsa_heads: int
    dsa_dim: int
    idx_heads: int
    idx_dim: int
    topk: int
    hg_heads: int
    hg_dk: int
    hg_dv: int
    d_ff: int
    chunk: int
    q_block: int


def _prod_dims():
    d = 2048
    return _Dims(d_model=d, batch=4, seq=2048, depth=2,
                 gla_heads=4, gla_dk=d // 2 // 4, gla_dv=d // 4, gla_rank=16,
                 dsa_heads=16, dsa_dim=128, idx_heads=8, idx_dim=64, topk=min(256, 2048 // 4),
                 hg_heads=d // 128, hg_dk=128, hg_dv=128, d_ff=4 * d, chunk=64, q_block=256)


def _in_sizes(dm):
    return (dm.gla_heads * dm.gla_dk, dm.gla_heads * dm.gla_dk, dm.gla_heads * dm.gla_dv,
            dm.gla_heads * dm.gla_dv, dm.gla_rank,
            dm.dsa_heads * dm.dsa_dim, dm.dsa_dim, dm.dsa_dim, dm.idx_heads * dm.idx_dim, dm.idx_dim,
            dm.idx_heads,
            dm.hg_heads * dm.hg_dk, dm.hg_heads * dm.hg_dk, dm.hg_heads * dm.hg_dv, dm.hg_heads * dm.hg_dv,
            dm.d_model, dm.d_model, dm.d_model)


_IN_NAMES = ("gla_q", "gla_k", "gla_v", "gla_g", "gla_a", "dsa_q", "dsa_k", "dsa_v", "idx_q", "idx_k",
             "idx_w", "hg_q", "hg_f", "hg_i", "hg_g", "gate_a", "gate_b", "gate_c")
_GROUPS = (("gla_q", "gla_k", "gla_v", "gla_g"), ("dsa_q", "dsa_k", "dsa_v"),
           ("hg_q", "hg_f", "hg_i", "hg_g", "gate_a", "gate_b", "gate_c"))
_SMALL_A = 0
_SMALL_IQ = LANES


def _tile(n, pref):
    t = min(n, pref)
    while n % t:
        t //= 2
    return t


def _layout(dm):
    sizes = dict(zip(_IN_NAMES, _in_sizes(dm)))
    src, off = {}, 0
    for name in _IN_NAMES:
        src[name] = (off, sizes[name])
        off += sizes[name]
    groups = []
    for names in _GROUPS:
        start = src[names[0]][0]
        offs = {n: src[n][0] - start for n in names}
        width = src[names[-1]][0] + src[names[-1]][1] - start
        for n in names:
            assert offs[n] % min(sizes[n], 2 * LANES) == 0, n
        groups.append((start, width, offs))
    return src, groups


def _cparams(sem, vmem=VMEM_LIMIT):
    return pltpu.CompilerParams(dimension_semantics=sem, vmem_limit_bytes=vmem)


def _rms(x, w):
    ms = jnp.mean(x * x, axis=-1, keepdims=True)
    return x * lax.rsqrt(ms + NORM_EPS) * w


def _rmsnorm_kernel(x_ref, w_ref, o_ref):
    o_ref[...] = _rms(x_ref[...], w_ref[...]).astype(o_ref.dtype)


def _rmsnorm(x, w, out_dtype):
    m, d = x.shape
    tm = _tile(m, 512)
    return pl.pallas_call(
        _rmsnorm_kernel,
        grid=(m // tm,),
        in_specs=[pl.BlockSpec((tm, d), lambda i: (i, 0)), pl.BlockSpec((1, d), lambda i: (0, 0))],
        out_specs=pl.BlockSpec((tm, d), lambda i: (i, 0)),
        out_shape=jax.ShapeDtypeStruct((m, d), out_dtype),
        compiler_params=_cparams(("parallel",)),
        name="rmsnorm",
    )(x, w.reshape(1, d))


def _w_spec(w, layer, k, tn):
    if w.ndim == 3:
        return pl.BlockSpec((None, k, tn), lambda i, j: (layer, 0, j))
    return pl.BlockSpec((k, tn), lambda i, j: (0, j))


def _matmul_kernel(x_ref, w_ref, o_ref):
    o_ref[...] = jnp.dot(x_ref[...], w_ref[...].astype(BF16), preferred_element_type=F32).astype(o_ref.dtype)


def _matmul(x, w, out_dtype, n=None, layer=0, tm_pref=2048, tn_pref=512, name="matmul"):
    m, k = x.shape
    n = w.shape[-1] if n is None else n
    tm, tn = _tile(m, tm_pref), _tile(n, tn_pref)
    return pl.pallas_call(
        _matmul_kernel,
        grid=(m // tm, n // tn),
        in_specs=[pl.BlockSpec((tm, k), lambda i, j: (i, 0)), _w_spec(w, layer, k, tn)],
        out_specs=pl.BlockSpec((tm, tn), lambda i, j: (i, j)),
        out_shape=jax.ShapeDtypeStruct((m, n), out_dtype),
        compiler_params=_cparams(("parallel", "arbitrary")),
        name=name,
    )(x, w)


def _matmul_window_kernel(x_ref, *refs, shift):
    *w_refs, o_ref = refs
    tn = o_ref.shape[1]
    w = jnp.concatenate([r[...] for r in w_refs], axis=1)[:, shift:shift + tn]
    o_ref[...] = jnp.dot(x_ref[...], w.astype(BF16), preferred_element_type=F32).astype(o_ref.dtype)


def _matmul_window(x, w, layer, start, n, out_dtype, tm_pref=2048, tn_pref=512, name="matmul_window"):
    m, k = x.shape
    tm, tn = _tile(m, tm_pref), _tile(n, tn_pref)
    assert tn % LANES == 0
    base, shift = divmod(start, LANES)
    per_tile = tn // LANES
    n_blocks = per_tile + (1 if shift else 0)
    blk = lambda b: pl.BlockSpec((None, k, LANES), lambda i, j: (layer, 0, base + j * per_tile + b))
    return pl.pallas_call(
        functools.partial(_matmul_window_kernel, shift=shift),
        grid=(m // tm, n // tn),
        in_specs=[pl.BlockSpec((tm, k), lambda i, j: (i, 0))] + [blk(b) for b in range(n_blocks)],
        out_specs=pl.BlockSpec((tm, tn), lambda i, j: (i, j)),
        out_shape=jax.ShapeDtypeStruct((m, n), out_dtype),
        compiler_params=_cparams(("parallel", "arbitrary")),
        name=name,
    )(x, *([w] * n_blocks))


def _matmul_residual_kernel(x_ref, w_ref, r_ref, o_ref):
    o_ref[...] = r_ref[...] + jnp.dot(x_ref[...], w_ref[...].astype(BF16), preferred_element_type=F32)


def _matmul_residual(x, w, layer, res, tm_pref=1024, tn_pref=512):
    m, k = x.shape
    n = w.shape[-1]
    tm, tn = _tile(m, tm_pref), _tile(n, tn_pref)
    return pl.pallas_call(
        _matmul_residual_kernel,
        grid=(m // tm, n // tn),
        in_specs=[pl.BlockSpec((tm, k), lambda i, j: (i, 0)), _w_spec(w, layer, k, tn),
                  pl.BlockSpec((tm, tn), lambda i, j: (i, j))],
        out_specs=pl.BlockSpec((tm, tn), lambda i, j: (i, j)),
        out_shape=jax.ShapeDtypeStruct((m, n), F32),
        compiler_params=_cparams(("parallel", "arbitrary")),
        name="out_proj",
    )(x, w, res)


def _sigmoid(x):
    return 1.0 / (1.0 + jnp.exp(-x))


def _merge_kernel(oa_ref, ob_ref, oc_ref, wa_ref, wb_ref, wc_ref, ga_ref, gb_ref, gc_ref, o_ref):
    def branch(o_ref_, w_ref_, g_ref_):
        y = jnp.dot(o_ref_[...], w_ref_[...].astype(BF16), preferred_element_type=F32)
        return _sigmoid(g_ref_[...].astype(F32)) * y

    acc = branch(oa_ref, wa_ref, ga_ref) + branch(ob_ref, wb_ref, gb_ref) + branch(oc_ref, wc_ref, gc_ref)
    o_ref[...] = acc.astype(o_ref.dtype)


def _merge(o_gla, o_dsa, o_hg, w_gla, w_dsa, w_hg, layer, gates, gate_offs, d_model):
    m = o_gla.shape[0]
    tm, tn = _tile(m, 1024), _tile(d_model, 256)
    o_spec = lambda a: pl.BlockSpec((tm, a.shape[1]), lambda i, j: (i, 0))

    def g_spec(off):
        assert off % tn == 0
        return pl.BlockSpec((tm, tn), lambda i, j: (i, off // tn + j))

    return pl.pallas_call(
        _merge_kernel,
        grid=(m // tm, d_model // tn),
        in_specs=[o_spec(o_gla), o_spec(o_dsa), o_spec(o_hg),
                  _w_spec(w_gla, layer, w_gla.shape[-2], tn), _w_spec(w_dsa, layer, w_dsa.shape[-2], tn),
                  _w_spec(w_hg, layer, w_hg.shape[-2], tn),
                  g_spec(gate_offs[0]), g_spec(gate_offs[1]), g_spec(gate_offs[2])],
        out_specs=pl.BlockSpec((tm, tn), lambda i, j: (i, j)),
        out_shape=jax.ShapeDtypeStruct((m, d_model), BF16),
        compiler_params=_cparams(("parallel", "arbitrary"), VMEM_LIMIT_LARGE),
        name="merge",
    )(o_gla, o_dsa, o_hg, w_gla, w_dsa, w_hg, gates, gates, gates)


def _mlp_kernel(h_ref, nw_ref, wu_ref, wd_ref, nnw_ref, o_ref, *rest, last):
    u_ref = rest[-1]
    j = pl.program_id(1)

    @pl.when(j == 0)
    def _():
        x = h_ref[...]
        u_ref[...] = _rms(x, nw_ref[...]).astype(BF16)
        o_ref[...] = x

    a = jnp.dot(u_ref[...], wu_ref[...].astype(BF16), preferred_element_type=F32)
    a = jnp.square(jnp.maximum(a, 0.0)).astype(BF16)
    o_ref[...] += jnp.dot(a, wd_ref[...].astype(BF16), preferred_element_type=F32)

    @pl.when(j == pl.num_programs(1) - 1)
    def _():
        y = _rms(o_ref[...], nnw_ref[...])
        if last:
            o_ref[...] = y
        else:
            rest[0][...] = y.astype(BF16)


def _mlp(h, norm_w, w_up, w_down, layer, next_norm_w, last):
    m, d = h.shape
    f = w_up.shape[-1]
    tm, tf = _tile(m, 512), _tile(f, 1024)
    row = pl.BlockSpec((tm, d), lambda i, j: (i, 0))
    vec = pl.BlockSpec((1, d), lambda i, j: (0, 0))
    out_shape = [jax.ShapeDtypeStruct((m, d), F32)] + ([] if last else [jax.ShapeDtypeStruct((m, d), BF16)])
    res = pl.pallas_call(
        functools.partial(_mlp_kernel, last=last),
        grid=(m // tm, f // tf),
        in_specs=[row, vec, pl.BlockSpec((None, d, tf), lambda i, j: (layer, 0, j)),
                  pl.BlockSpec((None, tf, d), lambda i, j: (layer, j, 0)), vec],
        out_specs=[row] * len(out_shape),
        out_shape=out_shape,
        scratch_shapes=[pltpu.VMEM((tm, d), BF16)],
        compiler_params=_cparams(("parallel", "arbitrary"), VMEM_LIMIT_LARGE),
        name="mlp",
    )(h, norm_w.reshape(1, d), w_up, w_down, next_norm_w.reshape(1, d))
    return res[0] if last else res


def _chunk_head(qs, kk, vv, b, st_ref, head, factored, kb_ref):
    c, kdim = qs.shape
    row = lax.broadcasted_iota(jnp.int32, (c, c), 0)
    col = lax.broadcasted_iota(jnp.int32, (c, c), 1)
    b_last = b[c - 1:c, :]
    st = st_ref[head]
    nt = (((1,), (1,)), ((), ()))
    if factored:
        ref_row = b[c // 2 - 1:c // 2, :]
        qd = qs * jnp.exp(b - ref_row)
        kd = kk * jnp.exp(ref_row - b)
        q_in = qd * jnp.exp(ref_row)
        k_dec = kd * jnp.exp(b_last - ref_row)
        att = lax.dot_general(qd.astype(BF16), kd.astype(BF16), nt, preferred_element_type=F32)
    else:
        q_in = qs * jnp.exp(b)
        k_dec = kk * jnp.exp(b_last - b)
        kb_ref[0, :, 0:kdim] = kk
        kb_ref[1, :, 0:kdim] = b

        def body(s, att):
            k_row = kb_ref[0, pl.ds(s, 1), 0:kdim]
            b_row = kb_ref[1, pl.ds(s, 1), 0:kdim]
            w = jnp.sum(qs * k_row * jnp.exp(jnp.minimum(b - b_row, 0.0)), axis=-1, keepdims=True)
            return jnp.where(col == s, w, att)

        att = lax.fori_loop(0, c, body, jnp.zeros((c, c), F32))
    att = jnp.where(col <= row, att, 0.0)
    o = lax.dot_general(q_in.astype(BF16), st.astype(BF16), nt, preferred_element_type=F32)
    o = o + jnp.dot(att.astype(BF16), vv.astype(BF16), preferred_element_type=F32)
    upd = lax.dot_general(vv.astype(BF16), k_dec.astype(BF16), (((0,), (0,)), ((), ())),
                          preferred_element_type=F32)
    st_ref[head] = st * jnp.exp(b_last) + upd
    return o


def _cumsum_chunks(x, nb, c):
    row = lax.broadcasted_iota(jnp.int32, (c, c), 0)
    col = lax.broadcasted_iota(jnp.int32, (c, c), 1)
    tri = jnp.where(col <= row, 1.0, 0.0).astype(F32)
    return [jnp.dot(tri, x[bi * c:(bi + 1) * c, :], preferred_element_type=F32, precision=HIGHEST)
            for bi in range(nb)]


def _chunk_is_safe(b_all):
    c = b_all[0].shape[0]
    worst = functools.reduce(jnp.minimum, [b[c - 1:c, :] for b in b_all])
    return jnp.max(-worst) <= SAFE_CHUNK_DECAY


def _log_sigmoid(x):
    return jnp.minimum(x, 0.0) - jnp.log(1.0 + jnp.exp(-jnp.abs(x)))


def _gla_kernel(q_ref, k_ref, v_ref, g_ref, a_ref, w2_ref, b2_ref, nw_ref, o_ref, st_ref, kb_ref, *,
                heads, dk, dv):
    nb, c = q_ref.shape[0], q_ref.shape[1]

    @pl.when(pl.program_id(0) == 0)
    def _():
        st_ref[...] = jnp.zeros_like(st_ref)

    a = a_ref[...].reshape(nb * c, a_ref.shape[2])
    z = jnp.dot(a, w2_ref[...], preferred_element_type=F32, precision=HIGHEST) + b2_ref[...]
    b_all = _cumsum_chunks(_log_sigmoid(z) * (1.0 / GLA_GATE_NORMALIZER), nb, c)

    def step(factored):
        for bi in range(nb):
            for h in range(heads):
                qs = q_ref[bi, :, h * dk:(h + 1) * dk].astype(F32) * (dk ** -0.5)
                kk = k_ref[bi, :, h * dk:(h + 1) * dk].astype(F32)
                vv = v_ref[bi, :, h * dv:(h + 1) * dv]
                o = _chunk_head(qs, kk, vv, b_all[bi][:, h * dk:(h + 1) * dk], st_ref, bi * heads + h, factored,
                                kb_ref)
                g = g_ref[bi, :, h * dv:(h + 1) * dv].astype(F32)
                o = _rms(o, nw_ref[...]) * (g * _sigmoid(g))
                o_ref[bi, :, h * dv:(h + 1) * dv] = o.astype(o_ref.dtype)

    lax.cond(_chunk_is_safe(b_all), functools.partial(step, True), functools.partial(step, False))


def _chunk_specs(nb, c, arr, offs):
    def seg(name, width):
        assert offs[name] % width == 0
        blk = offs[name] // width
        return pl.BlockSpec((nb, c, width), lambda ci: (0, ci, blk))

    return seg, arr.reshape(nb, -1, arr.shape[-1])


def _gla(p_gla, small, w2p, b2, onorm_w, dm, offs):
    nb, c = dm.batch, dm.chunk
    hk, hv = dm.gla_heads * dm.gla_dk, dm.gla_heads * dm.gla_dv
    seg, p3 = _chunk_specs(nb, c, p_gla, offs)
    small3 = small.reshape(nb, dm.seq, small.shape[-1])
    const = lambda shape: pl.BlockSpec(shape, lambda ci: (0, 0))
    out = pl.pallas_call(
        functools.partial(_gla_kernel, heads=dm.gla_heads, dk=dm.gla_dk, dv=dm.gla_dv),
        grid=(dm.seq // c,),
        in_specs=[seg("gla_q", hk), seg("gla_k", hk), seg("gla_v", hv), seg("gla_g", hv),
                  pl.BlockSpec((nb, c, LANES), lambda ci: (0, ci, _SMALL_A // LANES)),
                  const((LANES, hk)), const((1, hk)), const((1, dm.gla_dv))],
        out_specs=pl.BlockSpec((nb, c, hv), lambda ci: (0, ci, 0)),
        out_shape=jax.ShapeDtypeStruct((nb, dm.seq, hv), BF16),
        scratch_shapes=[pltpu.VMEM((nb * dm.gla_heads, dm.gla_dv, dm.gla_dk), F32),
                        pltpu.VMEM((2, c, dm.gla_dk), F32)],
        compiler_params=_cparams(("arbitrary",)),
        name="gla",
    )(p3, p3, p3, p3, small3, w2p, b2.reshape(1, hk), onorm_w.reshape(1, dm.gla_dv))
    return out.reshape(nb * dm.seq, hv)


def _hgrn_kernel(q_ref, f_ref, i_ref, g_ref, lbp_ref, nw_ref, o_ref, st_ref, kb_ref, *, layer, heads, dk, dv):
    nb, c = q_ref.shape[0], q_ref.shape[1]

    @pl.when(pl.program_id(0) == 0)
    def _():
        st_ref[...] = jnp.zeros_like(st_ref)

    lbp = lbp_ref[...]
    e = jnp.exp(lbp - jnp.max(lbp, axis=0, keepdims=True))
    p = e / jnp.sum(e, axis=0, keepdims=True)
    lb = jnp.zeros_like(p[0:1, :])
    for j in range(1, layer + 1):
        lb = lb + p[j:j + 1, :]
    sig_f = _sigmoid(f_ref[...].reshape(nb * c, f_ref.shape[2]).astype(F32))
    f_gate = lb + (1.0 - lb) * sig_f
    b_all = _cumsum_chunks(jnp.log(jnp.maximum(f_gate, F_MIN)), nb, c)
    k_all = (1.0 - lb) * (1.0 - sig_f)

    def step(factored):
        for bi in range(nb):
            for h in range(heads):
                q = q_ref[bi, :, h * dk:(h + 1) * dk].astype(F32)
                qs = q * _sigmoid(q) * (dk ** -0.5)
                vv = i_ref[bi, :, h * dv:(h + 1) * dv]
                o = _chunk_head(qs, k_all[bi * c:(bi + 1) * c, h * dk:(h + 1) * dk], vv,
                                b_all[bi][:, h * dk:(h + 1) * dk], st_ref, bi * heads + h, factored, kb_ref)
                g = g_ref[bi, :, h * dv:(h + 1) * dv].astype(F32)
                o = _rms(o, nw_ref[...]) * _sigmoid(g)
                o_ref[bi, :, h * dv:(h + 1) * dv] = o.astype(o_ref.dtype)

    lax.cond(_chunk_is_safe(b_all), functools.partial(step, True), functools.partial(step, False))


def _hgrn(p_hg, lower_bounds, onorm_w, layer, dm, offs):
    nb, c = dm.batch, dm.chunk
    hk, hv = dm.hg_heads * dm.hg_dk, dm.hg_heads * dm.hg_dv
    seg, p3 = _chunk_specs(nb, c, p_hg, offs)
    const = lambda shape: pl.BlockSpec(shape, lambda ci: (0, 0))
    out = pl.pallas_call(
        functools.partial(_hgrn_kernel, layer=layer, heads=dm.hg_heads, dk=dm.hg_dk, dv=dm.hg_dv),
        grid=(dm.seq // c,),
        in_specs=[seg("hg_q", hk), seg("hg_f", hk), seg("hg_i", hv), seg("hg_g", hv),
                  const((dm.depth, hk)), const((1, dm.hg_dv))],
        out_specs=pl.BlockSpec((nb, c, hv), lambda ci: (0, ci, 0)),
        out_shape=jax.ShapeDtypeStruct((nb, dm.seq, hv), BF16),
        scratch_shapes=[pltpu.VMEM((nb * dm.hg_heads, dm.hg_dv, dm.hg_dk), F32),
                        pltpu.VMEM((2, c, dm.hg_dk), F32)],
        compiler_params=_cparams(("arbitrary",)),
        name="hgrn",
    )(p3, p3, p3, p3, lower_bounds, onorm_w.reshape(1, dm.hg_dv))
    return out.reshape(nb * dm.seq, hv)


def _rope_table_kernel(pos_ref, f_ref, sgn_ref, cos_ref, sin_ref):
    ang = pos_ref[...].astype(F32) * f_ref[...]
    cos_ref[...] = jnp.cos(ang)
    sin_ref[...] = jnp.sin(ang) * sgn_ref[...]


def _rope_tables(pos_col, inv_freq_lanes, sign_lanes):
    m = pos_col.shape[0]
    tm = _tile(m, 512)
    lane = pl.BlockSpec((1, LANES), lambda i: (0, 0))
    tab = pl.BlockSpec((tm, LANES), lambda i: (i, 0))
    return pl.pallas_call(
        _rope_table_kernel,
        grid=(m // tm,),
        in_specs=[pl.BlockSpec((tm, 1), lambda i: (i, 0)), lane, lane],
        out_specs=[tab, tab],
        out_shape=[jax.ShapeDtypeStruct((m, LANES), F32)] * 2,
        compiler_params=_cparams(("parallel",)),
        name="rope_tables",
    )(pos_col, inv_freq_lanes, sign_lanes)


def _swap_halves(x, half):
    n = x.shape[-1]
    if 2 * half == n:
        return pltpu.roll(x, half, axis=1)
    lane = lax.broadcasted_iota(jnp.int32, x.shape, 1)
    return jnp.where((lane & half) == 0, pltpu.roll(x, n - half, axis=1), pltpu.roll(x, half, axis=1))


def _rope_apply_kernel(q_ref, k_ref, s_ref, c1_ref, s1_ref, c2_ref, s2_ref, qo_ref, ko_ref, iqo_ref,
                       iko_ref, iwo_ref, *, heads, dim, idx_heads, idx_dim):
    c1, s1, c2, s2 = c1_ref[...], s1_ref[...], c2_ref[...], s2_ref[...]
    q_scale = dim ** -0.5
    for h in range(heads):
        x = q_ref[:, h * dim:(h + 1) * dim].astype(F32)
        qo_ref[:, h * dim:(h + 1) * dim] = ((x * c1 + _swap_halves(x, dim // 2) * s1) * q_scale).astype(BF16)
    x = k_ref[...].astype(F32)
    ko_ref[...] = (x * c1 + _swap_halves(x, dim // 2) * s1).astype(BF16)
    iq_scale = idx_dim ** -0.5
    for g in range(idx_heads * idx_dim // LANES):
        x = s_ref[:, _SMALL_IQ + g * LANES:_SMALL_IQ + (g + 1) * LANES]
        r = (x * c2 + _swap_halves(x, idx_dim // 2) * s2) * iq_scale
        iqo_ref[:, g * LANES:(g + 1) * LANES] = r.astype(BF16)
    off = _SMALL_IQ + idx_heads * idx_dim
    x = s_ref[:, off:off + LANES]
    r = x * c2 + _swap_halves(x, idx_dim // 2) * s2
    lane = lax.broadcasted_iota(jnp.int32, x.shape, 1)
    iko_ref[...] = jnp.where(lane < idx_dim, r, pltpu.roll(r, idx_dim, axis=1)).astype(BF16)
    iwo_ref[...] = jnp.where(lane < idx_heads, pltpu.roll(x, LANES - idx_dim, axis=1), 0.0) * (idx_heads ** -0.5)


def _rope_apply(p_dsa, small, tabs, dm, offs):
    m = p_dsa.shape[0]
    tm = _tile(m, 256)
    hd = dm.dsa_heads * dm.dsa_dim
    iq = dm.idx_heads * dm.idx_dim
    assert dm.dsa_dim == LANES and 2 * dm.idx_dim == LANES and dm.idx_heads <= dm.idx_dim
    assert offs["dsa_q"] % hd == 0 and offs["dsa_k"] % dm.dsa_dim == 0
    rows = lambda w, blk=0: pl.BlockSpec((tm, w), lambda i: (i, blk))
    return pl.pallas_call(
        functools.partial(_rope_apply_kernel, heads=dm.dsa_heads, dim=dm.dsa_dim, idx_heads=dm.idx_heads,
                          idx_dim=dm.idx_dim),
        grid=(m // tm,),
        in_specs=[rows(hd, offs["dsa_q"] // hd), rows(dm.dsa_dim, offs["dsa_k"] // dm.dsa_dim),
                  rows(small.shape[1]), rows(LANES), rows(LANES), rows(LANES), rows(LANES)],
        out_specs=[rows(hd), rows(dm.dsa_dim), rows(iq), rows(LANES), rows(LANES)],
        out_shape=[jax.ShapeDtypeStruct((m, hd), BF16), jax.ShapeDtypeStruct((m, dm.dsa_dim), BF16),
                   jax.ShapeDtypeStruct((m, iq), BF16), jax.ShapeDtypeStruct((m, LANES), BF16),
                   jax.ShapeDtypeStruct((m, LANES), F32)],
        compiler_params=_cparams(("parallel",)),
        name="rope_apply",
    )(p_dsa, p_dsa, small, *tabs)


def _ind(mask):
    return jnp.where(mask, 1.0, 0.0)


def _row_sum(x):
    return jnp.sum(x, axis=-1, keepdims=True)


def _float_of_ordered(u):
    k = u ^ jnp.int32(INT_MIN)
    return pltpu.bitcast(jnp.where(k < 0, k ^ jnp.int32(0x7FFFFFFF), k), F32)


def _dsa_kernel(q_ref, iq_ref, iw_ref, k_ref, v_ref, ik_ref, o_ref, *, heads, dim, idx_heads, idx_dim, topk,
                q_start, n_hidden):
    tq = q_ref.shape[0]
    tk = k_ref.shape[0]
    nt = (((1,), (1,)), ((), ()))
    ik = ik_ref[...]
    lane = lax.broadcasted_iota(jnp.int32, (tq, LANES), 1)
    score = jnp.zeros((tq, tk), F32)
    for h in range(idx_heads):
        g = (h * idx_dim) // LANES
        lo = (h * idx_dim) % LANES
        x = iq_ref[:, g * LANES:(g + 1) * LANES]
        x = jnp.where(lane >= lo, jnp.where(lane < lo + idx_dim, x, jnp.zeros_like(x)), jnp.zeros_like(x))
        rel = lax.dot_general(x, ik, nt, preferred_element_type=F32)
        score = score + iw_ref[:, h:h + 1] * jnp.maximum(rel, 0.0)
    qpos = q_start + lax.broadcasted_iota(jnp.int32, (tq, 1), 0)
    kpos = lax.broadcasted_iota(jnp.int32, (1, tk), 1)
    allowed = kpos <= qpos
    score = jnp.where(allowed, score, MASK_VALUE)
    hidden = float(n_hidden)

    def thr_body(it, prefix):
        trial = prefix | lax.shift_left(jnp.int32(1), 31 - it)
        cand = _float_of_ordered(trial)
        cnt = _row_sum(_ind(score >= cand)) + jnp.where(MASK_VALUE >= cand, hidden, 0.0)
        return jnp.where(cnt >= topk, trial, prefix)

    thr = _float_of_ordered(lax.fori_loop(0, 32, thr_body, jnp.zeros((tq, 1), jnp.int32)))
    above = _ind(score > thr)
    need = topk - _row_sum(above) - jnp.where(MASK_VALUE > thr, hidden, 0.0)
    tie = jnp.where(allowed, _ind(score == thr), 0.0)

    def all_ties():
        return jnp.where(allowed, _ind(score >= thr), 0.0)

    def ordered_ties():
        nbits = (tk - 1).bit_length()

        def tie_body(it, j):
            trial = j | lax.shift_left(jnp.int32(1), nbits - 1 - it)
            cnt = _row_sum(jnp.where(kpos < trial, tie, 0.0))
            return jnp.where(cnt < need, trial, j)

        j_last = lax.fori_loop(0, nbits, tie_body, jnp.zeros((tq, 1), jnp.int32))
        return jnp.where(allowed, above, 0.0) + jnp.where(kpos <= j_last, tie, 0.0)

    valid = lax.cond(jnp.max(_row_sum(tie) - need) > 0.0, ordered_ties, all_ties) > 0.0

    k = k_ref[...]
    v_ones = jnp.concatenate([v_ref[...], jnp.ones((tk, dim), BF16)], axis=1)
    for h in range(heads):
        s = lax.dot_general(q_ref[:, h * dim:(h + 1) * dim], k, nt, preferred_element_type=F32)
        s = jnp.where(valid, s, MASK_VALUE)
        p = jnp.exp(s - jnp.max(s, axis=-1, keepdims=True)).astype(BF16)
        o = jnp.dot(p, v_ones, preferred_element_type=F32)
        o_ref[:, h * dim:(h + 1) * dim] = (o[:, 0:dim] / o[:, dim:dim + 1]).astype(o_ref.dtype)


def _with_carried_output(kern, q_ref, iq_ref, iw_ref, k_ref, v_ref, ik_ref, carried_ref, o_ref):
    del carried_ref
    kern(q_ref, iq_ref, iw_ref, k_ref, v_ref, ik_ref, o_ref)


def _dsa(q_rot, k_rot, p_dsa, iq_rot, ik_rot, iw, dm, offs):
    tq, t, nb = dm.q_block, dm.seq, dm.batch
    hd = dm.dsa_heads * dm.dsa_dim
    assert offs["dsa_v"] % dm.dsa_dim == 0 and t % tq == 0
    v_blk = offs["dsa_v"] // dm.dsa_dim
    per_batch = lambda a: a.reshape(nb, t, a.shape[-1])
    q3, k3, p3, iq3, ik3, iw3 = map(per_batch, (q_rot, k_rot, p_dsa, iq_rot, ik_rot, iw))
    out = None
    for g in range(t // tq):
        tk = (g + 1) * tq
        qrow = lambda w, g=g: pl.BlockSpec((None, tq, w), lambda bi: (bi, g, 0))
        krow = lambda w, blk=0, tk=tk: pl.BlockSpec((None, tk, w), lambda bi: (bi, 0, blk))
        kern = functools.partial(_dsa_kernel, heads=dm.dsa_heads, dim=dm.dsa_dim, idx_heads=dm.idx_heads,
                                 idx_dim=dm.idx_dim, topk=dm.topk, q_start=g * tq, n_hidden=t - tk)
        args = [q3, iq3, iw3, k3, p3, ik3]
        in_specs = [qrow(hd), qrow(iq3.shape[-1]), qrow(LANES), krow(dm.dsa_dim), krow(dm.dsa_dim, v_blk),
                    krow(LANES)]
        if out is not None:
            kern = functools.partial(_with_carried_output, kern)
            args.append(out)
            in_specs.append(pl.BlockSpec(memory_space=pl.ANY))
        out = pl.pallas_call(
            kern,
            grid=(nb,),
            in_specs=in_specs,
            out_specs=qrow(hd),
            out_shape=jax.ShapeDtypeStruct((nb, t, hd), BF16),
            input_output_aliases={} if len(args) == 6 else {6: 0},
            compiler_params=_cparams(("parallel",)),
            name=f"dsa_q{g}",
        )(*args)
    return out.reshape(nb * t, hd)


def _layer_weights(l, dm, w_in, gla_gate_w2):
    src, _ = _layout(dm)
    w = w_in[l]
    col = lambda name: w[:, src[name][0]:src[name][0] + src[name][1]]
    zeros = lambda n: jnp.zeros((dm.d_model, n), w.dtype)
    w_small = jnp.concatenate(
        [col("gla_a"), zeros(LANES - dm.gla_rank), col("idx_q"), col("idx_k"), col("idx_w"),
         zeros(LANES - dm.idx_dim - dm.idx_heads)], axis=1).astype(BF16)
    w2p = jnp.concatenate([gla_gate_w2[l], jnp.zeros((LANES - dm.gla_rank, gla_gate_w2.shape[2]), F32)], axis=0)
    return w_small, w2p


def _forward(dm, x, positions, norm1_w, w_in, gla_gate_w2, gla_gate_b, gla_onorm_w, hgrn_lower_bounds,
             hgrn_onorm_w, w_branch_gla, w_branch_dsa, w_branch_hgrn, w_out, norm2_w, w_mlp_up, w_mlp_down,
             final_norm_w):
    m = dm.batch * dm.seq
    h = x.reshape(m, dm.d_model)
    _, ((gla_start, gla_width, gla_offs), (dsa_start, dsa_width, dsa_offs), (hg_start, hg_width, hg_offs)) = \
        _layout(dm)
    assert gla_start == 0

    def lanes(d):
        inv = ROPE_THETA ** (-jnp.arange(0, d, 2, dtype=F32) / d)
        reps = LANES // d
        f = jnp.tile(jnp.concatenate([inv, inv]), reps).reshape(1, LANES)
        sgn = jnp.tile(jnp.concatenate([-jnp.ones(d // 2, F32), jnp.ones(d // 2, F32)]), reps).reshape(1, LANES)
        return f, sgn

    pos_col = positions.reshape(m, 1)
    tabs = _rope_tables(pos_col, *lanes(dm.dsa_dim)) + _rope_tables(pos_col, *lanes(dm.idx_dim))

    w_up_bf16, w_down_bf16 = w_mlp_up.astype(BF16), w_mlp_down.astype(BF16)
    u = _rmsnorm(h, norm1_w[0], BF16)
    for l in range(dm.depth):
        w_small, w2p = _layer_weights(l, dm, w_in, gla_gate_w2)
        p_gla = _matmul(u, w_in, BF16, n=gla_width, layer=l, name="in_proj_gla")
        p_hg = _matmul_window(u, w_in, l, hg_start, hg_width, BF16, name="in_proj_hg")
        p_dsa = _matmul_window(u, w_in, l, dsa_start, dsa_width, BF16, tn_pref=256, name="in_proj_dsa")
        small = _matmul(u, w_small, F32, tn_pref=256, name="in_proj_small")
        o_gla = _gla(p_gla, small, w2p, gla_gate_b[l], gla_onorm_w[l], dm, gla_offs)
        o_hg = _hgrn(p_hg, hgrn_lower_bounds, hgrn_onorm_w[l], l, dm, hg_offs)
        q_rot, k_rot, iq_rot, ik_rot, iw = _rope_apply(p_dsa, small, tabs, dm, dsa_offs)
        o_dsa = _dsa(q_rot, k_rot, p_dsa, iq_rot, ik_rot, iw, dm, dsa_offs)
        merged = _merge(o_gla, o_dsa, o_hg, w_branch_gla, w_branch_dsa, w_branch_hgrn, l, p_hg,
                        (hg_offs["gate_a"], hg_offs["gate_b"], hg_offs["gate_c"]), dm.d_model)
        h = _matmul_residual(merged, w_out, l, h)
        last = l == dm.depth - 1
        res = _mlp(h, norm2_w[l], w_up_bf16, w_down_bf16, l, final_norm_w if last else norm1_w[l + 1], last)
        if not last:
            h, u = res
    return res.reshape(dm.batch, dm.seq, dm.d_model)


def kernel(x, positions, norm1_w, w_in, gla_gate_w2, gla_gate_b, gla_onorm_w, hgrn_lower_bounds, hgrn_onorm_w,
           w_branch_gla, w_branch_dsa, w_branch_hgrn, w_out, norm2_w, w_mlp_up, w_mlp_down, final_norm_w):
    return _forward(_prod_dims(), x, positions, norm1_w, w_in, gla_gate_w2, gla_gate_b, gla_onorm_w,
                    hgrn_lower_bounds, hgrn_onorm_w, w_branch_gla, w_branch_dsa, w_branch_hgrn, w_out, norm2_w,
                    w_mlp_up, w_mlp_down, final_norm_w)
```

```python
import functools
from typing import NamedTuple

import jax
import jax.numpy as jnp
from jax import lax
from jax.experimental import pallas as pl
from jax.experimental.pallas import tpu as pltpu

F32 = jnp.float32
BF16 = jnp.bfloat16
HIGHEST = lax.Precision.HIGHEST

ROPE_THETA = 10000.0
NORM_EPS = 1e-6
MASK_VALUE = -1e30
F_MIN = 1e-12
GLA_GATE_NORMALIZER = 16.0
LANES = 128
SUBLANES = 8
ROW_BLOCK = 128
INT_MIN = -2 ** 31
SAFE_CHUNK_DECAY = 150.0
VMEM_LIMIT = 48 * 1024 * 1024
VMEM_LIMIT_LARGE = 56 * 1024 * 1024


class _Dims(NamedTuple):
    d_model: int
    batch: int
    seq: int
    depth: int
    gla_heads: int
    gla_dk: int
    gla_dv: int
    gla_rank: int
    dsa_heads: int
    dsa_dim: int
    idx_heads: int
    idx_dim: int
    topk: int
    hg_heads: int
    hg_dk: int
    hg_dv: int
    d_ff: int
    chunk: int
    q_block: int


def _prod_dims():
    d = 2048
    return _Dims(d_model=d, batch=4, seq=2048, depth=2,
                 gla_heads=4, gla_dk=d // 2 // 4, gla_dv=d // 4, gla_rank=16,
                 dsa_heads=16, dsa_dim=128, idx_heads=8, idx_dim=64, topk=min(256, 2048 // 4),
                 hg_heads=d // 128, hg_dk=128, hg_dv=128, d_ff=4 * d, chunk=64, q_block=256)


def _in_sizes(dm):
    return (dm.gla_heads * dm.gla_dk, dm.gla_heads * dm.gla_dk, dm.gla_heads * dm.gla_dv,
            dm.gla_heads * dm.gla_dv, dm.gla_rank,
            dm.dsa_heads * dm.dsa_dim, dm.dsa_dim, dm.dsa_dim, dm.idx_heads * dm.idx_dim, dm.idx_dim,
            dm.idx_heads,
            dm.hg_heads * dm.hg_dk, dm.hg_heads * dm.hg_dk, dm.hg_heads * dm.hg_dv, dm.hg_heads * dm.hg_dv,
            dm.d_model, dm.d_model, dm.d_model)


_IN_NAMES = ("gla_q", "gla_k", "gla_v", "gla_g", "gla_a", "dsa_q", "dsa_k", "dsa_v", "idx_q", "idx_k",
             "idx_w", "hg_q", "hg_f", "hg_i", "hg_g", "gate_a", "gate_b", "gate_c")
_GROUPS = (("gla_q", "gla_k", "gla_v", "gla_g"), ("dsa_q", "dsa_k", "dsa_v"),
           ("hg_q", "hg_f", "hg_i", "hg_g", "gate_a", "gate_b", "gate_c"))


def _tile(n, pref):
    t = min(n, pref)
    while n % t:
        t //= 2
    return t


def _layout(dm):
    sizes = dict(zip(_IN_NAMES, _in_sizes(dm)))
    src, off = {}, 0
    for name in _IN_NAMES:
        src[name] = (off, sizes[name])
        off += sizes[name]
    groups = []
    for names in _GROUPS:
        start = src[names[0]][0]
        offs = {n: src[n][0] - start for n in names}
        width = src[names[-1]][0] + src[names[-1]][1] - start
        for n in names:
            assert offs[n] % min(sizes[n], 2 * LANES) == 0, n
        groups.append((start, width, offs))
    return src, groups


def _cparams(sem, vmem=VMEM_LIMIT):
    return pltpu.CompilerParams(dimension_semantics=sem, vmem_limit_bytes=vmem)


def _rms(x, w):
    ms = jnp.mean(x * x, axis=-1, keepdims=True)
    return x * lax.rsqrt(ms + NORM_EPS) * w


def _rmsnorm_kernel(x_ref, w_ref, o_ref):
    o_ref[...] = _rms(x_ref[...], w_ref[...]).astype(o_ref.dtype)


def _rmsnorm(x, w, out_dtype):
    m, d = x.shape
    tm = _tile(m, 512)
    return pl.pallas_call(
        _rmsnorm_kernel,
        grid=(m // tm,),
        in_specs=[pl.BlockSpec((tm, d), lambda i: (i, 0)), pl.BlockSpec((1, d), lambda i: (0, 0))],
        out_specs=pl.BlockSpec((tm, d), lambda i: (i, 0)),
        out_shape=jax.ShapeDtypeStruct((m, d), out_dtype),
        compiler_params=_cparams(("parallel",)),
        name="rmsnorm",
    )(x, w.reshape(1, d))


def _w_spec(w, layer, k, tn):
    if w.ndim == 3:
        return pl.BlockSpec((None, k, tn), lambda i, j: (layer, 0, j))
    return pl.BlockSpec((k, tn), lambda i, j: (0, j))


def _proj_kernel(x_ref, *refs, shift):
    *w_refs, o_ref = refs
    tn = o_ref.shape[1]
    if len(w_refs) == 1:
        w = w_refs[0][...]
    else:
        w = jnp.concatenate([r[...] for r in w_refs], axis=0)[shift:shift + tn, :]
    y = lax.dot_general(x_ref[...], w.astype(BF16), (((1,), (1,)), ((), ())), preferred_element_type=F32)
    o_ref[...] = y.astype(o_ref.dtype)


def _proj(x, w_t, layer, start, n, out_dtype, tm_pref=2048, tn_pref=512, name="proj"):
    m, k = x.shape
    tm, tn = _tile(m, tm_pref), _tile(n, tn_pref)
    if start % tn == 0:
        shift = 0
        specs = [pl.BlockSpec((None, tn, k), lambda i, j: (layer, start // tn + j, 0))]
    else:
        assert tn % ROW_BLOCK == 0 and start % SUBLANES == 0
        base, shift = divmod(start, ROW_BLOCK)
        per_tile = tn // ROW_BLOCK
        blk = lambda b: pl.BlockSpec((None, ROW_BLOCK, k), lambda i, j: (layer, base + j * per_tile + b, 0))
        specs = [blk(b) for b in range(per_tile + 1)]
    return pl.pallas_call(
        functools.partial(_proj_kernel, shift=shift),
        grid=(m // tm, n // tn),
        in_specs=[pl.BlockSpec((tm, k), lambda i, j: (i, 0))] + specs,
        out_specs=pl.BlockSpec((tm, tn), lambda i, j: (i, j)),
        out_shape=jax.ShapeDtypeStruct((m, n), out_dtype),
        compiler_params=_cparams(("parallel", "arbitrary")),
        name=name,
    )(x, *([w_t] * len(specs)))


def _matmul_residual_kernel(x_ref, w_ref, r_ref, o_ref):
    o_ref[...] = r_ref[...] + jnp.dot(x_ref[...], w_ref[...].astype(BF16), preferred_element_type=F32)


def _matmul_residual(x, w, layer, res, tm_pref=1024, tn_pref=512):
    m, k = x.shape
    n = w.shape[-1]
    tm, tn = _tile(m, tm_pref), _tile(n, tn_pref)
    return pl.pallas_call(
        _matmul_residual_kernel,
        grid=(m // tm, n // tn),
        in_specs=[pl.BlockSpec((tm, k), lambda i, j: (i, 0)), _w_spec(w, layer, k, tn),
                  pl.BlockSpec((tm, tn), lambda i, j: (i, j))],
        out_specs=pl.BlockSpec((tm, tn), lambda i, j: (i, j)),
        out_shape=jax.ShapeDtypeStruct((m, n), F32),
        compiler_params=_cparams(("parallel", "arbitrary")),
        name="out_proj",
    )(x, w, res)


def _sigmoid(x):
    return 1.0 / (1.0 + jnp.exp(-x))


def _merge_kernel(oa_ref, ob_ref, oc_ref, wa_ref, wb_ref, wc_ref, ga_ref, gb_ref, gc_ref, o_ref):
    def branch(o_ref_, w_ref_, g_ref_):
        y = jnp.dot(o_ref_[...], w_ref_[...].astype(BF16), preferred_element_type=F32)
        return _sigmoid(g_ref_[...].astype(F32)) * y

    acc = branch(oa_ref, wa_ref, ga_ref) + branch(ob_ref, wb_ref, gb_ref) + branch(oc_ref, wc_ref, gc_ref)
    o_ref[...] = acc.astype(o_ref.dtype)


def _merge(o_gla, o_dsa, o_hg, w_gla, w_dsa, w_hg, layer, gates, gate_offs, d_model):
    m = o_gla.shape[0]
    tm, tn = _tile(m, 1024), _tile(d_model, 256)
    o_spec = lambda a: pl.BlockSpec((tm, a.shape[1]), lambda i, j: (i, 0))

    def g_spec(off):
        assert off % tn == 0
        return pl.BlockSpec((tm, tn), lambda i, j: (i, off // tn + j))

    return pl.pallas_call(
        _merge_kernel,
        grid=(m // tm, d_model // tn),
        in_specs=[o_spec(o_gla), o_spec(o_dsa), o_spec(o_hg),
                  _w_spec(w_gla, layer, w_gla.shape[-2], tn), _w_spec(w_dsa, layer, w_dsa.shape[-2], tn),
                  _w_spec(w_hg, layer, w_hg.shape[-2], tn),
                  g_spec(gate_offs[0]), g_spec(gate_offs[1]), g_spec(gate_offs[2])],
        out_specs=pl.BlockSpec((tm, tn), lambda i, j: (i, j)),
        out_shape=jax.ShapeDtypeStruct((m, d_model), BF16),
        compiler_params=_cparams(("parallel", "arbitrary"), VMEM_LIMIT_LARGE),
        name="merge",
    )(o_gla, o_dsa, o_hg, w_gla, w_dsa, w_hg, gates, gates, gates)


def _mlp_kernel(h_ref, nw_ref, wu_ref, wd_ref, nnw_ref, o_ref, *rest, last):
    u_ref = rest[-1]
    j = pl.program_id(1)

    @pl.when(j == 0)
    def _():
        x = h_ref[...]
        u_ref[...] = _rms(x, nw_ref[...]).astype(BF16)
        o_ref[...] = x

    a = jnp.dot(u_ref[...], wu_ref[...].astype(BF16), preferred_element_type=F32)
    a = jnp.square(jnp.maximum(a, 0.0)).astype(BF16)
    o_ref[...] += jnp.dot(a, wd_ref[...].astype(BF16), preferred_element_type=F32)

    @pl.when(j == pl.num_programs(1) - 1)
    def _():
        y = _rms(o_ref[...], nnw_ref[...])
        if last:
            o_ref[...] = y
        else:
            rest[0][...] = y.astype(BF16)


def _mlp(h, norm_w, w_up, w_down, layer, next_norm_w, last):
    m, d = h.shape
    f = w_up.shape[-1]
    tm, tf = _tile(m, 512), _tile(f, 1024)
    row = pl.BlockSpec((tm, d), lambda i, j: (i, 0))
    vec = pl.BlockSpec((1, d), lambda i, j: (0, 0))
    out_shape = [jax.ShapeDtypeStruct((m, d), F32)] + ([] if last else [jax.ShapeDtypeStruct((m, d), BF16)])
    res = pl.pallas_call(
        functools.partial(_mlp_kernel, last=last),
        grid=(m // tm, f // tf),
        in_specs=[row, vec, pl.BlockSpec((None, d, tf), lambda i, j: (layer, 0, j)),
                  pl.BlockSpec((None, tf, d), lambda i, j: (layer, j, 0)), vec],
        out_specs=[row] * len(out_shape),
        out_shape=out_shape,
        scratch_shapes=[pltpu.VMEM((tm, d), BF16)],
        compiler_params=_cparams(("parallel", "arbitrary"), VMEM_LIMIT_LARGE),
        name="mlp",
    )(h, norm_w.reshape(1, d), w_up, w_down, next_norm_w.reshape(1, d))
    return res[0] if last else res


def _chunk_head(qs, kk, vv, b, st_ref, head, factored, kb_ref):
    c, kdim = qs.shape
    row = lax.broadcasted_iota(jnp.int32, (c, c), 0)
    col = lax.broadcasted_iota(jnp.int32, (c, c), 1)
    b_last = b[c - 1:c, :]
    st = st_ref[head]
    nt = (((1,), (1,)), ((), ()))
    if factored:
        ref_row = b[c // 2 - 1:c // 2, :]
        qd = qs * jnp.exp(b - ref_row)
        kd = kk * jnp.exp(ref_row - b)
        q_in = qd * jnp.exp(ref_row)
        k_dec = kd * jnp.exp(b_last - ref_row)
        att = lax.dot_general(qd.astype(BF16), kd.astype(BF16), nt, preferred_element_type=F32)
    else:
        q_in = qs * jnp.exp(b)
        k_dec = kk * jnp.exp(b_last - b)
        kb_ref[0, :, 0:kdim] = kk
        kb_ref[1, :, 0:kdim] = b

        def body(s, att):
            k_row = kb_ref[0, pl.ds(s, 1), 0:kdim]
            b_row = kb_ref[1, pl.ds(s, 1), 0:kdim]
            w = jnp.sum(qs * k_row * jnp.exp(jnp.minimum(b - b_row, 0.0)), axis=-1, keepdims=True)
            return jnp.where(col == s, w, att)

        att = lax.fori_loop(0, c, body, jnp.zeros((c, c), F32))
    att = jnp.where(col <= row, att, 0.0)
    o = lax.dot_general(q_in.astype(BF16), st.astype(BF16), nt, preferred_element_type=F32)
    o = o + jnp.dot(att.astype(BF16), vv.astype(BF16), preferred_element_type=F32)
    upd = lax.dot_general(vv.astype(BF16), k_dec.astype(BF16), (((0,), (0,)), ((), ())),
                          preferred_element_type=F32)
    st_ref[head] = st * jnp.exp(b_last) + upd
    return o


def _cumsum_chunks(x, nb, c):
    row = lax.broadcasted_iota(jnp.int32, (c, c), 0)
    col = lax.broadcasted_iota(jnp.int32, (c, c), 1)
    tri = jnp.where(col <= row, 1.0, 0.0).astype(F32)
    return [jnp.dot(tri, x[bi * c:(bi + 1) * c, :], preferred_element_type=F32, precision=HIGHEST)
            for bi in range(nb)]


def _chunk_is_safe(b_all):
    c = b_all[0].shape[0]
    worst = functools.reduce(jnp.minimum, [b[c - 1:c, :] for b in b_all])
    return jnp.max(-worst) <= SAFE_CHUNK_DECAY


def _log_sigmoid(x):
    return jnp.minimum(x, 0.0) - jnp.log(1.0 + jnp.exp(-jnp.abs(x)))


def _gla_kernel(q_ref, k_ref, v_ref, g_ref, a_ref, w2_ref, b2_ref, nw_ref, o_ref, st_ref, kb_ref, *,
                heads, dk, dv):
    nb, c = q_ref.shape[0], q_ref.shape[1]

    @pl.when(pl.program_id(0) == 0)
    def _():
        st_ref[...] = jnp.zeros_like(st_ref)

    a = a_ref[...].reshape(nb * c, a_ref.shape[2])
    z = jnp.dot(a, w2_ref[...], preferred_element_type=F32, precision=HIGHEST) + b2_ref[...]
    b_all = _cumsum_chunks(_log_sigmoid(z) * (1.0 / GLA_GATE_NORMALIZER), nb, c)

    def step(factored):
        for bi in range(nb):
            for h in range(heads):
                qs = q_ref[bi, :, h * dk:(h + 1) * dk].astype(F32) * (dk ** -0.5)
                kk = k_ref[bi, :, h * dk:(h + 1) * dk].astype(F32)
                vv = v_ref[bi, :, h * dv:(h + 1) * dv]
                o = _chunk_head(qs, kk, vv, b_all[bi][:, h * dk:(h + 1) * dk], st_ref, bi * heads + h, factored,
                                kb_ref)
                g = g_ref[bi, :, h * dv:(h + 1) * dv].astype(F32)
                o = _rms(o, nw_ref[...]) * (g * _sigmoid(g))
                o_ref[bi, :, h * dv:(h + 1) * dv] = o.astype(o_ref.dtype)

    lax.cond(_chunk_is_safe(b_all), functools.partial(step, True), functools.partial(step, False))


def _chunk_specs(nb, c, arr, offs):
    def seg(name, width):
        assert offs[name] % width == 0
        blk = offs[name] // width
        return pl.BlockSpec((nb, c, width), lambda ci: (0, ci, blk))

    return seg, arr.reshape(nb, -1, arr.shape[-1])


def _gla(p_gla, a_low, w2p, b2, onorm_w, dm, offs):
    nb, c = dm.batch, dm.chunk
    hk, hv = dm.gla_heads * dm.gla_dk, dm.gla_heads * dm.gla_dv
    seg, p3 = _chunk_specs(nb, c, p_gla, offs)
    a3 = a_low.reshape(nb, dm.seq, LANES)
    const = lambda shape: pl.BlockSpec(shape, lambda ci: (0, 0))
    out = pl.pallas_call(
        functools.partial(_gla_kernel, heads=dm.gla_heads, dk=dm.gla_dk, dv=dm.gla_dv),
        grid=(dm.seq // c,),
        in_specs=[seg("gla_q", hk), seg("gla_k", hk), seg("gla_v", hv), seg("gla_g", hv),
                  pl.BlockSpec((nb, c, LANES), lambda ci: (0, ci, 0)),
                  const((LANES, hk)), const((1, hk)), const((1, dm.gla_dv))],
        out_specs=pl.BlockSpec((nb, c, hv), lambda ci: (0, ci, 0)),
        out_shape=jax.ShapeDtypeStruct((nb, dm.seq, hv), BF16),
        scratch_shapes=[pltpu.VMEM((nb * dm.gla_heads, dm.gla_dv, dm.gla_dk), F32),
                        pltpu.VMEM((2, c, dm.gla_dk), F32)],
        compiler_params=_cparams(("arbitrary",)),
        name="gla",
    )(p3, p3, p3, p3, a3, w2p, b2.reshape(1, hk), onorm_w.reshape(1, dm.gla_dv))
    return out.reshape(nb * dm.seq, hv)


def _hgrn_kernel(q_ref, f_ref, i_ref, g_ref, lbp_ref, nw_ref, o_ref, st_ref, kb_ref, *, layer, heads, dk, dv):
    nb, c = q_ref.shape[0], q_ref.shape[1]

    @pl.when(pl.program_id(0) == 0)
    def _():
        st_ref[...] = jnp.zeros_like(st_ref)

    lbp = lbp_ref[...]
    e = jnp.exp(lbp - jnp.max(lbp, axis=0, keepdims=True))
    p = e / jnp.sum(e, axis=0, keepdims=True)
    lb = jnp.zeros_like(p[0:1, :])
    for j in range(1, layer + 1):
        lb = lb + p[j:j + 1, :]
    sig_f = _sigmoid(f_ref[...].reshape(nb * c, f_ref.shape[2]).astype(F32))
    f_gate = lb + (1.0 - lb) * sig_f
    b_all = _cumsum_chunks(jnp.log(jnp.maximum(f_gate, F_MIN)), nb, c)
    k_all = (1.0 - lb) * (1.0 - sig_f)

    def step(factored):
        for bi in range(nb):
            for h in range(heads):
                q = q_ref[bi, :, h * dk:(h + 1) * dk].astype(F32)
                qs = q * _sigmoid(q) * (dk ** -0.5)
                vv = i_ref[bi, :, h * dv:(h + 1) * dv]
                o = _chunk_head(qs, k_all[bi * c:(bi + 1) * c, h * dk:(h + 1) * dk], vv,
                                b_all[bi][:, h * dk:(h + 1) * dk], st_ref, bi * heads + h, factored, kb_ref)
                g = g_ref[bi, :, h * dv:(h + 1) * dv].astype(F32)
                o = _rms(o, nw_ref[...]) * _sigmoid(g)
                o_ref[bi, :, h * dv:(h + 1) * dv] = o.astype(o_ref.dtype)

    lax.cond(_chunk_is_safe(b_all), functools.partial(step, True), functools.partial(step, False))


def _hgrn(p_hg, lower_bounds, onorm_w, layer, dm, offs):
    nb, c = dm.batch, dm.chunk
    hk, hv = dm.hg_heads * dm.hg_dk, dm.hg_heads * dm.hg_dv
    seg, p3 = _chunk_specs(nb, c, p_hg, offs)
    const = lambda shape: pl.BlockSpec(shape, lambda ci: (0, 0))
    out = pl.pallas_call(
        functools.partial(_hgrn_kernel, layer=layer, heads=dm.hg_heads, dk=dm.hg_dk, dv=dm.hg_dv),
        grid=(dm.seq // c,),
        in_specs=[seg("hg_q", hk), seg("hg_f", hk), seg("hg_i", hv), seg("hg_g", hv),
                  const((dm.depth, hk)), const((1, dm.hg_dv))],
        out_specs=pl.BlockSpec((nb, c, hv), lambda ci: (0, ci, 0)),
        out_shape=jax.ShapeDtypeStruct((nb, dm.seq, hv), BF16),
        scratch_shapes=[pltpu.VMEM((nb * dm.hg_heads, dm.hg_dv, dm.hg_dk), F32),
                        pltpu.VMEM((2, c, dm.hg_dk), F32)],
        compiler_params=_cparams(("arbitrary",)),
        name="hgrn",
    )(p3, p3, p3, p3, lower_bounds, onorm_w.reshape(1, dm.hg_dv))
    return out.reshape(nb * dm.seq, hv)


def _rope_table_kernel(pos_ref, f_ref, sgn_ref, cos_ref, sin_ref):
    ang = pos_ref[...].astype(F32) * f_ref[...]
    cos_ref[...] = jnp.cos(ang)
    sin_ref[...] = jnp.sin(ang) * sgn_ref[...]


def _rope_tables(pos_col, inv_freq_lanes, sign_lanes):
    m = pos_col.shape[0]
    tm = _tile(m, 512)
    lane = pl.BlockSpec((1, LANES), lambda i: (0, 0))
    tab = pl.BlockSpec((tm, LANES), lambda i: (i, 0))
    return pl.pallas_call(
        _rope_table_kernel,
        grid=(m // tm,),
        in_specs=[pl.BlockSpec((tm, 1), lambda i: (i, 0)), lane, lane],
        out_specs=[tab, tab],
        out_shape=[jax.ShapeDtypeStruct((m, LANES), F32)] * 2,
        compiler_params=_cparams(("parallel",)),
        name="rope_tables",
    )(pos_col, inv_freq_lanes, sign_lanes)


def _swap_halves(x, half):
    n = x.shape[-1]
    if 2 * half == n:
        return pltpu.roll(x, half, axis=1)
    lane = lax.broadcasted_iota(jnp.int32, x.shape, 1)
    return jnp.where((lane & half) == 0, pltpu.roll(x, n - half, axis=1), pltpu.roll(x, half, axis=1))


def _rope_apply_kernel(q_ref, k_ref, s_ref, c1_ref, s1_ref, c2_ref, s2_ref, qo_ref, ko_ref, iqo_ref,
                       iko_ref, iwo_ref, *, heads, dim, idx_heads, idx_dim):
    c1, s1, c2, s2 = c1_ref[...], s1_ref[...], c2_ref[...], s2_ref[...]
    q_scale = dim ** -0.5
    for h in range(heads):
        x = q_ref[:, h * dim:(h + 1) * dim].astype(F32)
        qo_ref[:, h * dim:(h + 1) * dim] = ((x * c1 + _swap_halves(x, dim // 2) * s1) * q_scale).astype(BF16)
    x = k_ref[...].astype(F32)
    ko_ref[...] = (x * c1 + _swap_halves(x, dim // 2) * s1).astype(BF16)
    iq_scale = idx_dim ** -0.5
    for g in range(idx_heads * idx_dim // LANES):
        x = s_ref[:, g * LANES:(g + 1) * LANES]
        r = (x * c2 + _swap_halves(x, idx_dim // 2) * s2) * iq_scale
        iqo_ref[:, g * LANES:(g + 1) * LANES] = r.astype(BF16)
    off = idx_heads * idx_dim
    x = s_ref[:, off:off + LANES]
    r = x * c2 + _swap_halves(x, idx_dim // 2) * s2
    lane = lax.broadcasted_iota(jnp.int32, x.shape, 1)
    iko_ref[...] = jnp.where(lane < idx_dim, r, pltpu.roll(r, idx_dim, axis=1)).astype(BF16)
    iwo_ref[...] = jnp.where(lane < idx_heads, pltpu.roll(x, LANES - idx_dim, axis=1), 0.0) * (idx_heads ** -0.5)


def _rope_apply(p_dsa, idx, tabs, dm, offs):
    m = p_dsa.shape[0]
    tm = _tile(m, 256)
    hd = dm.dsa_heads * dm.dsa_dim
    iq = dm.idx_heads * dm.idx_dim
    assert dm.dsa_dim == LANES and 2 * dm.idx_dim == LANES and dm.idx_heads <= dm.idx_dim
    assert offs["dsa_q"] % hd == 0 and offs["dsa_k"] % dm.dsa_dim == 0
    rows = lambda w, blk=0: pl.BlockSpec((tm, w), lambda i: (i, blk))
    return pl.pallas_call(
        functools.partial(_rope_apply_kernel, heads=dm.dsa_heads, dim=dm.dsa_dim, idx_heads=dm.idx_heads,
                          idx_dim=dm.idx_dim),
        grid=(m // tm,),
        in_specs=[rows(hd, offs["dsa_q"] // hd), rows(dm.dsa_dim, offs["dsa_k"] // dm.dsa_dim),
                  rows(idx.shape[1]), rows(LANES), rows(LANES), rows(LANES), rows(LANES)],
        out_specs=[rows(hd), rows(dm.dsa_dim), rows(iq), rows(LANES), rows(LANES)],
        out_shape=[jax.ShapeDtypeStruct((m, hd), BF16), jax.ShapeDtypeStruct((m, dm.dsa_dim), BF16),
                   jax.ShapeDtypeStruct((m, iq), BF16), jax.ShapeDtypeStruct((m, LANES), BF16),
                   jax.ShapeDtypeStruct((m, LANES), F32)],
        compiler_params=_cparams(("parallel",)),
        name="rope_apply",
    )(p_dsa, p_dsa, idx, *tabs)


def _ind(mask):
    return jnp.where(mask, 1.0, 0.0)


def _row_sum(x):
    return jnp.sum(x, axis=-1, keepdims=True)


def _float_of_ordered(u):
    k = u ^ jnp.int32(INT_MIN)
    return pltpu.bitcast(jnp.where(k < 0, k ^ jnp.int32(0x7FFFFFFF), k), F32)


def _dsa_kernel(q_ref, iq_ref, iw_ref, k_ref, v_ref, ik_ref, o_ref, *, heads, dim, idx_heads, idx_dim, topk,
                q_start, n_hidden):
    tq = q_ref.shape[0]
    tk = k_ref.shape[0]
    nt = (((1,), (1,)), ((), ()))
    ik = ik_ref[...]
    lane = lax.broadcasted_iota(jnp.int32, (tq, LANES), 1)
    score = jnp.zeros((tq, tk), F32)
    for h in range(idx_heads):
        g = (h * idx_dim) // LANES
        lo = (h * idx_dim) % LANES
        x = iq_ref[:, g * LANES:(g + 1) * LANES]
        x = jnp.where(lane >= lo, jnp.where(lane < lo + idx_dim, x, jnp.zeros_like(x)), jnp.zeros_like(x))
        rel = lax.dot_general(x, ik, nt, preferred_element_type=F32)
        score = score + iw_ref[:, h:h + 1] * jnp.maximum(rel, 0.0)
    qpos = q_start + lax.broadcasted_iota(jnp.int32, (tq, 1), 0)
    kpos = lax.broadcasted_iota(jnp.int32, (1, tk), 1)
    allowed = kpos <= qpos
    score = jnp.where(allowed, score, MASK_VALUE)
    hidden = float(n_hidden)

    def thr_body(it, prefix):
        trial = prefix | lax.shift_left(jnp.int32(1), 31 - it)
        cand = _float_of_ordered(trial)
        cnt = _row_sum(_ind(score >= cand)) + jnp.where(MASK_VALUE >= cand, hidden, 0.0)
        return jnp.where(cnt >= topk, trial, prefix)

    thr = _float_of_ordered(lax.fori_loop(0, 32, thr_body, jnp.zeros((tq, 1), jnp.int32)))
    above = _ind(score > thr)
    need = topk - _row_sum(above) - jnp.where(MASK_VALUE > thr, hidden, 0.0)
    tie = jnp.where(allowed, _ind(score == thr), 0.0)

    def all_ties():
        return jnp.where(allowed, _ind(score >= thr), 0.0)

    def ordered_ties():
        nbits = (tk - 1).bit_length()

        def tie_body(it, j):
            trial = j | lax.shift_left(jnp.int32(1), nbits - 1 - it)
            cnt = _row_sum(jnp.where(kpos < trial, tie, 0.0))
            return jnp.where(cnt < need, trial, j)

        j_last = lax.fori_loop(0, nbits, tie_body, jnp.zeros((tq, 1), jnp.int32))
        return jnp.where(allowed, above, 0.0) + jnp.where(kpos <= j_last, tie, 0.0)

    valid = lax.cond(jnp.max(_row_sum(tie) - need) > 0.0, ordered_ties, all_ties) > 0.0

    k = k_ref[...]
    v_ones = jnp.concatenate([v_ref[...], jnp.ones((tk, dim), BF16)], axis=1)
    for h in range(heads):
        s = lax.dot_general(q_ref[:, h * dim:(h + 1) * dim], k, nt, preferred_element_type=F32)
        s = jnp.where(valid, s, MASK_VALUE)
        p = jnp.exp(s - jnp.max(s, axis=-1, keepdims=True)).astype(BF16)
        o = jnp.dot(p, v_ones, preferred_element_type=F32)
        o_ref[:, h * dim:(h + 1) * dim] = (o[:, 0:dim] / o[:, dim:dim + 1]).astype(o_ref.dtype)


def _with_carried_output(kern, q_ref, iq_ref, iw_ref, k_ref, v_ref, ik_ref, carried_ref, o_ref):
    del carried_ref
    kern(q_ref, iq_ref, iw_ref, k_ref, v_ref, ik_ref, o_ref)


def _dsa(q_rot, k_rot, p_dsa, iq_rot, ik_rot, iw, dm, offs):
    tq, t, nb = dm.q_block, dm.seq, dm.batch
    hd = dm.dsa_heads * dm.dsa_dim
    assert offs["dsa_v"] % dm.dsa_dim == 0 and t % tq == 0
    v_blk = offs["dsa_v"] // dm.dsa_dim
    per_batch = lambda a: a.reshape(nb, t, a.shape[-1])
    q3, k3, p3, iq3, ik3, iw3 = map(per_batch, (q_rot, k_rot, p_dsa, iq_rot, ik_rot, iw))
    out = None
    for g in range(t // tq):
        tk = (g + 1) * tq
        qrow = lambda w, g=g: pl.BlockSpec((None, tq, w), lambda bi: (bi, g, 0))
        krow = lambda w, blk=0, tk=tk: pl.BlockSpec((None, tk, w), lambda bi: (bi, 0, blk))
        kern = functools.partial(_dsa_kernel, heads=dm.dsa_heads, dim=dm.dsa_dim, idx_heads=dm.idx_heads,
                                 idx_dim=dm.idx_dim, topk=dm.topk, q_start=g * tq, n_hidden=t - tk)
        args = [q3, iq3, iw3, k3, p3, ik3]
        in_specs = [qrow(hd), qrow(iq3.shape[-1]), qrow(LANES), krow(dm.dsa_dim), krow(dm.dsa_dim, v_blk),
                    krow(LANES)]
        if out is not None:
            kern = functools.partial(_with_carried_output, kern)
            args.append(out)
            in_specs.append(pl.BlockSpec(memory_space=pl.ANY))
        out = pl.pallas_call(
            kern,
            grid=(nb,),
            in_specs=in_specs,
            out_specs=qrow(hd),
            out_shape=jax.ShapeDtypeStruct((nb, t, hd), BF16),
            input_output_aliases={} if len(args) == 6 else {6: 0},
            compiler_params=_cparams(("parallel",)),
            name=f"dsa_q{g}",
        )(*args)
    return out.reshape(nb * t, hd)


def _forward(dm, x, positions, norm1_w, w_in, gla_gate_w2, gla_gate_b, gla_onorm_w, hgrn_lower_bounds,
             hgrn_onorm_w, w_branch_gla, w_branch_dsa, w_branch_hgrn, w_out, norm2_w, w_mlp_up, w_mlp_down,
             final_norm_w):
    m = dm.batch * dm.seq
    h = x.reshape(m, dm.d_model)
    src, ((gla_start, gla_width, gla_offs), (dsa_start, dsa_width, dsa_offs), (hg_start, hg_width, hg_offs)) = \
        _layout(dm)
    idx_start = src["idx_q"][0]
    idx_width = -(-(src["idx_w"][0] + src["idx_w"][1] - idx_start) // LANES) * LANES
    w_t = jnp.swapaxes(w_in, 1, 2)

    def lanes(d):
        inv = ROPE_THETA ** (-jnp.arange(0, d, 2, dtype=F32) / d)
        reps = LANES // d
        f = jnp.tile(jnp.concatenate([inv, inv]), reps).reshape(1, LANES)
        sgn = jnp.tile(jnp.concatenate([-jnp.ones(d // 2, F32), jnp.ones(d // 2, F32)]), reps).reshape(1, LANES)
        return f, sgn

    pos_col = positions.reshape(m, 1)
    tabs = _rope_tables(pos_col, *lanes(dm.dsa_dim)) + _rope_tables(pos_col, *lanes(dm.idx_dim))

    w_up_bf16, w_down_bf16 = w_mlp_up.astype(BF16), w_mlp_down.astype(BF16)
    u = _rmsnorm(h, norm1_w[0], BF16)
    for l in range(dm.depth):
        w2p = jnp.concatenate(
            [gla_gate_w2[l], jnp.zeros((LANES - dm.gla_rank, gla_gate_w2.shape[2]), F32)], axis=0)
        p_gla = _proj(u, w_t, l, gla_start, gla_width, BF16, name="in_proj_gla")
        p_hg = _proj(u, w_t, l, hg_start, hg_width, BF16, name="in_proj_hg")
        p_dsa = _proj(u, w_t, l, dsa_start, dsa_width, BF16, tn_pref=256, name="in_proj_dsa")
        a_low = _proj(u, w_t, l, src["gla_a"][0], LANES, F32, name="in_proj_gate")
        idx = _proj(u, w_t, l, idx_start, idx_width, F32, tn_pref=LANES, name="in_proj_idx")
        o_gla = _gla(p_gla, a_low, w2p, gla_gate_b[l], gla_onorm_w[l], dm, gla_offs)
        o_hg = _hgrn(p_hg, hgrn_lower_bounds, hgrn_onorm_w[l], l, dm, hg_offs)
        q_rot, k_rot, iq_rot, ik_rot, iw = _rope_apply(p_dsa, idx, tabs, dm, dsa_offs)
        o_dsa = _dsa(q_rot, k_rot, p_dsa, iq_rot, ik_rot, iw, dm, dsa_offs)
        merged = _merge(o_gla, o_dsa, o_hg, w_branch_gla, w_branch_dsa, w_branch_hgrn, l, p_hg,
                        (hg_offs["gate_a"], hg_offs["gate_b"], hg_offs["gate_c"]), dm.d_model)
        h = _matmul_residual(merged, w_out, l, h)
        last = l == dm.depth - 1
        res = _mlp(h, norm2_w[l], w_up_bf16, w_down_bf16, l, final_norm_w if last else norm1_w[l + 1], last)
        if not last:
            h, u = res
    return res.reshape(dm.batch, dm.seq, dm.d_model)


def kernel(x, positions, norm1_w, w_in, gla_gate_w2, gla_gate_b, gla_onorm_w, hgrn_lower_bounds, hgrn_onorm_w,
           w_branch_gla, w_branch_dsa, w_branch_hgrn, w_out, norm2_w, w_mlp_up, w_mlp_down, final_norm_w):
    return _forward(_prod_dims(), x, positions, norm1_w, w_in, gla_gate_w2, gla_gate_b, gla_onorm_w,
                    hgrn_lower_bounds, hgrn_onorm_w, w_branch_gla, w_branch_dsa, w_branch_hgrn, w_out, norm2_w,
                    w_mlp_up, w_mlp_down, final_norm_w)
```

```python
import functools
from typing import NamedTuple

import jax
import jax.numpy as jnp
from jax import lax
from jax.experimental import pallas as pl
from jax.experimental.pallas import tpu as pltpu

F32 = jnp.float32
BF16 = jnp.bfloat16
HIGHEST = lax.Precision.HIGHEST

ROPE_THETA = 10000.0
NORM_EPS = 1e-6
MASK_VALUE = -1e30
F_MIN = 1e-12
GLA_GATE_NORMALIZER = 16.0
LANES = 128
SUBLANES = 8
ROW_BLOCK = 128
INT_MIN = -2 ** 31
SAFE_CHUNK_DECAY = 150.0
VMEM_LIMIT = 48 * 1024 * 1024
VMEM_LIMIT_LARGE = 56 * 1024 * 1024


class _Dims(NamedTuple):
    d_model: int
    batch: int
    seq: int
    depth: int
    gla_heads: int
    gla_dk: int
    gla_dv: int
    gla_rank: int
    dsa_heads: int
    dsa_dim: int
    idx_heads: int
    idx_dim: int
    topk: int
    hg_heads: int
    hg_dk: int
    hg_dv: int
    d_ff: int
    chunk: int
    q_block: int


def _prod_dims():
    d = 2048
    return _Dims(d_model=d, batch=4, seq=2048, depth=2,
                 gla_heads=4, gla_dk=d // 2 // 4, gla_dv=d // 4, gla_rank=16,
                 dsa_heads=16, dsa_dim=128, idx_heads=8, idx_dim=64, topk=min(256, 2048 // 4),
                 hg_heads=d // 128, hg_dk=128, hg_dv=128, d_ff=4 * d, chunk=64, q_block=256)


def _in_sizes(dm):
    return (dm.gla_heads * dm.gla_dk, dm.gla_heads * dm.gla_dk, dm.gla_heads * dm.gla_dv,
            dm.gla_heads * dm.gla_dv, dm.gla_rank,
            dm.dsa_heads * dm.dsa_dim, dm.dsa_dim, dm.dsa_dim, dm.idx_heads * dm.idx_dim, dm.idx_dim,
            dm.idx_heads,
            dm.hg_heads * dm.hg_dk, dm.hg_heads * dm.hg_dk, dm.hg_heads * dm.hg_dv, dm.hg_heads * dm.hg_dv,
            dm.d_model, dm.d_model, dm.d_model)


_IN_NAMES = ("gla_q", "gla_k", "gla_v", "gla_g", "gla_a", "dsa_q", "dsa_k", "dsa_v", "idx_q", "idx_k",
             "idx_w", "hg_q", "hg_f", "hg_i", "hg_g", "gate_a", "gate_b", "gate_c")
_GROUPS = (("gla_q", "gla_k", "gla_v", "gla_g"), ("dsa_q", "dsa_k", "dsa_v"),
           ("hg_q", "hg_f", "hg_i", "hg_g", "gate_a", "gate_b", "gate_c"))


def _tile(n, pref):
    t = min(n, pref)
    while n % t:
        t //= 2
    return t


def _layout(dm):
    sizes = dict(zip(_IN_NAMES, _in_sizes(dm)))
    src, off = {}, 0
    for name in _IN_NAMES:
        src[name] = (off, sizes[name])
        off += sizes[name]
    groups = []
    for names in _GROUPS:
        start = src[names[0]][0]
        offs = {n: src[n][0] - start for n in names}
        width = src[names[-1]][0] + src[names[-1]][1] - start
        for n in names:
            assert offs[n] % min(sizes[n], 2 * LANES) == 0, n
        groups.append((start, width, offs))
    return src, groups


def _cparams(sem, vmem=VMEM_LIMIT):
    return pltpu.CompilerParams(dimension_semantics=sem, vmem_limit_bytes=vmem)


def _rms(x, w):
    ms = jnp.mean(x * x, axis=-1, keepdims=True)
    return x * lax.rsqrt(ms + NORM_EPS) * w


def _rmsnorm_kernel(x_ref, w_ref, o_ref):
    o_ref[...] = _rms(x_ref[...], w_ref[...]).astype(o_ref.dtype)


def _rmsnorm(x, w, out_dtype):
    m, d = x.shape
    tm = _tile(m, 512)
    return pl.pallas_call(
        _rmsnorm_kernel,
        grid=(m // tm,),
        in_specs=[pl.BlockSpec((tm, d), lambda i: (i, 0)), pl.BlockSpec((1, d), lambda i: (0, 0))],
        out_specs=pl.BlockSpec((tm, d), lambda i: (i, 0)),
        out_shape=jax.ShapeDtypeStruct((m, d), out_dtype),
        compiler_params=_cparams(("parallel",)),
        name="rmsnorm",
    )(x, w.reshape(1, d))


def _w_spec(w, layer, k, tn):
    if w.ndim == 3:
        return pl.BlockSpec((None, k, tn), lambda i, j: (layer, 0, j))
    return pl.BlockSpec((k, tn), lambda i, j: (0, j))


def _proj_kernel(x_ref, *refs, shift):
    *w_refs, o_ref = refs
    tn = o_ref.shape[1]
    if len(w_refs) == 1:
        w = w_refs[0][...]
    else:
        w = jnp.concatenate([r[...] for r in w_refs], axis=0)[shift:shift + tn, :]
    y = lax.dot_general(x_ref[...], w.astype(BF16), (((1,), (1,)), ((), ())), preferred_element_type=F32)
    o_ref[...] = y.astype(o_ref.dtype)


def _proj(x, w_t, layer, start, n, out_dtype, tm_pref=2048, tn_pref=512, name="proj"):
    m, k = x.shape
    tm, tn = _tile(m, tm_pref), _tile(n, tn_pref)
    if start % tn == 0:
        shift = 0
        specs = [pl.BlockSpec((None, tn, k), lambda i, j: (layer, start // tn + j, 0))]
    else:
        assert tn % ROW_BLOCK == 0 and start % SUBLANES == 0
        base, shift = divmod(start, ROW_BLOCK)
        per_tile = tn // ROW_BLOCK
        blk = lambda b: pl.BlockSpec((None, ROW_BLOCK, k), lambda i, j: (layer, base + j * per_tile + b, 0))
        specs = [blk(b) for b in range(per_tile + 1)]
    return pl.pallas_call(
        functools.partial(_proj_kernel, shift=shift),
        grid=(m // tm, n // tn),
        in_specs=[pl.BlockSpec((tm, k), lambda i, j: (i, 0))] + specs,
        out_specs=pl.BlockSpec((tm, tn), lambda i, j: (i, j)),
        out_shape=jax.ShapeDtypeStruct((m, n), out_dtype),
        compiler_params=_cparams(("parallel", "arbitrary")),
        name=name,
    )(x, *([w_t] * len(specs)))


def _sigmoid(x):
    return 1.0 / (1.0 + jnp.exp(-x))


def _merge_kernel(oa_ref, ob_ref, oc_ref, wa_ref, wb_ref, wc_ref, ga_ref, gb_ref, gc_ref, o_ref):
    def branch(o_ref_, w_ref_, g_ref_):
        y = jnp.dot(o_ref_[...], w_ref_[...].astype(BF16), preferred_element_type=F32)
        return _sigmoid(g_ref_[...].astype(F32)) * y

    acc = branch(oa_ref, wa_ref, ga_ref) + branch(ob_ref, wb_ref, gb_ref) + branch(oc_ref, wc_ref, gc_ref)
    o_ref[...] = acc.astype(o_ref.dtype)


def _merge(o_gla, o_dsa, o_hg, w_gla, w_dsa, w_hg, layer, gates, gate_offs, d_model):
    m = o_gla.shape[0]
    tm, tn = _tile(m, 1024), _tile(d_model, 256)
    o_spec = lambda a: pl.BlockSpec((tm, a.shape[1]), lambda i, j: (i, 0))

    def g_spec(off):
        assert off % tn == 0
        return pl.BlockSpec((tm, tn), lambda i, j: (i, off // tn + j))

    return pl.pallas_call(
        _merge_kernel,
        grid=(m // tm, d_model // tn),
        in_specs=[o_spec(o_gla), o_spec(o_dsa), o_spec(o_hg),
                  _w_spec(w_gla, layer, w_gla.shape[-2], tn), _w_spec(w_dsa, layer, w_dsa.shape[-2], tn),
                  _w_spec(w_hg, layer, w_hg.shape[-2], tn),
                  g_spec(gate_offs[0]), g_spec(gate_offs[1]), g_spec(gate_offs[2])],
        out_specs=pl.BlockSpec((tm, tn), lambda i, j: (i, j)),
        out_shape=jax.ShapeDtypeStruct((m, d_model), BF16),
        compiler_params=_cparams(("parallel", "arbitrary"), VMEM_LIMIT_LARGE),
        name="merge",
    )(o_gla, o_dsa, o_hg, w_gla, w_dsa, w_hg, gates, gates, gates)


def _mlp_kernel(m_ref, wo_ref, h_ref, nw_ref, wu_ref, wd_ref, nnw_ref, o_ref, *rest, last):
    u_ref = rest[-1]
    j = pl.program_id(1)

    @pl.when(j == 0)
    def _():
        x = h_ref[...] + jnp.dot(m_ref[...], wo_ref[...], preferred_element_type=F32)
        u_ref[...] = _rms(x, nw_ref[...]).astype(BF16)
        o_ref[...] = x

    a = jnp.dot(u_ref[...], wu_ref[...], preferred_element_type=F32)
    a = jnp.square(jnp.maximum(a, 0.0)).astype(BF16)
    o_ref[...] += jnp.dot(a, wd_ref[...], preferred_element_type=F32)

    @pl.when(j == pl.num_programs(1) - 1)
    def _():
        y = _rms(o_ref[...], nnw_ref[...])
        if last:
            o_ref[...] = y
        else:
            rest[0][...] = y.astype(BF16)


def _out_mlp(merged, w_out, h, norm_w, w_up, w_down, layer, next_norm_w, last):
    m, d = h.shape
    f = w_up.shape[-1]
    tm, tf = _tile(m, 512), _tile(f, 1024)
    row = pl.BlockSpec((tm, d), lambda i, j: (i, 0))
    once = lambda shape, imap: pl.BlockSpec(shape, imap, pipeline_mode=pl.Buffered(1))
    vec = pl.BlockSpec((1, d), lambda i, j: (0, 0))
    out_shape = [jax.ShapeDtypeStruct((m, d), F32)] + ([] if last else [jax.ShapeDtypeStruct((m, d), BF16)])
    res = pl.pallas_call(
        functools.partial(_mlp_kernel, last=last),
        grid=(m // tm, f // tf),
        in_specs=[once((tm, merged.shape[1]), lambda i, j: (i, 0)),
                  once((None,) + w_out.shape[1:], lambda i, j: (layer, 0, 0)),
                  once((tm, d), lambda i, j: (i, 0)), vec,
                  pl.BlockSpec((None, d, tf), lambda i, j: (layer, 0, j)),
                  pl.BlockSpec((None, tf, d), lambda i, j: (layer, j, 0)), vec],
        out_specs=[row] * len(out_shape),
        out_shape=out_shape,
        scratch_shapes=[pltpu.VMEM((tm, d), BF16)],
        compiler_params=_cparams(("parallel", "arbitrary"), VMEM_LIMIT_LARGE),
        name="out_mlp",
    )(merged, w_out, h, norm_w.reshape(1, d), w_up, w_down, next_norm_w.reshape(1, d))
    return res[0] if last else res


def _chunk_head(qs, kk, vv, b, st_ref, head, factored, kb_ref):
    c, kdim = qs.shape
    row = lax.broadcasted_iota(jnp.int32, (c, c), 0)
    col = lax.broadcasted_iota(jnp.int32, (c, c), 1)
    b_last = b[c - 1:c, :]
    st = st_ref[head]
    nt = (((1,), (1,)), ((), ()))
    if factored:
        ref_row = b[c // 2 - 1:c // 2, :]
        qd = qs * jnp.exp(b - ref_row)
        kd = kk * jnp.exp(ref_row - b)
        q_in = qd * jnp.exp(ref_row)
        k_dec = kd * jnp.exp(b_last - ref_row)
        att = lax.dot_general(qd.astype(BF16), kd.astype(BF16), nt, preferred_element_type=F32)
    else:
        q_in = qs * jnp.exp(b)
        k_dec = kk * jnp.exp(b_last - b)
        kb_ref[0, :, 0:kdim] = kk
        kb_ref[1, :, 0:kdim] = b

        def body(s, att):
            k_row = kb_ref[0, pl.ds(s, 1), 0:kdim]
            b_row = kb_ref[1, pl.ds(s, 1), 0:kdim]
            w = jnp.sum(qs * k_row * jnp.exp(jnp.minimum(b - b_row, 0.0)), axis=-1, keepdims=True)
            return jnp.where(col == s, w, att)

        att = lax.fori_loop(0, c, body, jnp.zeros((c, c), F32))
    att = jnp.where(col <= row, att, 0.0)
    o = lax.dot_general(q_in.astype(BF16), st.astype(BF16), nt, preferred_element_type=F32)
    o = o + jnp.dot(att.astype(BF16), vv.astype(BF16), preferred_element_type=F32)
    upd = lax.dot_general(vv.astype(BF16), k_dec.astype(BF16), (((0,), (0,)), ((), ())),
                          preferred_element_type=F32)
    st_ref[head] = st * jnp.exp(b_last) + upd
    return o


def _cumsum_chunks(x, nb, c):
    row = lax.broadcasted_iota(jnp.int32, (c, c), 0)
    col = lax.broadcasted_iota(jnp.int32, (c, c), 1)
    tri = jnp.where(col <= row, 1.0, 0.0).astype(F32)
    return [jnp.dot(tri, x[bi * c:(bi + 1) * c, :], preferred_element_type=F32, precision=HIGHEST)
            for bi in range(nb)]


def _chunk_is_safe(b_all):
    c = b_all[0].shape[0]
    worst = functools.reduce(jnp.minimum, [b[c - 1:c, :] for b in b_all])
    return jnp.max(-worst) <= SAFE_CHUNK_DECAY


def _log_sigmoid(x):
    return jnp.minimum(x, 0.0) - jnp.log(1.0 + jnp.exp(-jnp.abs(x)))


def _gla_kernel(q_ref, k_ref, v_ref, g_ref, a_ref, w2_ref, b2_ref, nw_ref, o_ref, st_ref, kb_ref, *,
                heads, dk, dv):
    nb, c = q_ref.shape[0], q_ref.shape[1]

    @pl.when(pl.program_id(0) == 0)
    def _():
        st_ref[...] = jnp.zeros_like(st_ref)

    a = a_ref[...].reshape(nb * c, a_ref.shape[2])
    z = jnp.dot(a, w2_ref[...], preferred_element_type=F32, precision=HIGHEST) + b2_ref[...]
    b_all = _cumsum_chunks(_log_sigmoid(z) * (1.0 / GLA_GATE_NORMALIZER), nb, c)

    def step(factored):
        for bi in range(nb):
            for h in range(heads):
                qs = q_ref[bi, :, h * dk:(h + 1) * dk].astype(F32) * (dk ** -0.5)
                kk = k_ref[bi, :, h * dk:(h + 1) * dk].astype(F32)
                vv = v_ref[bi, :, h * dv:(h + 1) * dv]
                o = _chunk_head(qs, kk, vv, b_all[bi][:, h * dk:(h + 1) * dk], st_ref, bi * heads + h, factored,
                                kb_ref)
                g = g_ref[bi, :, h * dv:(h + 1) * dv].astype(F32)
                o = _rms(o, nw_ref[...]) * (g * _sigmoid(g))
                o_ref[bi, :, h * dv:(h + 1) * dv] = o.astype(o_ref.dtype)

    lax.cond(_chunk_is_safe(b_all), functools.partial(step, True), functools.partial(step, False))


def _chunk_specs(nb, c, arr, offs):
    def seg(name, width):
        assert offs[name] % width == 0
        blk = offs[name] // width
        return pl.BlockSpec((nb, c, width), lambda ci: (0, ci, blk))

    return seg, arr.reshape(nb, -1, arr.shape[-1])


def _gla(p_gla, a_low, w2p, b2, onorm_w, dm, offs):
    nb, c = dm.batch, dm.chunk
    hk, hv = dm.gla_heads * dm.gla_dk, dm.gla_heads * dm.gla_dv
    seg, p3 = _chunk_specs(nb, c, p_gla, offs)
    a3 = a_low.reshape(nb, dm.seq, LANES)
    const = lambda shape: pl.BlockSpec(shape, lambda ci: (0, 0))
    out = pl.pallas_call(
        functools.partial(_gla_kernel, heads=dm.gla_heads, dk=dm.gla_dk, dv=dm.gla_dv),
        grid=(dm.seq // c,),
        in_specs=[seg("gla_q", hk), seg("gla_k", hk), seg("gla_v", hv), seg("gla_g", hv),
                  pl.BlockSpec((nb, c, LANES), lambda ci: (0, ci, 0)),
                  const((LANES, hk)), const((1, hk)), const((1, dm.gla_dv))],
        out_specs=pl.BlockSpec((nb, c, hv), lambda ci: (0, ci, 0)),
        out_shape=jax.ShapeDtypeStruct((nb, dm.seq, hv), BF16),
        scratch_shapes=[pltpu.VMEM((nb * dm.gla_heads, dm.gla_dv, dm.gla_dk), F32),
                        pltpu.VMEM((2, c, dm.gla_dk), F32)],
        compiler_params=_cparams(("arbitrary",)),
        name="gla",
    )(p3, p3, p3, p3, a3, w2p, b2.reshape(1, hk), onorm_w.reshape(1, dm.gla_dv))
    return out.reshape(nb * dm.seq, hv)


def _hgrn_kernel(q_ref, f_ref, i_ref, g_ref, lbp_ref, nw_ref, o_ref, st_ref, kb_ref, *, layer, heads, dk, dv):
    nb, c = q_ref.shape[0], q_ref.shape[1]

    @pl.when(pl.program_id(0) == 0)
    def _():
        st_ref[...] = jnp.zeros_like(st_ref)

    lbp = lbp_ref[...]
    e = jnp.exp(lbp - jnp.max(lbp, axis=0, keepdims=True))
    p = e / jnp.sum(e, axis=0, keepdims=True)
    lb = jnp.zeros_like(p[0:1, :])
    for j in range(1, layer + 1):
        lb = lb + p[j:j + 1, :]
    sig_f = _sigmoid(f_ref[...].reshape(nb * c, f_ref.shape[2]).astype(F32))
    f_gate = lb + (1.0 - lb) * sig_f
    b_all = _cumsum_chunks(jnp.log(jnp.maximum(f_gate, F_MIN)), nb, c)
    k_all = (1.0 - lb) * (1.0 - sig_f)

    def step(factored):
        for bi in range(nb):
            for h in range(heads):
                q = q_ref[bi, :, h * dk:(h + 1) * dk].astype(F32)
                qs = q * _sigmoid(q) * (dk ** -0.5)
                vv = i_ref[bi, :, h * dv:(h + 1) * dv]
                o = _chunk_head(qs, k_all[bi * c:(bi + 1) * c, h * dk:(h + 1) * dk], vv,
                                b_all[bi][:, h * dk:(h + 1) * dk], st_ref, bi * heads + h, factored, kb_ref)
                g = g_ref[bi, :, h * dv:(h + 1) * dv].astype(F32)
                o = _rms(o, nw_ref[...]) * _sigmoid(g)
                o_ref[bi, :, h * dv:(h + 1) * dv] = o.astype(o_ref.dtype)

    lax.cond(_chunk_is_safe(b_all), functools.partial(step, True), functools.partial(step, False))


def _hgrn(p_hg, lower_bounds, onorm_w, layer, dm, offs):
    nb, c = dm.batch, dm.chunk
    hk, hv = dm.hg_heads * dm.hg_dk, dm.hg_heads * dm.hg_dv
    seg, p3 = _chunk_specs(nb, c, p_hg, offs)
    const = lambda shape: pl.BlockSpec(shape, lambda ci: (0, 0))
    out = pl.pallas_call(
        functools.partial(_hgrn_kernel, layer=layer, heads=dm.hg_heads, dk=dm.hg_dk, dv=dm.hg_dv),
        grid=(dm.seq // c,),
        in_specs=[seg("hg_q", hk), seg("hg_f", hk), seg("hg_i", hv), seg("hg_g", hv),
                  const((dm.depth, hk)), const((1, dm.hg_dv))],
        out_specs=pl.BlockSpec((nb, c, hv), lambda ci: (0, ci, 0)),
        out_shape=jax.ShapeDtypeStruct((nb, dm.seq, hv), BF16),
        scratch_shapes=[pltpu.VMEM((nb * dm.hg_heads, dm.hg_dv, dm.hg_dk), F32),
                        pltpu.VMEM((2, c, dm.hg_dk), F32)],
        compiler_params=_cparams(("arbitrary",)),
        name="hgrn",
    )(p3, p3, p3, p3, lower_bounds, onorm_w.reshape(1, dm.hg_dv))
    return out.reshape(nb * dm.seq, hv)


def _rope_table_kernel(pos_ref, f_ref, sgn_ref, cos_ref, sin_ref):
    ang = pos_ref[...].astype(F32) * f_ref[...]
    cos_ref[...] = jnp.cos(ang)
    sin_ref[...] = jnp.sin(ang) * sgn_ref[...]


def _rope_tables(pos_col, inv_freq_lanes, sign_lanes):
    m = pos_col.shape[0]
    tm = _tile(m, 512)
    lane = pl.BlockSpec((1, LANES), lambda i: (0, 0))
    tab = pl.BlockSpec((tm, LANES), lambda i: (i, 0))
    return pl.pallas_call(
        _rope_table_kernel,
        grid=(m // tm,),
        in_specs=[pl.BlockSpec((tm, 1), lambda i: (i, 0)), lane, lane],
        out_specs=[tab, tab],
        out_shape=[jax.ShapeDtypeStruct((m, LANES), F32)] * 2,
        compiler_params=_cparams(("parallel",)),
        name="rope_tables",
    )(pos_col, inv_freq_lanes, sign_lanes)


def _swap_halves(x, half):
    n = x.shape[-1]
    if 2 * half == n:
        return pltpu.roll(x, half, axis=1)
    lane = lax.broadcasted_iota(jnp.int32, x.shape, 1)
    return jnp.where((lane & half) == 0, pltpu.roll(x, n - half, axis=1), pltpu.roll(x, half, axis=1))


def _rope_apply_kernel(q_ref, k_ref, s_ref, c1_ref, s1_ref, c2_ref, s2_ref, qo_ref, ko_ref, iqo_ref,
                       iko_ref, iwo_ref, *, heads, dim, idx_heads, idx_dim):
    c1, s1, c2, s2 = c1_ref[...], s1_ref[...], c2_ref[...], s2_ref[...]
    q_scale = dim ** -0.5
    for h in range(heads):
        x = q_ref[:, h * dim:(h + 1) * dim].astype(F32)
        qo_ref[:, h * dim:(h + 1) * dim] = ((x * c1 + _swap_halves(x, dim // 2) * s1) * q_scale).astype(BF16)
    x = k_ref[...].astype(F32)
    ko_ref[...] = (x * c1 + _swap_halves(x, dim // 2) * s1).astype(BF16)
    iq_scale = idx_dim ** -0.5
    for g in range(idx_heads * idx_dim // LANES):
        x = s_ref[:, g * LANES:(g + 1) * LANES]
        r = (x * c2 + _swap_halves(x, idx_dim // 2) * s2) * iq_scale
        iqo_ref[:, g * LANES:(g + 1) * LANES] = r.astype(BF16)
    off = idx_heads * idx_dim
    x = s_ref[:, off:off + LANES]
    r = x * c2 + _swap_halves(x, idx_dim // 2) * s2
    lane = lax.broadcasted_iota(jnp.int32, x.shape, 1)
    iko_ref[...] = jnp.where(lane < idx_dim, r, pltpu.roll(r, idx_dim, axis=1)).astype(BF16)
    iwo_ref[...] = jnp.where(lane < idx_heads, pltpu.roll(x, LANES - idx_dim, axis=1), 0.0) * (idx_heads ** -0.5)


def _rope_apply(p_dsa, idx, tabs, dm, offs):
    m = p_dsa.shape[0]
    tm = _tile(m, 256)
    hd = dm.dsa_heads * dm.dsa_dim
    iq = dm.idx_heads * dm.idx_dim
    assert dm.dsa_dim == LANES and 2 * dm.idx_dim == LANES and dm.idx_heads <= dm.idx_dim
    assert offs["dsa_q"] % hd == 0 and offs["dsa_k"] % dm.dsa_dim == 0
    rows = lambda w, blk=0: pl.BlockSpec((tm, w), lambda i: (i, blk))
    return pl.pallas_call(
        functools.partial(_rope_apply_kernel, heads=dm.dsa_heads, dim=dm.dsa_dim, idx_heads=dm.idx_heads,
                          idx_dim=dm.idx_dim),
        grid=(m // tm,),
        in_specs=[rows(hd, offs["dsa_q"] // hd), rows(dm.dsa_dim, offs["dsa_k"] // dm.dsa_dim),
                  rows(idx.shape[1]), rows(LANES), rows(LANES), rows(LANES), rows(LANES)],
        out_specs=[rows(hd), rows(dm.dsa_dim), rows(iq), rows(LANES), rows(LANES)],
        out_shape=[jax.ShapeDtypeStruct((m, hd), BF16), jax.ShapeDtypeStruct((m, dm.dsa_dim), BF16),
                   jax.ShapeDtypeStruct((m, iq), BF16), jax.ShapeDtypeStruct((m, LANES), BF16),
                   jax.ShapeDtypeStruct((m, LANES), F32)],
        compiler_params=_cparams(("parallel",)),
        name="rope_apply",
    )(p_dsa, p_dsa, idx, *tabs)


def _ind(mask):
    return jnp.where(mask, 1.0, 0.0)


def _row_sum(x):
    return jnp.sum(x, axis=-1, keepdims=True)


def _float_of_ordered(u):
    k = u ^ jnp.int32(INT_MIN)
    return pltpu.bitcast(jnp.where(k < 0, k ^ jnp.int32(0x7FFFFFFF), k), F32)


def _dsa_kernel(q_ref, iq_ref, iw_ref, k_ref, v_ref, ik_ref, o_ref, *, heads, dim, idx_heads, idx_dim, topk,
                q_start, n_hidden):
    nbb, tq = q_ref.shape[0], q_ref.shape[1]
    tk = k_ref.shape[1]
    rows = nbb * tq
    nt = (((1,), (1,)), ((), ()))
    stack = lambda parts: parts[0] if len(parts) == 1 else jnp.concatenate(parts, axis=0)
    lane = lax.broadcasted_iota(jnp.int32, (tq, LANES), 1)
    scores = []
    for bi in range(nbb):
        ik = ik_ref[bi]
        score = jnp.zeros((tq, tk), F32)
        for h in range(idx_heads):
            g = (h * idx_dim) // LANES
            lo = (h * idx_dim) % LANES
            x = iq_ref[bi, :, g * LANES:(g + 1) * LANES]
            x = jnp.where(lane >= lo, jnp.where(lane < lo + idx_dim, x, jnp.zeros_like(x)), jnp.zeros_like(x))
            rel = lax.dot_general(x, ik, nt, preferred_element_type=F32)
            score = score + iw_ref[bi, :, h:h + 1] * jnp.maximum(rel, 0.0)
        scores.append(score)
    qpos = stack([q_start + lax.broadcasted_iota(jnp.int32, (tq, 1), 0)] * nbb)
    kpos = lax.broadcasted_iota(jnp.int32, (1, tk), 1)
    allowed = kpos <= qpos
    score = jnp.where(allowed, stack(scores), MASK_VALUE)
    hidden = float(n_hidden)

    def thr_body(it, prefix):
        trial = prefix | lax.shift_left(jnp.int32(1), 31 - it)
        cand = _float_of_ordered(trial)
        cnt = _row_sum(_ind(score >= cand)) + jnp.where(MASK_VALUE >= cand, hidden, 0.0)
        return jnp.where(cnt >= topk, trial, prefix)

    thr = _float_of_ordered(lax.fori_loop(0, 32, thr_body, jnp.zeros((rows, 1), jnp.int32)))
    above = _ind(score > thr)
    need = topk - _row_sum(above) - jnp.where(MASK_VALUE > thr, hidden, 0.0)
    tie = jnp.where(allowed, _ind(score == thr), 0.0)

    def all_ties():
        return jnp.where(allowed, _ind(score >= thr), 0.0)

    def ordered_ties():
        nbits = (tk - 1).bit_length()

        def tie_body(it, j):
            trial = j | lax.shift_left(jnp.int32(1), nbits - 1 - it)
            cnt = _row_sum(jnp.where(kpos < trial, tie, 0.0))
            return jnp.where(cnt < need, trial, j)

        j_last = lax.fori_loop(0, nbits, tie_body, jnp.zeros((rows, 1), jnp.int32))
        return jnp.where(allowed, above, 0.0) + jnp.where(kpos <= j_last, tie, 0.0)

    selected = lax.cond(jnp.max(_row_sum(tie) - need) > 0.0, ordered_ties, all_ties)

    for bi in range(nbb):
        valid = selected[bi * tq:(bi + 1) * tq, :] > 0.0
        k = k_ref[bi]
        v_ones = jnp.concatenate([v_ref[bi], jnp.ones((tk, dim), BF16)], axis=1)
        for h in range(heads):
            s = lax.dot_general(q_ref[bi, :, h * dim:(h + 1) * dim], k, nt, preferred_element_type=F32)
            s = jnp.where(valid, s, MASK_VALUE)
            p = jnp.exp(s - jnp.max(s, axis=-1, keepdims=True)).astype(BF16)
            o = jnp.dot(p, v_ones, preferred_element_type=F32)
            o_ref[bi, :, h * dim:(h + 1) * dim] = (o[:, 0:dim] / o[:, dim:dim + 1]).astype(o_ref.dtype)


def _with_carried_output(kern, q_ref, iq_ref, iw_ref, k_ref, v_ref, ik_ref, carried_ref, o_ref):
    del carried_ref
    kern(q_ref, iq_ref, iw_ref, k_ref, v_ref, ik_ref, o_ref)


def _dsa(q_rot, k_rot, p_dsa, iq_rot, ik_rot, iw, dm, offs):
    tq, t, nb = dm.q_block, dm.seq, dm.batch
    hd = dm.dsa_heads * dm.dsa_dim
    assert offs["dsa_v"] % dm.dsa_dim == 0 and t % tq == 0
    v_blk = offs["dsa_v"] // dm.dsa_dim
    per_batch = lambda a: a.reshape(nb, t, a.shape[-1])
    q3, k3, p3, iq3, ik3, iw3 = map(per_batch, (q_rot, k_rot, p_dsa, iq_rot, ik_rot, iw))
    out = None
    for g in range(t // tq):
        tk = (g + 1) * tq
        nbb = 2 if (2 * tk <= t and nb % 2 == 0) else 1
        qrow = lambda w, g=g, nbb=nbb: pl.BlockSpec((nbb, tq, w), lambda bi: (bi, g, 0))
        krow = lambda w, blk=0, tk=tk, nbb=nbb: pl.BlockSpec((nbb, tk, w), lambda bi: (bi, 0, blk))
        kern = functools.partial(_dsa_kernel, heads=dm.dsa_heads, dim=dm.dsa_dim, idx_heads=dm.idx_heads,
                                 idx_dim=dm.idx_dim, topk=dm.topk, q_start=g * tq, n_hidden=t - tk)
        args = [q3, iq3, iw3, k3, p3, ik3]
        in_specs = [qrow(hd), qrow(iq3.shape[-1]), qrow(LANES), krow(dm.dsa_dim), krow(dm.dsa_dim, v_blk),
                    krow(LANES)]
        if out is not None:
            kern = functools.partial(_with_carried_output, kern)
            args.append(out)
            in_specs.append(pl.BlockSpec(memory_space=pl.ANY))
        out = pl.pallas_call(
            kern,
            grid=(nb // nbb,),
            in_specs=in_specs,
            out_specs=qrow(hd),
            out_shape=jax.ShapeDtypeStruct((nb, t, hd), BF16),
            input_output_aliases={} if len(args) == 6 else {6: 0},
            compiler_params=_cparams(("parallel",)),
            name=f"dsa_q{g}",
        )(*args)
    return out.reshape(nb * t, hd)


def _forward(dm, x, positions, norm1_w, w_in, gla_gate_w2, gla_gate_b, gla_onorm_w, hgrn_lower_bounds,
             hgrn_onorm_w, w_branch_gla, w_branch_dsa, w_branch_hgrn, w_out, norm2_w, w_mlp_up, w_mlp_down,
             final_norm_w):
    m = dm.batch * dm.seq
    h = x.reshape(m, dm.d_model)
    src, ((gla_start, gla_width, gla_offs), (dsa_start, dsa_width, dsa_offs), (hg_start, hg_width, hg_offs)) = \
        _layout(dm)
    idx_start = src["idx_q"][0]
    idx_width = -(-(src["idx_w"][0] + src["idx_w"][1] - idx_start) // LANES) * LANES
    w_t = jnp.swapaxes(w_in, 1, 2)

    def lanes(d):
        inv = ROPE_THETA ** (-jnp.arange(0, d, 2, dtype=F32) / d)
        reps = LANES // d
        f = jnp.tile(jnp.concatenate([inv, inv]), reps).reshape(1, LANES)
        sgn = jnp.tile(jnp.concatenate([-jnp.ones(d // 2, F32), jnp.ones(d // 2, F32)]), reps).reshape(1, LANES)
        return f, sgn

    pos_col = positions.reshape(m, 1)
    tabs = _rope_tables(pos_col, *lanes(dm.dsa_dim)) + _rope_tables(pos_col, *lanes(dm.idx_dim))

    w_up_bf16, w_down_bf16, w_out_bf16 = w_mlp_up.astype(BF16), w_mlp_down.astype(BF16), w_out.astype(BF16)
    u = _rmsnorm(h, norm1_w[0], BF16)
    for l in range(dm.depth):
        w2p = jnp.concatenate(
            [gla_gate_w2[l], jnp.zeros((LANES - dm.gla_rank, gla_gate_w2.shape[2]), F32)], axis=0)
        p_gla = _proj(u, w_t, l, gla_start, gla_width, BF16, name="in_proj_gla")
        p_hg = _proj(u, w_t, l, hg_start, hg_width, BF16, name="in_proj_hg")
        p_dsa = _proj(u, w_t, l, dsa_start, dsa_width, BF16, tm_pref=1024, tn_pref=768, name="in_proj_dsa")
        a_low = _proj(u, w_t, l, src["gla_a"][0], LANES, F32, name="in_proj_gate")
        idx = _proj(u, w_t, l, idx_start, idx_width, F32, tm_pref=1024, tn_pref=idx_width, name="in_proj_idx")
        o_gla = _gla(p_gla, a_low, w2p, gla_gate_b[l], gla_onorm_w[l], dm, gla_offs)
        o_hg = _hgrn(p_hg, hgrn_lower_bounds, hgrn_onorm_w[l], l, dm, hg_offs)
        q_rot, k_rot, iq_rot, ik_rot, iw = _rope_apply(p_dsa, idx, tabs, dm, dsa_offs)
        o_dsa = _dsa(q_rot, k_rot, p_dsa, iq_rot, ik_rot, iw, dm, dsa_offs)
        merged = _merge(o_gla, o_dsa, o_hg, w_branch_gla, w_branch_dsa, w_branch_hgrn, l, p_hg,
                        (hg_offs["gate_a"], hg_offs["gate_b"], hg_offs["gate_c"]), dm.d_model)
        last = l == dm.depth - 1
        res = _out_mlp(merged, w_out_bf16, h, norm2_w[l], w_up_bf16, w_down_bf16, l,
                       final_norm_w if last else norm1_w[l + 1], last)
        if not last:
            h, u = res
    return res.reshape(dm.batch, dm.seq, dm.d_model)


def kernel(x, positions, norm1_w, w_in, gla_gate_w2, gla_gate_b, gla_onorm_w, hgrn_lower_bounds, hgrn_onorm_w,
           w_branch_gla, w_branch_dsa, w_branch_hgrn, w_out, norm2_w, w_mlp_up, w_mlp_down, final_norm_w):
    return _forward(_prod_dims(), x, positions, norm1_w, w_in, gla_gate_w2, gla_gate_b, gla_onorm_w,
                    hgrn_lower_bounds, hgrn_onorm_w, w_branch_gla, w_branch_dsa, w_branch_hgrn, w_out, norm2_w,
                    w_mlp_up, w_mlp_down, final_norm_w)
```

```python
import functools
from typing import NamedTuple

import jax
import jax.numpy as jnp
from jax import lax
from jax.experimental import pallas as pl
from jax.experimental.pallas import tpu as pltpu

F32 = jnp.float32
BF16 = jnp.bfloat16
HIGHEST = lax.Precision.HIGHEST

ROPE_THETA = 10000.0
NORM_EPS = 1e-6
MASK_VALUE = -1e30
F_MIN = 1e-12
GLA_GATE_NORMALIZER = 16.0
LANES = 128
SUBLANES = 8
ROW_BLOCK = 128
INT_MIN = -2 ** 31
SAFE_HALF_DECAY = 80.0
VMEM_LIMIT = 48 * 1024 * 1024
VMEM_LIMIT_LARGE = 56 * 1024 * 1024


class _Dims(NamedTuple):
    d_model: int
    batch: int
    seq: int
    depth: int
    gla_heads: int
    gla_dk: int
    gla_dv: int
    gla_rank: int
    dsa_heads: int
    dsa_dim: int
    idx_heads: int
    idx_dim: int
    topk: int
    hg_heads: int
    hg_dk: int
    hg_dv: int
    d_ff: int
    chunk: int
    q_block: int


def _prod_dims():
    d = 2048
    return _Dims(d_model=d, batch=4, seq=2048, depth=2,
                 gla_heads=4, gla_dk=d // 2 // 4, gla_dv=d // 4, gla_rank=16,
                 dsa_heads=16, dsa_dim=128, idx_heads=8, idx_dim=64, topk=min(256, 2048 // 4),
                 hg_heads=d // 128, hg_dk=128, hg_dv=128, d_ff=4 * d, chunk=128, q_block=256)


def _in_sizes(dm):
    return (dm.gla_heads * dm.gla_dk, dm.gla_heads * dm.gla_dk, dm.gla_heads * dm.gla_dv,
            dm.gla_heads * dm.gla_dv, dm.gla_rank,
            dm.dsa_heads * dm.dsa_dim, dm.dsa_dim, dm.dsa_dim, dm.idx_heads * dm.idx_dim, dm.idx_dim,
            dm.idx_heads,
            dm.hg_heads * dm.hg_dk, dm.hg_heads * dm.hg_dk, dm.hg_heads * dm.hg_dv, dm.hg_heads * dm.hg_dv,
            dm.d_model, dm.d_model, dm.d_model)


_IN_NAMES = ("gla_q", "gla_k", "gla_v", "gla_g", "gla_a", "dsa_q", "dsa_k", "dsa_v", "idx_q", "idx_k",
             "idx_w", "hg_q", "hg_f", "hg_i", "hg_g", "gate_a", "gate_b", "gate_c")
_GROUPS = (("gla_q", "gla_k", "gla_v", "gla_g"), ("dsa_q", "dsa_k", "dsa_v"),
           ("hg_q", "hg_f", "hg_i", "hg_g", "gate_a", "gate_b", "gate_c"))


def _tile(n, pref):
    t = min(n, pref)
    while n % t:
        t //= 2
    return t


def _layout(dm):
    sizes = dict(zip(_IN_NAMES, _in_sizes(dm)))
    src, off = {}, 0
    for name in _IN_NAMES:
        src[name] = (off, sizes[name])
        off += sizes[name]
    groups = []
    for names in _GROUPS:
        start = src[names[0]][0]
        offs = {n: src[n][0] - start for n in names}
        width = src[names[-1]][0] + src[names[-1]][1] - start
        for n in names:
            assert offs[n] % min(sizes[n], 2 * LANES) == 0, n
        groups.append((start, width, offs))
    return src, groups


def _cparams(sem, vmem=VMEM_LIMIT):
    return pltpu.CompilerParams(dimension_semantics=sem, vmem_limit_bytes=vmem)


def _rms(x, w):
    ms = jnp.mean(x * x, axis=-1, keepdims=True)
    return x * lax.rsqrt(ms + NORM_EPS) * w


def _rmsnorm_kernel(x_ref, w_ref, o_ref):
    o_ref[...] = _rms(x_ref[...], w_ref[...]).astype(o_ref.dtype)


def _rmsnorm(x, w, out_dtype):
    m, d = x.shape
    tm = _tile(m, 512)
    return pl.pallas_call(
        _rmsnorm_kernel,
        grid=(m // tm,),
        in_specs=[pl.BlockSpec((tm, d), lambda i: (i, 0)), pl.BlockSpec((1, d), lambda i: (0, 0))],
        out_specs=pl.BlockSpec((tm, d), lambda i: (i, 0)),
        out_shape=jax.ShapeDtypeStruct((m, d), out_dtype),
        compiler_params=_cparams(("parallel",)),
        name="rmsnorm",
    )(x, w.reshape(1, d))


def _w_spec(w, layer, k, tn):
    if w.ndim == 3:
        return pl.BlockSpec((None, k, tn), lambda i, j: (layer, 0, j))
    return pl.BlockSpec((k, tn), lambda i, j: (0, j))


def _proj_kernel(x_ref, *refs, shift):
    *w_refs, o_ref = refs
    tn = o_ref.shape[1]
    if len(w_refs) == 1:
        w = w_refs[0][...]
    else:
        w = jnp.concatenate([r[...] for r in w_refs], axis=0)[shift:shift + tn, :]
    y = lax.dot_general(x_ref[...], w.astype(BF16), (((1,), (1,)), ((), ())), preferred_element_type=F32)
    o_ref[...] = y.astype(o_ref.dtype)


def _proj(x, w_t, layer, start, n, out_dtype, tm_pref=2048, tn_pref=512, name="proj"):
    m, k = x.shape
    tm, tn = _tile(m, tm_pref), _tile(n, tn_pref)
    if start % tn == 0:
        shift = 0
        specs = [pl.BlockSpec((None, tn, k), lambda i, j: (layer, start // tn + j, 0))]
    else:
        assert tn % ROW_BLOCK == 0 and start % SUBLANES == 0
        base, shift = divmod(start, ROW_BLOCK)
        per_tile = tn // ROW_BLOCK
        blk = lambda b: pl.BlockSpec((None, ROW_BLOCK, k), lambda i, j: (layer, base + j * per_tile + b, 0))
        specs = [blk(b) for b in range(per_tile + 1)]
    return pl.pallas_call(
        functools.partial(_proj_kernel, shift=shift),
        grid=(m // tm, n // tn),
        in_specs=[pl.BlockSpec((tm, k), lambda i, j: (i, 0))] + specs,
        out_specs=pl.BlockSpec((tm, tn), lambda i, j: (i, j)),
        out_shape=jax.ShapeDtypeStruct((m, n), out_dtype),
        compiler_params=_cparams(("parallel", "arbitrary")),
        name=name,
    )(x, *([w_t] * len(specs)))


def _sigmoid(x):
    return 1.0 / (1.0 + jnp.exp(-x))


def _merge_kernel(oa_ref, ob_ref, oc_ref, wa_ref, wb_ref, wc_ref, ga_ref, gb_ref, gc_ref, o_ref):
    def branch(o_ref_, w_ref_, g_ref_):
        y = jnp.dot(o_ref_[...], w_ref_[...].astype(BF16), preferred_element_type=F32)
        return _sigmoid(g_ref_[...].astype(F32)) * y

    acc = branch(oa_ref, wa_ref, ga_ref) + branch(ob_ref, wb_ref, gb_ref) + branch(oc_ref, wc_ref, gc_ref)
    o_ref[...] = acc.astype(o_ref.dtype)


def _merge(o_gla, o_dsa, o_hg, w_gla, w_dsa, w_hg, layer, gates, gate_offs, d_model):
    m = o_gla.shape[0]
    tm, tn = _tile(m, 1024), _tile(d_model, 256)
    o_spec = lambda a: pl.BlockSpec((tm, a.shape[1]), lambda i, j: (i, 0))

    def g_spec(off):
        assert off % tn == 0
        return pl.BlockSpec((tm, tn), lambda i, j: (i, off // tn + j))

    return pl.pallas_call(
        _merge_kernel,
        grid=(m // tm, d_model // tn),
        in_specs=[o_spec(o_gla), o_spec(o_dsa), o_spec(o_hg),
                  _w_spec(w_gla, layer, w_gla.shape[-2], tn), _w_spec(w_dsa, layer, w_dsa.shape[-2], tn),
                  _w_spec(w_hg, layer, w_hg.shape[-2], tn),
                  g_spec(gate_offs[0]), g_spec(gate_offs[1]), g_spec(gate_offs[2])],
        out_specs=pl.BlockSpec((tm, tn), lambda i, j: (i, j)),
        out_shape=jax.ShapeDtypeStruct((m, d_model), BF16),
        compiler_params=_cparams(("parallel", "arbitrary"), VMEM_LIMIT_LARGE),
        name="merge",
    )(o_gla, o_dsa, o_hg, w_gla, w_dsa, w_hg, gates, gates, gates)


def _matmul_residual_kernel(x_ref, w_ref, r_ref, o_ref):
    o_ref[...] = r_ref[...] + jnp.dot(x_ref[...], w_ref[...].astype(BF16), preferred_element_type=F32)


def _matmul_residual(x, w, layer, res, tm_pref=1024, tn_pref=512):
    m, k = x.shape
    n = w.shape[-1]
    tm, tn = _tile(m, tm_pref), _tile(n, tn_pref)
    return pl.pallas_call(
        _matmul_residual_kernel,
        grid=(m // tm, n // tn),
        in_specs=[pl.BlockSpec((tm, k), lambda i, j: (i, 0)), _w_spec(w, layer, k, tn),
                  pl.BlockSpec((tm, tn), lambda i, j: (i, j))],
        out_specs=pl.BlockSpec((tm, tn), lambda i, j: (i, j)),
        out_shape=jax.ShapeDtypeStruct((m, n), F32),
        compiler_params=_cparams(("parallel", "arbitrary")),
        name="out_proj",
    )(x, w, res)


def _mlp_kernel(h_ref, nw_ref, wu_ref, wd_ref, nnw_ref, o_ref, *rest, last):
    u_ref = rest[-1]
    j = pl.program_id(1)

    @pl.when(j == 0)
    def _():
        x = h_ref[...]
        u_ref[...] = _rms(x, nw_ref[...]).astype(BF16)
        o_ref[...] = x

    a = jnp.dot(u_ref[...], wu_ref[...], preferred_element_type=F32)
    a = jnp.square(jnp.maximum(a, 0.0)).astype(BF16)
    o_ref[...] += jnp.dot(a, wd_ref[...], preferred_element_type=F32)

    @pl.when(j == pl.num_programs(1) - 1)
    def _():
        y = _rms(o_ref[...], nnw_ref[...])
        if last:
            o_ref[...] = y
        else:
            rest[0][...] = y.astype(BF16)


def _mlp(h, norm_w, w_up, w_down, layer, next_norm_w, last):
    m, d = h.shape
    f = w_up.shape[-1]
    tm, tf = _tile(m, 512), _tile(f, 1024)
    row = pl.BlockSpec((tm, d), lambda i, j: (i, 0))
    vec = pl.BlockSpec((1, d), lambda i, j: (0, 0))
    out_shape = [jax.ShapeDtypeStruct((m, d), F32)] + ([] if last else [jax.ShapeDtypeStruct((m, d), BF16)])
    res = pl.pallas_call(
        functools.partial(_mlp_kernel, last=last),
        grid=(m // tm, f // tf),
        in_specs=[row, vec, pl.BlockSpec((None, d, tf), lambda i, j: (layer, 0, j)),
                  pl.BlockSpec((None, tf, d), lambda i, j: (layer, j, 0)), vec],
        out_specs=[row] * len(out_shape),
        out_shape=out_shape,
        scratch_shapes=[pltpu.VMEM((tm, d), BF16)],
        compiler_params=_cparams(("parallel", "arbitrary"), VMEM_LIMIT_LARGE),
        name="mlp",
    )(h, norm_w.reshape(1, d), w_up, w_down, next_norm_w.reshape(1, d))
    return res[0] if last else res


def _chunk_head(qs, kk, vv, b, st_ref, head, factored, kb_ref):
    c, kdim = qs.shape
    row = lax.broadcasted_iota(jnp.int32, (c, c), 0)
    col = lax.broadcasted_iota(jnp.int32, (c, c), 1)
    b_last = b[c - 1:c, :]
    st = st_ref[head]
    nt = (((1,), (1,)), ((), ()))
    if factored:
        ref_row = b[c // 2 - 1:c // 2, :]
        qd = qs * jnp.exp(b - ref_row)
        kd = kk * jnp.exp(ref_row - b)
        q_in = qd * jnp.exp(ref_row)
        k_dec = kd * jnp.exp(b_last - ref_row)
        att = lax.dot_general(qd.astype(BF16), kd.astype(BF16), nt, preferred_element_type=F32)
    else:
        q_in = qs * jnp.exp(b)
        k_dec = kk * jnp.exp(b_last - b)
        kb_ref[0, :, 0:kdim] = kk
        kb_ref[1, :, 0:kdim] = b

        def body(s, att):
            k_row = kb_ref[0, pl.ds(s, 1), 0:kdim]
            b_row = kb_ref[1, pl.ds(s, 1), 0:kdim]
            w = jnp.sum(qs * k_row * jnp.exp(jnp.minimum(b - b_row, 0.0)), axis=-1, keepdims=True)
            return jnp.where(col == s, w, att)

        att = lax.fori_loop(0, c, body, jnp.zeros((c, c), F32))
    att = jnp.where(col <= row, att, 0.0)
    o = lax.dot_general(q_in.astype(BF16), st.astype(BF16), nt, preferred_element_type=F32)
    o = o + jnp.dot(att.astype(BF16), vv.astype(BF16), preferred_element_type=F32)
    upd = lax.dot_general(vv.astype(BF16), k_dec.astype(BF16), (((0,), (0,)), ((), ())),
                          preferred_element_type=F32)
    st_ref[head] = st * jnp.exp(b_last) + upd
    return o


def _cumsum_chunks(x, nb, c):
    row = lax.broadcasted_iota(jnp.int32, (c, c), 0)
    col = lax.broadcasted_iota(jnp.int32, (c, c), 1)
    tri = jnp.where(col <= row, 1.0, 0.0).astype(F32)
    return [jnp.dot(tri, x[bi * c:(bi + 1) * c, :], preferred_element_type=F32, precision=HIGHEST)
            for bi in range(nb)]


def _chunk_is_safe(b_all):
    c = b_all[0].shape[0]
    worst = None
    for b in b_all:
        ref_row = b[c // 2 - 1:c // 2, :]
        span = jnp.maximum(-ref_row, ref_row - b[c - 1:c, :])
        worst = span if worst is None else jnp.maximum(worst, span)
    return jnp.max(worst) <= SAFE_HALF_DECAY


def _log_sigmoid(x):
    return jnp.minimum(x, 0.0) - jnp.log(1.0 + jnp.exp(-jnp.abs(x)))


def _gla_kernel(q_ref, k_ref, v_ref, g_ref, a_ref, w2_ref, b2_ref, nw_ref, o_ref, st_ref, kb_ref, *,
                heads, dk, dv):
    nb, c = q_ref.shape[0], q_ref.shape[1]

    @pl.when(pl.program_id(0) == 0)
    def _():
        st_ref[...] = jnp.zeros_like(st_ref)

    a = a_ref[...].reshape(nb * c, a_ref.shape[2])
    z = jnp.dot(a, w2_ref[...], preferred_element_type=F32, precision=HIGHEST) + b2_ref[...]
    b_all = _cumsum_chunks(_log_sigmoid(z) * (1.0 / GLA_GATE_NORMALIZER), nb, c)

    def step(factored):
        for bi in range(nb):
            for h in range(heads):
                qs = q_ref[bi, :, h * dk:(h + 1) * dk].astype(F32) * (dk ** -0.5)
                kk = k_ref[bi, :, h * dk:(h + 1) * dk].astype(F32)
                vv = v_ref[bi, :, h * dv:(h + 1) * dv]
                o = _chunk_head(qs, kk, vv, b_all[bi][:, h * dk:(h + 1) * dk], st_ref, bi * heads + h, factored,
                                kb_ref)
                g = g_ref[bi, :, h * dv:(h + 1) * dv].astype(F32)
                o = _rms(o, nw_ref[...]) * (g * _sigmoid(g))
                o_ref[bi, :, h * dv:(h + 1) * dv] = o.astype(o_ref.dtype)

    lax.cond(_chunk_is_safe(b_all), functools.partial(step, True), functools.partial(step, False))


def _chunk_specs(nb, c, arr, offs):
    def seg(name, width):
        assert offs[name] % width == 0
        blk = offs[name] // width
        return pl.BlockSpec((nb, c, width), lambda ci: (0, ci, blk))

    return seg, arr.reshape(nb, -1, arr.shape[-1])


def _gla(p_gla, a_low, w2p, b2, onorm_w, dm, offs):
    nb, c = dm.batch, dm.chunk
    hk, hv = dm.gla_heads * dm.gla_dk, dm.gla_heads * dm.gla_dv
    seg, p3 = _chunk_specs(nb, c, p_gla, offs)
    a3 = a_low.reshape(nb, dm.seq, LANES)
    const = lambda shape: pl.BlockSpec(shape, lambda ci: (0, 0))
    out = pl.pallas_call(
        functools.partial(_gla_kernel, heads=dm.gla_heads, dk=dm.gla_dk, dv=dm.gla_dv),
        grid=(dm.seq // c,),
        in_specs=[seg("gla_q", hk), seg("gla_k", hk), seg("gla_v", hv), seg("gla_g", hv),
                  pl.BlockSpec((nb, c, LANES), lambda ci: (0, ci, 0)),
                  const((LANES, hk)), const((1, hk)), const((1, dm.gla_dv))],
        out_specs=pl.BlockSpec((nb, c, hv), lambda ci: (0, ci, 0)),
        out_shape=jax.ShapeDtypeStruct((nb, dm.seq, hv), BF16),
        scratch_shapes=[pltpu.VMEM((nb * dm.gla_heads, dm.gla_dv, dm.gla_dk), F32),
                        pltpu.VMEM((2, c, dm.gla_dk), F32)],
        compiler_params=_cparams(("arbitrary",)),
        name="gla",
    )(p3, p3, p3, p3, a3, w2p, b2.reshape(1, hk), onorm_w.reshape(1, dm.gla_dv))
    return out.reshape(nb * dm.seq, hv)


def _hgrn_kernel(q_ref, f_ref, i_ref, g_ref, lbp_ref, nw_ref, o_ref, st_ref, kb_ref, *, layer, heads, dk, dv):
    nb, c = q_ref.shape[0], q_ref.shape[1]

    @pl.when(pl.program_id(0) == 0)
    def _():
        st_ref[...] = jnp.zeros_like(st_ref)

    lbp = lbp_ref[...]
    e = jnp.exp(lbp - jnp.max(lbp, axis=0, keepdims=True))
    p = e / jnp.sum(e, axis=0, keepdims=True)
    lb = jnp.zeros_like(p[0:1, :])
    for j in range(1, layer + 1):
        lb = lb + p[j:j + 1, :]
    sig_f = _sigmoid(f_ref[...].reshape(nb * c, f_ref.shape[2]).astype(F32))
    f_gate = lb + (1.0 - lb) * sig_f
    b_all = _cumsum_chunks(jnp.log(jnp.maximum(f_gate, F_MIN)), nb, c)
    k_all = (1.0 - lb) * (1.0 - sig_f)

    def step(factored):
        for bi in range(nb):
            for h in range(heads):
                q = q_ref[bi, :, h * dk:(h + 1) * dk].astype(F32)
                qs = q * _sigmoid(q) * (dk ** -0.5)
                vv = i_ref[bi, :, h * dv:(h + 1) * dv]
                o = _chunk_head(qs, k_all[bi * c:(bi + 1) * c, h * dk:(h + 1) * dk], vv,
                                b_all[bi][:, h * dk:(h + 1) * dk], st_ref, bi * heads + h, factored, kb_ref)
                g = g_ref[bi, :, h * dv:(h + 1) * dv].astype(F32)
                o = _rms(o, nw_ref[...]) * _sigmoid(g)
                o_ref[bi, :, h * dv:(h + 1) * dv] = o.astype(o_ref.dtype)

    lax.cond(_chunk_is_safe(b_all), functools.partial(step, True), functools.partial(step, False))


def _hgrn(p_hg, lower_bounds, onorm_w, layer, dm, offs):
    nb, c = dm.batch, dm.chunk
    hk, hv = dm.hg_heads * dm.hg_dk, dm.hg_heads * dm.hg_dv
    seg, p3 = _chunk_specs(nb, c, p_hg, offs)
    const = lambda shape: pl.BlockSpec(shape, lambda ci: (0, 0))
    out = pl.pallas_call(
        functools.partial(_hgrn_kernel, layer=layer, heads=dm.hg_heads, dk=dm.hg_dk, dv=dm.hg_dv),
        grid=(dm.seq // c,),
        in_specs=[seg("hg_q", hk), seg("hg_f", hk), seg("hg_i", hv), seg("hg_g", hv),
                  const((dm.depth, hk)), const((1, dm.hg_dv))],
        out_specs=pl.BlockSpec((nb, c, hv), lambda ci: (0, ci, 0)),
        out_shape=jax.ShapeDtypeStruct((nb, dm.seq, hv), BF16),
        scratch_shapes=[pltpu.VMEM((nb * dm.hg_heads, dm.hg_dv, dm.hg_dk), F32),
                        pltpu.VMEM((2, c, dm.hg_dk), F32)],
        compiler_params=_cparams(("arbitrary",)),
        name="hgrn",
    )(p3, p3, p3, p3, lower_bounds, onorm_w.reshape(1, dm.hg_dv))
    return out.reshape(nb * dm.seq, hv)


def _rope_table_kernel(pos_ref, f_ref, sgn_ref, cos_ref, sin_ref):
    ang = pos_ref[...].astype(F32) * f_ref[...]
    cos_ref[...] = jnp.cos(ang)
    sin_ref[...] = jnp.sin(ang) * sgn_ref[...]


def _rope_tables(pos_col, inv_freq_lanes, sign_lanes):
    m = pos_col.shape[0]
    tm = _tile(m, 512)
    lane = pl.BlockSpec((1, LANES), lambda i: (0, 0))
    tab = pl.BlockSpec((tm, LANES), lambda i: (i, 0))
    return pl.pallas_call(
        _rope_table_kernel,
        grid=(m // tm,),
        in_specs=[pl.BlockSpec((tm, 1), lambda i: (i, 0)), lane, lane],
        out_specs=[tab, tab],
        out_shape=[jax.ShapeDtypeStruct((m, LANES), F32)] * 2,
        compiler_params=_cparams(("parallel",)),
        name="rope_tables",
    )(pos_col, inv_freq_lanes, sign_lanes)


def _swap_halves(x, half):
    n = x.shape[-1]
    if 2 * half == n:
        return pltpu.roll(x, half, axis=1)
    lane = lax.broadcasted_iota(jnp.int32, x.shape, 1)
    return jnp.where((lane & half) == 0, pltpu.roll(x, n - half, axis=1), pltpu.roll(x, half, axis=1))


def _rope_apply_kernel(q_ref, k_ref, s_ref, c1_ref, s1_ref, c2_ref, s2_ref, qo_ref, ko_ref, iqo_ref,
                       iko_ref, iwo_ref, *, heads, dim, idx_heads, idx_dim):
    c1, s1, c2, s2 = c1_ref[...], s1_ref[...], c2_ref[...], s2_ref[...]
    q_scale = dim ** -0.5
    for h in range(heads):
        x = q_ref[:, h * dim:(h + 1) * dim].astype(F32)
        qo_ref[:, h * dim:(h + 1) * dim] = ((x * c1 + _swap_halves(x, dim // 2) * s1) * q_scale).astype(BF16)
    x = k_ref[...].astype(F32)
    ko_ref[...] = (x * c1 + _swap_halves(x, dim // 2) * s1).astype(BF16)
    iq_scale = idx_dim ** -0.5
    for g in range(idx_heads * idx_dim // LANES):
        x = s_ref[:, g * LANES:(g + 1) * LANES]
        r = (x * c2 + _swap_halves(x, idx_dim // 2) * s2) * iq_scale
        iqo_ref[:, g * LANES:(g + 1) * LANES] = r.astype(BF16)
    off = idx_heads * idx_dim
    x = s_ref[:, off:off + LANES]
    r = x * c2 + _swap_halves(x, idx_dim // 2) * s2
    lane = lax.broadcasted_iota(jnp.int32, x.shape, 1)
    iko_ref[...] = jnp.where(lane < idx_dim, r, pltpu.roll(r, idx_dim, axis=1)).astype(BF16)
    iwo_ref[...] = jnp.where(lane < idx_heads, pltpu.roll(x, LANES - idx_dim, axis=1), 0.0) * (idx_heads ** -0.5)


def _rope_apply(p_dsa, idx, tabs, dm, offs):
    m = p_dsa.shape[0]
    tm = _tile(m, 256)
    hd = dm.dsa_heads * dm.dsa_dim
    iq = dm.idx_heads * dm.idx_dim
    assert dm.dsa_dim == LANES and 2 * dm.idx_dim == LANES and dm.idx_heads <= dm.idx_dim
    assert offs["dsa_q"] % hd == 0 and offs["dsa_k"] % dm.dsa_dim == 0
    rows = lambda w, blk=0: pl.BlockSpec((tm, w), lambda i: (i, blk))
    return pl.pallas_call(
        functools.partial(_rope_apply_kernel, heads=dm.dsa_heads, dim=dm.dsa_dim, idx_heads=dm.idx_heads,
                          idx_dim=dm.idx_dim),
        grid=(m // tm,),
        in_specs=[rows(hd, offs["dsa_q"] // hd), rows(dm.dsa_dim, offs["dsa_k"] // dm.dsa_dim),
                  rows(idx.shape[1]), rows(LANES), rows(LANES), rows(LANES), rows(LANES)],
        out_specs=[rows(hd), rows(dm.dsa_dim), rows(iq), rows(LANES), rows(LANES)],
        out_shape=[jax.ShapeDtypeStruct((m, hd), BF16), jax.ShapeDtypeStruct((m, dm.dsa_dim), BF16),
                   jax.ShapeDtypeStruct((m, iq), BF16), jax.ShapeDtypeStruct((m, LANES), BF16),
                   jax.ShapeDtypeStruct((m, LANES), F32)],
        compiler_params=_cparams(("parallel",)),
        name="rope_apply",
    )(p_dsa, p_dsa, idx, *tabs)


def _ind(mask):
    return jnp.where(mask, 1.0, 0.0)


def _row_sum(x):
    return jnp.sum(x, axis=-1, keepdims=True)


def _float_of_ordered(u):
    k = u ^ jnp.int32(INT_MIN)
    return pltpu.bitcast(jnp.where(k < 0, k ^ jnp.int32(0x7FFFFFFF), k), F32)


def _dsa_kernel(q_ref, iq_ref, iw_ref, k_ref, v_ref, ik_ref, o_ref, *, heads, dim, idx_heads, idx_dim, topk,
                q_start, n_hidden):
    nbb, tq = q_ref.shape[0], q_ref.shape[1]
    tk = k_ref.shape[1]
    rows = nbb * tq
    nt = (((1,), (1,)), ((), ()))
    stack = lambda parts: parts[0] if len(parts) == 1 else jnp.concatenate(parts, axis=0)
    lane = lax.broadcasted_iota(jnp.int32, (tq, LANES), 1)
    scores = []
    for bi in range(nbb):
        ik = ik_ref[bi]
        score = jnp.zeros((tq, tk), F32)
        for h in range(idx_heads):
            g = (h * idx_dim) // LANES
            lo = (h * idx_dim) % LANES
            x = iq_ref[bi, :, g * LANES:(g + 1) * LANES]
            x = jnp.where(lane >= lo, jnp.where(lane < lo + idx_dim, x, jnp.zeros_like(x)), jnp.zeros_like(x))
            rel = lax.dot_general(x, ik, nt, preferred_element_type=F32)
            score = score + iw_ref[bi, :, h:h + 1] * jnp.maximum(rel, 0.0)
        scores.append(score)
    qpos = stack([q_start + lax.broadcasted_iota(jnp.int32, (tq, 1), 0)] * nbb)
    kpos = lax.broadcasted_iota(jnp.int32, (1, tk), 1)
    allowed = kpos <= qpos
    score = jnp.where(allowed, stack(scores), MASK_VALUE)
    hidden = float(n_hidden)

    def thr_body(it, prefix):
        trial = prefix | lax.shift_left(jnp.int32(1), 31 - it)
        cand = _float_of_ordered(trial)
        cnt = _row_sum(_ind(score >= cand)) + jnp.where(MASK_VALUE >= cand, hidden, 0.0)
        return jnp.where(cnt >= topk, trial, prefix)

    thr = _float_of_ordered(lax.fori_loop(0, 32, thr_body, jnp.zeros((rows, 1), jnp.int32)))
    above = _ind(score > thr)
    need = topk - _row_sum(above) - jnp.where(MASK_VALUE > thr, hidden, 0.0)
    tie = jnp.where(allowed, _ind(score == thr), 0.0)

    def all_ties():
        return jnp.where(allowed, _ind(score >= thr), 0.0)

    def ordered_ties():
        nbits = (tk - 1).bit_length()

        def tie_body(it, j):
            trial = j | lax.shift_left(jnp.int32(1), nbits - 1 - it)
            cnt = _row_sum(jnp.where(kpos < trial, tie, 0.0))
            return jnp.where(cnt < need, trial, j)

        j_last = lax.fori_loop(0, nbits, tie_body, jnp.zeros((rows, 1), jnp.int32))
        return jnp.where(allowed, above, 0.0) + jnp.where(kpos <= j_last, tie, 0.0)

    selected = lax.cond(jnp.max(_row_sum(tie) - need) > 0.0, ordered_ties, all_ties)

    for bi in range(nbb):
        valid = selected[bi * tq:(bi + 1) * tq, :] > 0.0
        k = k_ref[bi]
        v_ones = jnp.concatenate([v_ref[bi], jnp.ones((tk, dim), BF16)], axis=1)
        for h in range(heads):
            s = lax.dot_general(q_ref[bi, :, h * dim:(h + 1) * dim], k, nt, preferred_element_type=F32)
            s = jnp.where(valid, s, MASK_VALUE)
            p = jnp.exp(s - jnp.max(s, axis=-1, keepdims=True)).astype(BF16)
            o = jnp.dot(p, v_ones, preferred_element_type=F32)
            o_ref[bi, :, h * dim:(h + 1) * dim] = (o[:, 0:dim] / o[:, dim:dim + 1]).astype(o_ref.dtype)


def _with_carried_output(kern, q_ref, iq_ref, iw_ref, k_ref, v_ref, ik_ref, carried_ref, o_ref):
    del carried_ref
    kern(q_ref, iq_ref, iw_ref, k_ref, v_ref, ik_ref, o_ref)


def _zero_fill_kernel(o_ref):
    o_ref[...] = jnp.zeros_like(o_ref)


def _dsa(q_rot, k_rot, p_dsa, iq_rot, ik_rot, iw, dm, offs):
    tq, t, nb = dm.q_block, dm.seq, dm.batch
    hd = dm.dsa_heads * dm.dsa_dim
    assert offs["dsa_v"] % dm.dsa_dim == 0 and t % tq == 0
    v_blk = offs["dsa_v"] // dm.dsa_dim
    per_batch = lambda a: a.reshape(nb, t, a.shape[-1])
    q3, k3, p3, iq3, ik3, iw3 = map(per_batch, (q_rot, k_rot, p_dsa, iq_rot, ik_rot, iw))
    out = pl.pallas_call(
        _zero_fill_kernel,
        grid=(nb,),
        out_specs=pl.BlockSpec((None, t, hd), lambda bi: (bi, 0, 0)),
        out_shape=jax.ShapeDtypeStruct((nb, t, hd), BF16),
        compiler_params=_cparams(("parallel",)),
        name="dsa_out_init",
    )()
    for g in range(t // tq):
        tk = (g + 1) * tq
        nbb = 2 if (2 * tk <= t and nb % 2 == 0) else 1
        qrow = lambda w, g=g, nbb=nbb: pl.BlockSpec((nbb, tq, w), lambda bi: (bi, g, 0))
        krow = lambda w, blk=0, tk=tk, nbb=nbb: pl.BlockSpec((nbb, tk, w), lambda bi: (bi, 0, blk))
        kern = functools.partial(_dsa_kernel, heads=dm.dsa_heads, dim=dm.dsa_dim, idx_heads=dm.idx_heads,
                                 idx_dim=dm.idx_dim, topk=dm.topk, q_start=g * tq, n_hidden=t - tk)
        out = pl.pallas_call(
            functools.partial(_with_carried_output, kern),
            grid=(nb // nbb,),
            in_specs=[qrow(hd), qrow(iq3.shape[-1]), qrow(LANES), krow(dm.dsa_dim), krow(dm.dsa_dim, v_blk),
                      krow(LANES), pl.BlockSpec(memory_space=pl.ANY)],
            out_specs=qrow(hd),
            out_shape=jax.ShapeDtypeStruct((nb, t, hd), BF16),
            input_output_aliases={6: 0},
            compiler_params=_cparams(("parallel",)),
            name=f"dsa_q{g}",
        )(q3, iq3, iw3, k3, p3, ik3, out)
    return out.reshape(nb * t, hd)


def _forward(dm, x, positions, norm1_w, w_in, gla_gate_w2, gla_gate_b, gla_onorm_w, hgrn_lower_bounds,
             hgrn_onorm_w, w_branch_gla, w_branch_dsa, w_branch_hgrn, w_out, norm2_w, w_mlp_up, w_mlp_down,
             final_norm_w):
    m = dm.batch * dm.seq
    h = x.reshape(m, dm.d_model)
    src, ((gla_start, gla_width, gla_offs), (dsa_start, dsa_width, dsa_offs), (hg_start, hg_width, hg_offs)) = \
        _layout(dm)
    idx_start = src["idx_q"][0]
    idx_width = -(-(src["idx_w"][0] + src["idx_w"][1] - idx_start) // LANES) * LANES
    w_t = jnp.swapaxes(w_in, 1, 2)

    def lanes(d):
        inv = ROPE_THETA ** (-jnp.arange(0, d, 2, dtype=F32) / d)
        reps = LANES // d
        f = jnp.tile(jnp.concatenate([inv, inv]), reps).reshape(1, LANES)
        sgn = jnp.tile(jnp.concatenate([-jnp.ones(d // 2, F32), jnp.ones(d // 2, F32)]), reps).reshape(1, LANES)
        return f, sgn

    pos_col = positions.reshape(m, 1)
    tabs = _rope_tables(pos_col, *lanes(dm.dsa_dim)) + _rope_tables(pos_col, *lanes(dm.idx_dim))

    w_up_bf16, w_down_bf16 = w_mlp_up.astype(BF16), w_mlp_down.astype(BF16)
    u = _rmsnorm(h, norm1_w[0], BF16)
    for l in range(dm.depth):
        w2p = jnp.concatenate(
            [gla_gate_w2[l], jnp.zeros((LANES - dm.gla_rank, gla_gate_w2.shape[2]), F32)], axis=0)
        p_gla = _proj(u, w_t, l, gla_start, gla_width, BF16, name="in_proj_gla")
        p_hg = _proj(u, w_t, l, hg_start, hg_width, BF16, name="in_proj_hg")
        p_dsa = _proj(u, w_t, l, dsa_start, dsa_width, BF16, tm_pref=1024, tn_pref=768, name="in_proj_dsa")
        a_low = _proj(u, w_t, l, src["gla_a"][0], LANES, F32, name="in_proj_gate")
        idx = _proj(u, w_t, l, idx_start, idx_width, F32, tm_pref=1024, tn_pref=idx_width, name="in_proj_idx")
        o_gla = _gla(p_gla, a_low, w2p, gla_gate_b[l], gla_onorm_w[l], dm, gla_offs)
        o_hg = _hgrn(p_hg, hgrn_lower_bounds, hgrn_onorm_w[l], l, dm, hg_offs)
        q_rot, k_rot, iq_rot, ik_rot, iw = _rope_apply(p_dsa, idx, tabs, dm, dsa_offs)
        o_dsa = _dsa(q_rot, k_rot, p_dsa, iq_rot, ik_rot, iw, dm, dsa_offs)
        merged = _merge(o_gla, o_dsa, o_hg, w_branch_gla, w_branch_dsa, w_branch_hgrn, l, p_hg,
                        (hg_offs["gate_a"], hg_offs["gate_b"], hg_offs["gate_c"]), dm.d_model)
        h = _matmul_residual(merged, w_out, l, h)
        last = l == dm.depth - 1
        res = _mlp(h, norm2_w[l], w_up_bf16, w_down_bf16, l, final_norm_w if last else norm1_w[l + 1], last)
        if not last:
            h, u = res
    return res.reshape(dm.batch, dm.seq, dm.d_model)


def kernel(x, positions, norm1_w, w_in, gla_gate_w2, gla_gate_b, gla_onorm_w, hgrn_lower_bounds, hgrn_onorm_w,
           w_branch_gla, w_branch_dsa, w_branch_hgrn, w_out, norm2_w, w_mlp_up, w_mlp_down, final_norm_w):
    return _forward(_prod_dims(), x, positions, norm1_w, w_in, gla_gate_w2, gla_gate_b, gla_onorm_w,
                    hgrn_lower_bounds, hgrn_onorm_w, w_branch_gla, w_branch_dsa, w_branch_hgrn, w_out, norm2_w,
                    w_mlp_up, w_mlp_down, final_norm_w)
```

```python
import functools
from typing import NamedTuple

import jax
import jax.numpy as jnp
from jax import lax
from jax.experimental import pallas as pl
from jax.experimental.pallas import tpu as pltpu

F32 = jnp.float32
BF16 = jnp.bfloat16
HIGHEST = lax.Precision.HIGHEST

ROPE_THETA = 10000.0
NORM_EPS = 1e-6
MASK_VALUE = -1e30
F_MIN = 1e-12
GLA_GATE_NORMALIZER = 16.0
LANES = 128
SUBLANES = 8
ROW_BLOCK = 128
INT_MIN = -2 ** 31
SAFE_HALF_DECAY = 80.0
VMEM_LIMIT = 48 * 1024 * 1024
VMEM_LIMIT_LARGE = 56 * 1024 * 1024


class _Dims(NamedTuple):
    d_model: int
    batch: int
    seq: int
    depth: int
    gla_heads: int
    gla_dk: int
    gla_dv: int
    gla_rank: int
    dsa_heads: int
    dsa_dim: int
    idx_heads: int
    idx_dim: int
    topk: int
    hg_heads: int
    hg_dk: int
    hg_dv: int
    d_ff: int
    chunk: int
    q_block: int


def _prod_dims():
    d = 2048
    return _Dims(d_model=d, batch=4, seq=2048, depth=2,
                 gla_heads=4, gla_dk=d // 2 // 4, gla_dv=d // 4, gla_rank=16,
                 dsa_heads=16, dsa_dim=128, idx_heads=8, idx_dim=64, topk=min(256, 2048 // 4),
                 hg_heads=d // 128, hg_dk=128, hg_dv=128, d_ff=4 * d, chunk=128, q_block=256)


def _in_sizes(dm):
    return (dm.gla_heads * dm.gla_dk, dm.gla_heads * dm.gla_dk, dm.gla_heads * dm.gla_dv,
            dm.gla_heads * dm.gla_dv, dm.gla_rank,
            dm.dsa_heads * dm.dsa_dim, dm.dsa_dim, dm.dsa_dim, dm.idx_heads * dm.idx_dim, dm.idx_dim,
            dm.idx_heads,
            dm.hg_heads * dm.hg_dk, dm.hg_heads * dm.hg_dk, dm.hg_heads * dm.hg_dv, dm.hg_heads * dm.hg_dv,
            dm.d_model, dm.d_model, dm.d_model)


_IN_NAMES = ("gla_q", "gla_k", "gla_v", "gla_g", "gla_a", "dsa_q", "dsa_k", "dsa_v", "idx_q", "idx_k",
             "idx_w", "hg_q", "hg_f", "hg_i", "hg_g", "gate_a", "gate_b", "gate_c")
_GROUPS = (("gla_q", "gla_k", "gla_v", "gla_g"), ("dsa_q", "dsa_k", "dsa_v"),
           ("hg_q", "hg_f", "hg_i", "hg_g", "gate_a", "gate_b", "gate_c"))


def _tile(n, pref):
    t = min(n, pref)
    while n % t:
        t //= 2
    return t


def _layout(dm):
    sizes = dict(zip(_IN_NAMES, _in_sizes(dm)))
    src, off = {}, 0
    for name in _IN_NAMES:
        src[name] = (off, sizes[name])
        off += sizes[name]
    groups = []
    for names in _GROUPS:
        start = src[names[0]][0]
        offs = {n: src[n][0] - start for n in names}
        width = src[names[-1]][0] + src[names[-1]][1] - start
        for n in names:
            assert offs[n] % min(sizes[n], 2 * LANES) == 0, n
        groups.append((start, width, offs))
    return src, groups


def _cparams(sem, vmem=VMEM_LIMIT):
    return pltpu.CompilerParams(dimension_semantics=sem, vmem_limit_bytes=vmem)


def _rms(x, w):
    ms = jnp.mean(x * x, axis=-1, keepdims=True)
    return x * lax.rsqrt(ms + NORM_EPS) * w


def _rmsnorm_kernel(x_ref, w_ref, o_ref):
    o_ref[...] = _rms(x_ref[...], w_ref[...]).astype(o_ref.dtype)


def _rmsnorm(x, w, out_dtype):
    m, d = x.shape
    tm = _tile(m, 512)
    return pl.pallas_call(
        _rmsnorm_kernel,
        grid=(m // tm,),
        in_specs=[pl.BlockSpec((tm, d), lambda i: (i, 0)), pl.BlockSpec((1, d), lambda i: (0, 0))],
        out_specs=pl.BlockSpec((tm, d), lambda i: (i, 0)),
        out_shape=jax.ShapeDtypeStruct((m, d), out_dtype),
        compiler_params=_cparams(("parallel",)),
        name="rmsnorm",
    )(x, w.reshape(1, d))


def _w_spec(w, layer, k, tn):
    if w.ndim == 3:
        return pl.BlockSpec((None, k, tn), lambda i, j: (layer, 0, j))
    return pl.BlockSpec((k, tn), lambda i, j: (0, j))


def _store(y, o_ref):
    o_ref[...] = y.astype(o_ref.dtype)


def _proj_kernel(x_ref, *refs, shift, n_w, tn, epilogue):
    w_refs, rest = refs[:n_w], refs[n_w:]
    if n_w == 1:
        w = w_refs[0][...]
    else:
        w = jnp.concatenate([r[...] for r in w_refs], axis=0)[shift:shift + tn, :]
    y = lax.dot_general(x_ref[...], w.astype(BF16), (((1,), (1,)), ((), ())), preferred_element_type=F32)
    epilogue(y, *rest)


def _proj(x, w_t, layer, start, n, outs, epilogue=_store, tables=(), tm_pref=2048, tn_pref=512, name="proj"):
    m, k = x.shape
    tm, tn = _tile(m, tm_pref), _tile(n, tn_pref)
    outs = [(tn if cols is None else cols, dtype) for cols, dtype in outs]
    if start % tn == 0:
        shift = 0
        specs = [pl.BlockSpec((None, tn, k), lambda i, j: (layer, start // tn + j, 0))]
    else:
        assert tn % ROW_BLOCK == 0 and start % SUBLANES == 0
        base, shift = divmod(start, ROW_BLOCK)
        per_tile = tn // ROW_BLOCK
        blk = lambda b: pl.BlockSpec((None, ROW_BLOCK, k), lambda i, j: (layer, base + j * per_tile + b, 0))
        specs = [blk(b) for b in range(per_tile + 1)]
    res = pl.pallas_call(
        functools.partial(_proj_kernel, shift=shift, n_w=len(specs), tn=tn, epilogue=epilogue),
        grid=(m // tm, n // tn),
        in_specs=[pl.BlockSpec((tm, k), lambda i, j: (i, 0))] + specs
                 + [pl.BlockSpec((tm, LANES), lambda i, j: (i, 0)) for _ in tables],
        out_specs=[pl.BlockSpec((tm, cols), lambda i, j: (i, j)) for cols, _ in outs],
        out_shape=[jax.ShapeDtypeStruct((m, cols * (n // tn)), dtype) for cols, dtype in outs],
        compiler_params=_cparams(("parallel", "arbitrary")),
        name=name,
    )(x, *([w_t] * len(specs)), *tables)
    return res[0] if len(outs) == 1 else res


def _sigmoid(x):
    return 1.0 / (1.0 + jnp.exp(-x))


def _merge_kernel(oa_ref, ob_ref, oc_ref, wa_ref, wb_ref, wc_ref, ga_ref, gb_ref, gc_ref, o_ref):
    def branch(o_ref_, w_ref_, g_ref_):
        y = jnp.dot(o_ref_[...], w_ref_[...].astype(BF16), preferred_element_type=F32)
        return _sigmoid(g_ref_[...].astype(F32)) * y

    acc = branch(oa_ref, wa_ref, ga_ref) + branch(ob_ref, wb_ref, gb_ref) + branch(oc_ref, wc_ref, gc_ref)
    o_ref[...] = acc.astype(o_ref.dtype)


def _merge(o_gla, o_dsa, o_hg, w_gla, w_dsa, w_hg, layer, gates, gate_offs, d_model):
    m = o_gla.shape[0]
    tm, tn = _tile(m, 1024), _tile(d_model, 256)
    o_spec = lambda a: pl.BlockSpec((tm, a.shape[1]), lambda i, j: (i, 0))

    def g_spec(off):
        assert off % tn == 0
        return pl.BlockSpec((tm, tn), lambda i, j: (i, off // tn + j))

    return pl.pallas_call(
        _merge_kernel,
        grid=(m // tm, d_model // tn),
        in_specs=[o_spec(o_gla), o_spec(o_dsa), o_spec(o_hg),
                  _w_spec(w_gla, layer, w_gla.shape[-2], tn), _w_spec(w_dsa, layer, w_dsa.shape[-2], tn),
                  _w_spec(w_hg, layer, w_hg.shape[-2], tn),
                  g_spec(gate_offs[0]), g_spec(gate_offs[1]), g_spec(gate_offs[2])],
        out_specs=pl.BlockSpec((tm, tn), lambda i, j: (i, j)),
        out_shape=jax.ShapeDtypeStruct((m, d_model), BF16),
        compiler_params=_cparams(("parallel", "arbitrary"), VMEM_LIMIT_LARGE),
        name="merge",
    )(o_gla, o_dsa, o_hg, w_gla, w_dsa, w_hg, gates, gates, gates)


def _matmul_residual_kernel(x_ref, w_ref, r_ref, o_ref):
    o_ref[...] = r_ref[...] + jnp.dot(x_ref[...], w_ref[...].astype(BF16), preferred_element_type=F32)


def _matmul_residual(x, w, layer, res, tm_pref=1024, tn_pref=512):
    m, k = x.shape
    n = w.shape[-1]
    tm, tn = _tile(m, tm_pref), _tile(n, tn_pref)
    return pl.pallas_call(
        _matmul_residual_kernel,
        grid=(m // tm, n // tn),
        in_specs=[pl.BlockSpec((tm, k), lambda i, j: (i, 0)), _w_spec(w, layer, k, tn),
                  pl.BlockSpec((tm, tn), lambda i, j: (i, j))],
        out_specs=pl.BlockSpec((tm, tn), lambda i, j: (i, j)),
        out_shape=jax.ShapeDtypeStruct((m, n), F32),
        compiler_params=_cparams(("parallel", "arbitrary")),
        name="out_proj",
    )(x, w, res)


def _mlp_kernel(h_ref, nw_ref, wu_ref, wd_ref, nnw_ref, o_ref, *rest, last):
    u_ref = rest[-1]
    j = pl.program_id(1)

    @pl.when(j == 0)
    def _():
        x = h_ref[...]
        u_ref[...] = _rms(x, nw_ref[...]).astype(BF16)
        o_ref[...] = x

    a = jnp.dot(u_ref[...], wu_ref[...], preferred_element_type=F32)
    a = jnp.square(jnp.maximum(a, 0.0)).astype(BF16)
    o_ref[...] += jnp.dot(a, wd_ref[...], preferred_element_type=F32)

    @pl.when(j == pl.num_programs(1) - 1)
    def _():
        y = _rms(o_ref[...], nnw_ref[...])
        if last:
            o_ref[...] = y
        else:
            rest[0][...] = y.astype(BF16)


def _mlp(h, norm_w, w_up, w_down, layer, next_norm_w, last):
    m, d = h.shape
    f = w_up.shape[-1]
    tm, tf = _tile(m, 512), _tile(f, 1024)
    row = pl.BlockSpec((tm, d), lambda i, j: (i, 0))
    vec = pl.BlockSpec((1, d), lambda i, j: (0, 0))
    out_shape = [jax.ShapeDtypeStruct((m, d), F32)] + ([] if last else [jax.ShapeDtypeStruct((m, d), BF16)])
    res = pl.pallas_call(
        functools.partial(_mlp_kernel, last=last),
        grid=(m // tm, f // tf),
        in_specs=[row, vec, pl.BlockSpec((None, d, tf), lambda i, j: (layer, 0, j)),
                  pl.BlockSpec((None, tf, d), lambda i, j: (layer, j, 0)), vec],
        out_specs=[row] * len(out_shape),
        out_shape=out_shape,
        scratch_shapes=[pltpu.VMEM((tm, d), BF16)],
        compiler_params=_cparams(("parallel", "arbitrary"), VMEM_LIMIT_LARGE),
        name="mlp",
    )(h, norm_w.reshape(1, d), w_up, w_down, next_norm_w.reshape(1, d))
    return res[0] if last else res


def _chunk_head(qs, kk, vv, b, st_ref, head, factored, kb_ref):
    c, kdim = qs.shape
    row = lax.broadcasted_iota(jnp.int32, (c, c), 0)
    col = lax.broadcasted_iota(jnp.int32, (c, c), 1)
    b_last = b[c - 1:c, :]
    st = st_ref[head]
    nt = (((1,), (1,)), ((), ()))
    if factored:
        ref_row = b[c // 2 - 1:c // 2, :]
        qd = qs * jnp.exp(b - ref_row)
        kd = kk * jnp.exp(ref_row - b)
        q_in = qd * jnp.exp(ref_row)
        k_dec = kd * jnp.exp(b_last - ref_row)
        att = lax.dot_general(qd.astype(BF16), kd.astype(BF16), nt, preferred_element_type=F32)
    else:
        q_in = qs * jnp.exp(b)
        k_dec = kk * jnp.exp(b_last - b)
        kb_ref[0, :, 0:kdim] = kk
        kb_ref[1, :, 0:kdim] = b

        def body(s, att):
            k_row = kb_ref[0, pl.ds(s, 1), 0:kdim]
            b_row = kb_ref[1, pl.ds(s, 1), 0:kdim]
            w = jnp.sum(qs * k_row * jnp.exp(jnp.minimum(b - b_row, 0.0)), axis=-1, keepdims=True)
            return jnp.where(col == s, w, att)

        att = lax.fori_loop(0, c, body, jnp.zeros((c, c), F32))
    att = jnp.where(col <= row, att, 0.0)
    o = lax.dot_general(q_in.astype(BF16), st.astype(BF16), nt, preferred_element_type=F32)
    o = o + jnp.dot(att.astype(BF16), vv.astype(BF16), preferred_element_type=F32)
    upd = lax.dot_general(vv.astype(BF16), k_dec.astype(BF16), (((0,), (0,)), ((), ())),
                          preferred_element_type=F32)
    st_ref[head] = st * jnp.exp(b_last) + upd
    return o


def _cumsum_chunks(x, nb, c):
    row = lax.broadcasted_iota(jnp.int32, (c, c), 0)
    col = lax.broadcasted_iota(jnp.int32, (c, c), 1)
    tri = jnp.where(col <= row, 1.0, 0.0).astype(F32)
    return [jnp.dot(tri, x[bi * c:(bi + 1) * c, :], preferred_element_type=F32, precision=HIGHEST)
            for bi in range(nb)]


def _chunk_is_safe(b_all):
    c = b_all[0].shape[0]
    worst = None
    for b in b_all:
        ref_row = b[c // 2 - 1:c // 2, :]
        span = jnp.maximum(-ref_row, ref_row - b[c - 1:c, :])
        worst = span if worst is None else jnp.maximum(worst, span)
    return jnp.max(worst) <= SAFE_HALF_DECAY


def _log_sigmoid(x):
    return jnp.minimum(x, 0.0) - jnp.log(1.0 + jnp.exp(-jnp.abs(x)))


def _gla_kernel(q_ref, k_ref, v_ref, g_ref, a_ref, w2_ref, b2_ref, nw_ref, o_ref, st_ref, kb_ref, *,
                heads, dk, dv):
    nb, c = q_ref.shape[0], q_ref.shape[1]

    @pl.when(pl.program_id(0) == 0)
    def _():
        st_ref[...] = jnp.zeros_like(st_ref)

    a = a_ref[...].reshape(nb * c, a_ref.shape[2])
    z = jnp.dot(a, w2_ref[...], preferred_element_type=F32, precision=HIGHEST) + b2_ref[...]
    b_all = _cumsum_chunks(_log_sigmoid(z) * (1.0 / GLA_GATE_NORMALIZER), nb, c)

    def step(factored):
        for bi in range(nb):
            for h in range(heads):
                qs = q_ref[bi, :, h * dk:(h + 1) * dk].astype(F32) * (dk ** -0.5)
                kk = k_ref[bi, :, h * dk:(h + 1) * dk].astype(F32)
                vv = v_ref[bi, :, h * dv:(h + 1) * dv]
                o = _chunk_head(qs, kk, vv, b_all[bi][:, h * dk:(h + 1) * dk], st_ref, bi * heads + h, factored,
                                kb_ref)
                g = g_ref[bi, :, h * dv:(h + 1) * dv].astype(F32)
                o = _rms(o, nw_ref[...]) * (g * _sigmoid(g))
                o_ref[bi, :, h * dv:(h + 1) * dv] = o.astype(o_ref.dtype)

    lax.cond(_chunk_is_safe(b_all), functools.partial(step, True), functools.partial(step, False))


def _chunk_specs(nb, c, arr, offs):
    def seg(name, width):
        assert offs[name] % width == 0
        blk = offs[name] // width
        return pl.BlockSpec((nb, c, width), lambda ci: (0, ci, blk))

    return seg, arr.reshape(nb, -1, arr.shape[-1])


def _gla(p_gla, a_low, w2p, b2, onorm_w, dm, offs):
    nb, c = dm.batch, dm.chunk
    hk, hv = dm.gla_heads * dm.gla_dk, dm.gla_heads * dm.gla_dv
    seg, p3 = _chunk_specs(nb, c, p_gla, offs)
    a3 = a_low.reshape(nb, dm.seq, LANES)
    const = lambda shape: pl.BlockSpec(shape, lambda ci: (0, 0))
    out = pl.pallas_call(
        functools.partial(_gla_kernel, heads=dm.gla_heads, dk=dm.gla_dk, dv=dm.gla_dv),
        grid=(dm.seq // c,),
        in_specs=[seg("gla_q", hk), seg("gla_k", hk), seg("gla_v", hv), seg("gla_g", hv),
                  pl.BlockSpec((nb, c, LANES), lambda ci: (0, ci, 0)),
                  const((LANES, hk)), const((1, hk)), const((1, dm.gla_dv))],
        out_specs=pl.BlockSpec((nb, c, hv), lambda ci: (0, ci, 0)),
        out_shape=jax.ShapeDtypeStruct((nb, dm.seq, hv), BF16),
        scratch_shapes=[pltpu.VMEM((nb * dm.gla_heads, dm.gla_dv, dm.gla_dk), F32),
                        pltpu.VMEM((2, c, dm.gla_dk), F32)],
        compiler_params=_cparams(("arbitrary",)),
        name="gla",
    )(p3, p3, p3, p3, a3, w2p, b2.reshape(1, hk), onorm_w.reshape(1, dm.gla_dv))
    return out.reshape(nb * dm.seq, hv)


def _hgrn_kernel(q_ref, f_ref, i_ref, g_ref, lbp_ref, nw_ref, o_ref, st_ref, kb_ref, *, layer, heads, dk, dv):
    nb, c = q_ref.shape[0], q_ref.shape[1]

    @pl.when(pl.program_id(0) == 0)
    def _():
        st_ref[...] = jnp.zeros_like(st_ref)

    lbp = lbp_ref[...]
    e = jnp.exp(lbp - jnp.max(lbp, axis=0, keepdims=True))
    p = e / jnp.sum(e, axis=0, keepdims=True)
    lb = jnp.zeros_like(p[0:1, :])
    for j in range(1, layer + 1):
        lb = lb + p[j:j + 1, :]
    sig_f = _sigmoid(f_ref[...].reshape(nb * c, f_ref.shape[2]).astype(F32))
    f_gate = lb + (1.0 - lb) * sig_f
    b_all = _cumsum_chunks(jnp.log(jnp.maximum(f_gate, F_MIN)), nb, c)
    k_all = (1.0 - lb) * (1.0 - sig_f)

    def step(factored):
        for bi in range(nb):
            for h in range(heads):
                q = q_ref[bi, :, h * dk:(h + 1) * dk].astype(F32)
                qs = q * _sigmoid(q) * (dk ** -0.5)
                vv = i_ref[bi, :, h * dv:(h + 1) * dv]
                o = _chunk_head(qs, k_all[bi * c:(bi + 1) * c, h * dk:(h + 1) * dk], vv,
                                b_all[bi][:, h * dk:(h + 1) * dk], st_ref, bi * heads + h, factored, kb_ref)
                g = g_ref[bi, :, h * dv:(h + 1) * dv].astype(F32)
                o = _rms(o, nw_ref[...]) * _sigmoid(g)
                o_ref[bi, :, h * dv:(h + 1) * dv] = o.astype(o_ref.dtype)

    lax.cond(_chunk_is_safe(b_all), functools.partial(step, True), functools.partial(step, False))


def _hgrn(p_hg, lower_bounds, onorm_w, layer, dm, offs):
    nb, c = dm.batch, dm.chunk
    hk, hv = dm.hg_heads * dm.hg_dk, dm.hg_heads * dm.hg_dv
    seg, p3 = _chunk_specs(nb, c, p_hg, offs)
    const = lambda shape: pl.BlockSpec(shape, lambda ci: (0, 0))
    out = pl.pallas_call(
        functools.partial(_hgrn_kernel, layer=layer, heads=dm.hg_heads, dk=dm.hg_dk, dv=dm.hg_dv),
        grid=(dm.seq // c,),
        in_specs=[seg("hg_q", hk), seg("hg_f", hk), seg("hg_i", hv), seg("hg_g", hv),
                  const((dm.depth, hk)), const((1, dm.hg_dv))],
        out_specs=pl.BlockSpec((nb, c, hv), lambda ci: (0, ci, 0)),
        out_shape=jax.ShapeDtypeStruct((nb, dm.seq, hv), BF16),
        scratch_shapes=[pltpu.VMEM((nb * dm.hg_heads, dm.hg_dv, dm.hg_dk), F32),
                        pltpu.VMEM((2, c, dm.hg_dk), F32)],
        compiler_params=_cparams(("arbitrary",)),
        name="hgrn",
    )(p3, p3, p3, p3, lower_bounds, onorm_w.reshape(1, dm.hg_dv))
    return out.reshape(nb * dm.seq, hv)


def _rope_table_kernel(pos_ref, f_ref, sgn_ref, cos_ref, sin_ref):
    ang = pos_ref[...].astype(F32) * f_ref[...]
    cos_ref[...] = jnp.cos(ang)
    sin_ref[...] = jnp.sin(ang) * sgn_ref[...]


def _rope_tables(pos_col, inv_freq_lanes, sign_lanes):
    m = pos_col.shape[0]
    tm = _tile(m, 512)
    lane = pl.BlockSpec((1, LANES), lambda i: (0, 0))
    tab = pl.BlockSpec((tm, LANES), lambda i: (i, 0))
    return pl.pallas_call(
        _rope_table_kernel,
        grid=(m // tm,),
        in_specs=[pl.BlockSpec((tm, 1), lambda i: (i, 0)), lane, lane],
        out_specs=[tab, tab],
        out_shape=[jax.ShapeDtypeStruct((m, LANES), F32)] * 2,
        compiler_params=_cparams(("parallel",)),
        name="rope_tables",
    )(pos_col, inv_freq_lanes, sign_lanes)


def _swap_halves(x, half):
    n = x.shape[-1]
    if 2 * half == n:
        return pltpu.roll(x, half, axis=1)
    lane = lax.broadcasted_iota(jnp.int32, x.shape, 1)
    return jnp.where((lane & half) == 0, pltpu.roll(x, n - half, axis=1), pltpu.roll(x, half, axis=1))


def _rope_epilogue(y, cos_ref, sin_ref, o_ref, *, half, scale, copy_groups=()):
    c, s = cos_ref[...], sin_ref[...]
    for g in range(y.shape[1] // LANES):
        yg = y[:, g * LANES:(g + 1) * LANES]
        if g not in copy_groups:
            yg = (yg * c + _swap_halves(yg, half) * s) * scale
        o_ref[:, g * LANES:(g + 1) * LANES] = yg.astype(o_ref.dtype)


def _idx_epilogue(y, cos_ref, sin_ref, iq_ref, ik_ref, iw_ref, *, idx_heads, idx_dim):
    c, s = cos_ref[...], sin_ref[...]
    nq = idx_heads * idx_dim // LANES
    for g in range(nq):
        yg = y[:, g * LANES:(g + 1) * LANES]
        r = (yg * c + _swap_halves(yg, idx_dim // 2) * s) * (idx_dim ** -0.5)
        iq_ref[:, g * LANES:(g + 1) * LANES] = r.astype(iq_ref.dtype)
    x = y[:, nq * LANES:(nq + 1) * LANES]
    r = x * c + _swap_halves(x, idx_dim // 2) * s
    lane = lax.broadcasted_iota(jnp.int32, x.shape, 1)
    ik_ref[...] = jnp.where(lane < idx_dim, r, pltpu.roll(r, idx_dim, axis=1)).astype(ik_ref.dtype)
    iw_ref[...] = jnp.where(lane < idx_heads, pltpu.roll(x, LANES - idx_dim, axis=1), 0.0) * (idx_heads ** -0.5)


def _ind(mask):
    return jnp.where(mask, 1.0, 0.0)


def _row_sum(x):
    return jnp.sum(x, axis=-1, keepdims=True)


def _float_of_ordered(u):
    k = u ^ jnp.int32(INT_MIN)
    return pltpu.bitcast(jnp.where(k < 0, k ^ jnp.int32(0x7FFFFFFF), k), F32)


def _dsa_kernel(q_ref, iq_ref, iw_ref, k_ref, v_ref, ik_ref, o_ref, *, heads, dim, idx_heads, idx_dim, topk,
                q_start, n_hidden):
    nbb, tq = q_ref.shape[0], q_ref.shape[1]
    tk = k_ref.shape[1]
    rows = nbb * tq
    nt = (((1,), (1,)), ((), ()))
    stack = lambda parts: parts[0] if len(parts) == 1 else jnp.concatenate(parts, axis=0)
    lane = lax.broadcasted_iota(jnp.int32, (tq, LANES), 1)
    scores = []
    for bi in range(nbb):
        ik = ik_ref[bi]
        score = jnp.zeros((tq, tk), F32)
        for h in range(idx_heads):
            g = (h * idx_dim) // LANES
            lo = (h * idx_dim) % LANES
            x = iq_ref[bi, :, g * LANES:(g + 1) * LANES]
            x = jnp.where(lane >= lo, jnp.where(lane < lo + idx_dim, x, jnp.zeros_like(x)), jnp.zeros_like(x))
            rel = lax.dot_general(x, ik, nt, preferred_element_type=F32)
            score = score + iw_ref[bi, :, h:h + 1] * jnp.maximum(rel, 0.0)
        scores.append(score)
    qpos = stack([q_start + lax.broadcasted_iota(jnp.int32, (tq, 1), 0)] * nbb)
    kpos = lax.broadcasted_iota(jnp.int32, (1, tk), 1)
    allowed = kpos <= qpos
    score = jnp.where(allowed, stack(scores), MASK_VALUE)
    hidden = float(n_hidden)

    def thr_body(it, prefix):
        trial = prefix | lax.shift_left(jnp.int32(1), 31 - it)
        cand = _float_of_ordered(trial)
        cnt = _row_sum(_ind(score >= cand)) + jnp.where(MASK_VALUE >= cand, hidden, 0.0)
        return jnp.where(cnt >= topk, trial, prefix)

    thr = _float_of_ordered(lax.fori_loop(0, 32, thr_body, jnp.zeros((rows, 1), jnp.int32)))
    above = _ind(score > thr)
    need = topk - _row_sum(above) - jnp.where(MASK_VALUE > thr, hidden, 0.0)
    tie = jnp.where(allowed, _ind(score == thr), 0.0)

    def all_ties():
        return jnp.where(allowed, _ind(score >= thr), 0.0)

    def ordered_ties():
        nbits = (tk - 1).bit_length()

        def tie_body(it, j):
            trial = j | lax.shift_left(jnp.int32(1), nbits - 1 - it)
            cnt = _row_sum(jnp.where(kpos < trial, tie, 0.0))
            return jnp.where(cnt < need, trial, j)

        j_last = lax.fori_loop(0, nbits, tie_body, jnp.zeros((rows, 1), jnp.int32))
        return jnp.where(allowed, above, 0.0) + jnp.where(kpos <= j_last, tie, 0.0)

    selected = lax.cond(jnp.max(_row_sum(tie) - need) > 0.0, ordered_ties, all_ties)

    for bi in range(nbb):
        valid = selected[bi * tq:(bi + 1) * tq, :] > 0.0
        k = k_ref[bi]
        v_ones = jnp.concatenate([v_ref[bi], jnp.ones((tk, dim), BF16)], axis=1)
        for h in range(heads):
            s = lax.dot_general(q_ref[bi, :, h * dim:(h + 1) * dim], k, nt, preferred_element_type=F32)
            s = jnp.where(valid, s, MASK_VALUE)
            p = jnp.exp(s - jnp.max(s, axis=-1, keepdims=True)).astype(BF16)
            o = jnp.dot(p, v_ones, preferred_element_type=F32)
            o_ref[bi, :, h * dim:(h + 1) * dim] = (o[:, 0:dim] / o[:, dim:dim + 1]).astype(o_ref.dtype)


def _with_carried_output(kern, q_ref, iq_ref, iw_ref, k_ref, v_ref, ik_ref, carried_ref, o_ref):
    del carried_ref
    kern(q_ref, iq_ref, iw_ref, k_ref, v_ref, ik_ref, o_ref)


def _zero_fill_kernel(o_ref):
    o_ref[...] = jnp.zeros_like(o_ref)


def _dsa(q_rot, kv, iq_rot, ik_rot, iw, dm):
    tq, t, nb = dm.q_block, dm.seq, dm.batch
    hd = dm.dsa_heads * dm.dsa_dim
    assert t % tq == 0
    per_batch = lambda a: a.reshape(nb, t, a.shape[-1])
    q3, kv3, iq3, ik3, iw3 = map(per_batch, (q_rot, kv, iq_rot, ik_rot, iw))
    out = pl.pallas_call(
        _zero_fill_kernel,
        grid=(nb,),
        out_specs=pl.BlockSpec((None, t, hd), lambda bi: (bi, 0, 0)),
        out_shape=jax.ShapeDtypeStruct((nb, t, hd), BF16),
        compiler_params=_cparams(("parallel",)),
        name="dsa_out_init",
    )()
    for g in range(t // tq):
        tk = (g + 1) * tq
        nbb = 2 if (2 * tk <= t and nb % 2 == 0) else 1
        qrow = lambda w, g=g, nbb=nbb: pl.BlockSpec((nbb, tq, w), lambda bi: (bi, g, 0))
        krow = lambda w, blk=0, tk=tk, nbb=nbb: pl.BlockSpec((nbb, tk, w), lambda bi: (bi, 0, blk))
        kern = functools.partial(_dsa_kernel, heads=dm.dsa_heads, dim=dm.dsa_dim, idx_heads=dm.idx_heads,
                                 idx_dim=dm.idx_dim, topk=dm.topk, q_start=g * tq, n_hidden=t - tk)
        out = pl.pallas_call(
            functools.partial(_with_carried_output, kern),
            grid=(nb // nbb,),
            in_specs=[qrow(hd), qrow(iq3.shape[-1]), qrow(LANES), krow(dm.dsa_dim), krow(dm.dsa_dim, 1),
                      krow(LANES), pl.BlockSpec(memory_space=pl.ANY)],
            out_specs=qrow(hd),
            out_shape=jax.ShapeDtypeStruct((nb, t, hd), BF16),
            input_output_aliases={6: 0},
            compiler_params=_cparams(("parallel",)),
            name=f"dsa_q{g}",
        )(q3, iq3, iw3, kv3, kv3, ik3, out)
    return out.reshape(nb * t, hd)


def _forward(dm, x, positions, norm1_w, w_in, gla_gate_w2, gla_gate_b, gla_onorm_w, hgrn_lower_bounds,
             hgrn_onorm_w, w_branch_gla, w_branch_dsa, w_branch_hgrn, w_out, norm2_w, w_mlp_up, w_mlp_down,
             final_norm_w):
    m = dm.batch * dm.seq
    h = x.reshape(m, dm.d_model)
    src, ((gla_start, gla_width, gla_offs), (dsa_start, dsa_width, dsa_offs), (hg_start, hg_width, hg_offs)) = \
        _layout(dm)
    assert dm.dsa_dim == LANES and 2 * dm.idx_dim == LANES and dm.idx_heads <= dm.idx_dim
    idx_start = src["idx_q"][0]
    idx_width = -(-(src["idx_w"][0] + src["idx_w"][1] - idx_start) // LANES) * LANES
    w_t = jnp.swapaxes(w_in, 1, 2)

    def lanes(d):
        inv = ROPE_THETA ** (-jnp.arange(0, d, 2, dtype=F32) / d)
        reps = LANES // d
        f = jnp.tile(jnp.concatenate([inv, inv]), reps).reshape(1, LANES)
        sgn = jnp.tile(jnp.concatenate([-jnp.ones(d // 2, F32), jnp.ones(d // 2, F32)]), reps).reshape(1, LANES)
        return f, sgn

    pos_col = positions.reshape(m, 1)
    tabs = _rope_tables(pos_col, *lanes(dm.dsa_dim)) + _rope_tables(pos_col, *lanes(dm.idx_dim))

    w_up_bf16, w_down_bf16 = w_mlp_up.astype(BF16), w_mlp_down.astype(BF16)
    u = _rmsnorm(h, norm1_w[0], BF16)
    for l in range(dm.depth):
        w2p = jnp.concatenate(
            [gla_gate_w2[l], jnp.zeros((LANES - dm.gla_rank, gla_gate_w2.shape[2]), F32)], axis=0)
        p_gla = _proj(u, w_t, l, gla_start, gla_width, [(None, BF16)], name="in_proj_gla")
        p_hg = _proj(u, w_t, l, hg_start, hg_width, [(None, BF16)], name="in_proj_hg")
        a_low = _proj(u, w_t, l, src["gla_a"][0], LANES, [(LANES, F32)], name="in_proj_gate")
        hd = dm.dsa_heads * dm.dsa_dim
        q_rot = _proj(u, w_t, l, dsa_start, hd, [(None, BF16)], tables=tabs[:2], tm_pref=1024, name="in_proj_dsa_q",
                      epilogue=functools.partial(_rope_epilogue, half=dm.dsa_dim // 2, scale=dm.dsa_dim ** -0.5))
        kv = _proj(u, w_t, l, dsa_start + dsa_offs["dsa_k"], 2 * dm.dsa_dim, [(2 * dm.dsa_dim, BF16)],
                   tables=tabs[:2], tm_pref=1024, tn_pref=2 * dm.dsa_dim, name="in_proj_dsa_kv",
                   epilogue=functools.partial(_rope_epilogue, half=dm.dsa_dim // 2, scale=1.0, copy_groups=(1,)))
        iq_rot, ik_rot, iw = _proj(
            u, w_t, l, idx_start, idx_width, [(dm.idx_heads * dm.idx_dim, BF16), (LANES, BF16), (LANES, F32)],
            tables=tabs[2:], tm_pref=1024, tn_pref=idx_width, name="in_proj_idx",
            epilogue=functools.partial(_idx_epilogue, idx_heads=dm.idx_heads, idx_dim=dm.idx_dim))
        o_gla = _gla(p_gla, a_low, w2p, gla_gate_b[l], gla_onorm_w[l], dm, gla_offs)
        o_hg = _hgrn(p_hg, hgrn_lower_bounds, hgrn_onorm_w[l], l, dm, hg_offs)
        o_dsa = _dsa(q_rot, kv, iq_rot, ik_rot, iw, dm)
        merged = _merge(o_gla, o_dsa, o_hg, w_branch_gla, w_branch_dsa, w_branch_hgrn, l, p_hg,
                        (hg_offs["gate_a"], hg_offs["gate_b"], hg_offs["gate_c"]), dm.d_model)
        h = _matmul_residual(merged, w_out, l, h)
        last = l == dm.depth - 1
        res = _mlp(h, norm2_w[l], w_up_bf16, w_down_bf16, l, final_norm_w if last else norm1_w[l + 1], last)
        if not last:
            h, u = res
    return res.reshape(dm.batch, dm.seq, dm.d_model)


def kernel(x, positions, norm1_w, w_in, gla_gate_w2, gla_gate_b, gla_onorm_w, hgrn_lower_bounds, hgrn_onorm_w,
           w_branch_gla, w_branch_dsa, w_branch_hgrn, w_out, norm2_w, w_mlp_up, w_mlp_down, final_norm_w):
    return _forward(_prod_dims(), x, positions, norm1_w, w_in, gla_gate_w2, gla_gate_b, gla_onorm_w,
                    hgrn_lower_bounds, hgrn_onorm_w, w_branch_gla, w_branch_dsa, w_branch_hgrn, w_out, norm2_w,
                    w_mlp_up, w_mlp_down, final_norm_w)
```

```python
import functools
from typing import NamedTuple

import jax
import jax.numpy as jnp
from jax import lax
from jax.experimental import pallas as pl
from jax.experimental.pallas import tpu as pltpu

F32 = jnp.float32
BF16 = jnp.bfloat16
HIGHEST = lax.Precision.HIGHEST

ROPE_THETA = 10000.0
NORM_EPS = 1e-6
MASK_VALUE = -1e30
F_MIN = 1e-12
GLA_GATE_NORMALIZER = 16.0
LANES = 128
SUBLANES = 8
ROW_BLOCK = 128
INT_MIN = -2 ** 31
SAFE_HALF_DECAY = 80.0
VMEM_LIMIT = 48 * 1024 * 1024
VMEM_LIMIT_LARGE = 56 * 1024 * 1024


class _Dims(NamedTuple):
    d_model: int
    batch: int
    seq: int
    depth: int
    gla_heads: int
    gla_dk: int
    gla_dv: int
    gla_rank: int
    dsa_heads: int
    dsa_dim: int
    idx_heads: int
    idx_dim: int
    topk: int
    hg_heads: int
    hg_dk: int
    hg_dv: int
    d_ff: int
    chunk: int
    q_block: int


def _prod_dims():
    d = 2048
    return _Dims(d_model=d, batch=4, seq=2048, depth=2,
                 gla_heads=4, gla_dk=d // 2 // 4, gla_dv=d // 4, gla_rank=16,
                 dsa_heads=16, dsa_dim=128, idx_heads=8, idx_dim=64, topk=min(256, 2048 // 4),
                 hg_heads=d // 128, hg_dk=128, hg_dv=128, d_ff=4 * d, chunk=128, q_block=256)


def _in_sizes(dm):
    return (dm.gla_heads * dm.gla_dk, dm.gla_heads * dm.gla_dk, dm.gla_heads * dm.gla_dv,
            dm.gla_heads * dm.gla_dv, dm.gla_rank,
            dm.dsa_heads * dm.dsa_dim, dm.dsa_dim, dm.dsa_dim, dm.idx_heads * dm.idx_dim, dm.idx_dim,
            dm.idx_heads,
            dm.hg_heads * dm.hg_dk, dm.hg_heads * dm.hg_dk, dm.hg_heads * dm.hg_dv, dm.hg_heads * dm.hg_dv,
            dm.d_model, dm.d_model, dm.d_model)


_IN_NAMES = ("gla_q", "gla_k", "gla_v", "gla_g", "gla_a", "dsa_q", "dsa_k", "dsa_v", "idx_q", "idx_k",
             "idx_w", "hg_q", "hg_f", "hg_i", "hg_g", "gate_a", "gate_b", "gate_c")
_GROUPS = (("gla_q", "gla_k", "gla_v", "gla_g"), ("dsa_q", "dsa_k", "dsa_v"),
           ("hg_q", "hg_f", "hg_i", "hg_g", "gate_a", "gate_b", "gate_c"))


def _tile(n, pref):
    t = min(n, pref)
    while n % t:
        t //= 2
    return t


def _layout(dm):
    sizes = dict(zip(_IN_NAMES, _in_sizes(dm)))
    src, off = {}, 0
    for name in _IN_NAMES:
        src[name] = (off, sizes[name])
        off += sizes[name]
    groups = []
    for names in _GROUPS:
        start = src[names[0]][0]
        offs = {n: src[n][0] - start for n in names}
        width = src[names[-1]][0] + src[names[-1]][1] - start
        for n in names:
            assert offs[n] % min(sizes[n], 2 * LANES) == 0, n
        groups.append((start, width, offs))
    return src, groups


def _cparams(sem, vmem=VMEM_LIMIT):
    return pltpu.CompilerParams(dimension_semantics=sem, vmem_limit_bytes=vmem)


def _rms(x, w):
    ms = jnp.mean(x * x, axis=-1, keepdims=True)
    return x * lax.rsqrt(ms + NORM_EPS) * w


def _rmsnorm_kernel(x_ref, w_ref, o_ref):
    o_ref[...] = _rms(x_ref[...], w_ref[...]).astype(o_ref.dtype)


def _rmsnorm(x, w, out_dtype):
    m, d = x.shape
    tm = _tile(m, 512)
    return pl.pallas_call(
        _rmsnorm_kernel,
        grid=(m // tm,),
        in_specs=[pl.BlockSpec((tm, d), lambda i: (i, 0)), pl.BlockSpec((1, d), lambda i: (0, 0))],
        out_specs=pl.BlockSpec((tm, d), lambda i: (i, 0)),
        out_shape=jax.ShapeDtypeStruct((m, d), out_dtype),
        compiler_params=_cparams(("parallel",)),
        name="rmsnorm",
    )(x, w.reshape(1, d))


def _w_spec(w, layer, k, tn):
    if w.ndim == 3:
        return pl.BlockSpec((None, k, tn), lambda i, j: (layer, 0, j))
    return pl.BlockSpec((k, tn), lambda i, j: (0, j))


def _store(y, o_ref):
    o_ref[...] = y.astype(o_ref.dtype)


def _proj_kernel(x_ref, *refs, shift, n_w, n_tab, n_cast, tn, epilogue):
    w_refs, rest = refs[:n_w], refs[n_w:]
    tabs, cast_in, rest = rest[:n_tab], rest[n_tab:n_tab + n_cast], rest[n_tab + n_cast:]
    outs, cast_out = rest[:len(rest) - n_cast], rest[len(rest) - n_cast:]
    if n_w == 1:
        w = w_refs[0][...]
    else:
        w = jnp.concatenate([r[...] for r in w_refs], axis=0)[shift:shift + tn, :]
    y = lax.dot_general(x_ref[...], w.astype(BF16), (((1,), (1,)), ((), ())), preferred_element_type=F32)
    epilogue(y, *tabs, *outs)
    for src_ref, dst_ref in zip(cast_in, cast_out):
        dst_ref[...] = src_ref[...].astype(dst_ref.dtype)


def _cast_job(w, layer, gm, gn, axis_i):
    dims = list(w.shape[1:])
    other = dims[1 - axis_i]
    nj = max(d for d in range(1, gn + 1) if other % d == 0 and (other // d) % LANES == 0)
    blk = [0, 0]
    blk[axis_i], blk[1 - axis_i] = dims[axis_i] // gm, other // nj
    assert dims[axis_i] % gm == 0 and blk[axis_i] % LANES == 0

    def imap(i, j):
        jj = jnp.minimum(j, nj - 1)
        return (i, jj) if axis_i == 0 else (jj, i)

    return (pl.BlockSpec((None, *blk), lambda i, j: (layer, *imap(i, j))), pl.BlockSpec(tuple(blk), imap),
            jax.ShapeDtypeStruct(tuple(dims), BF16))


def _proj(x, w_t, layer, start, n, outs, epilogue=_store, tables=(), casts=(), tm_pref=2048, tn_pref=512,
          vmem=VMEM_LIMIT, name="proj"):
    m, k = x.shape
    tm, tn = _tile(m, tm_pref), _tile(n, tn_pref)
    outs = [(tn if cols is None else cols, dtype) for cols, dtype in outs]
    jobs = [_cast_job(w, layer, m // tm, n // tn, axis_i) for w, axis_i in casts]
    if start % tn == 0:
        shift = 0
        specs = [pl.BlockSpec((None, tn, k), lambda i, j: (layer, start // tn + j, 0))]
    else:
        assert tn % ROW_BLOCK == 0 and start % SUBLANES == 0
        base, shift = divmod(start, ROW_BLOCK)
        per_tile = tn // ROW_BLOCK
        blk = lambda b: pl.BlockSpec((None, ROW_BLOCK, k), lambda i, j: (layer, base + j * per_tile + b, 0))
        specs = [blk(b) for b in range(per_tile + 1)]
    res = pl.pallas_call(
        functools.partial(_proj_kernel, shift=shift, n_w=len(specs), n_tab=len(tables), n_cast=len(jobs), tn=tn,
                          epilogue=epilogue),
        grid=(m // tm, n // tn),
        in_specs=[pl.BlockSpec((tm, k), lambda i, j: (i, 0))] + specs
                 + [pl.BlockSpec((tm, LANES), lambda i, j: (i, 0)) for _ in tables] + [job[0] for job in jobs],
        out_specs=[pl.BlockSpec((tm, cols), lambda i, j: (i, j)) for cols, _ in outs] + [job[1] for job in jobs],
        out_shape=[jax.ShapeDtypeStruct((m, cols * (n // tn)), dtype) for cols, dtype in outs]
                  + [job[2] for job in jobs],
        compiler_params=_cparams(("parallel", "arbitrary"), vmem),
        name=name,
    )(x, *([w_t] * len(specs)), *tables, *[w for w, _ in casts])
    return res[0] if len(res) == 1 else res


def _sigmoid(x):
    return 1.0 / (1.0 + jnp.exp(-x))


def _merge_kernel(oa_ref, ob_ref, oc_ref, wa_ref, wb_ref, wc_ref, ga_ref, gb_ref, gc_ref, o_ref):
    def branch(o_ref_, w_ref_, g_ref_):
        y = jnp.dot(o_ref_[...], w_ref_[...].astype(BF16), preferred_element_type=F32)
        return _sigmoid(g_ref_[...].astype(F32)) * y

    acc = branch(oa_ref, wa_ref, ga_ref) + branch(ob_ref, wb_ref, gb_ref) + branch(oc_ref, wc_ref, gc_ref)
    o_ref[...] = acc.astype(o_ref.dtype)


def _merge(o_gla, o_dsa, o_hg, w_gla, w_dsa, w_hg, layer, gates, gate_offs, d_model):
    m = o_gla.shape[0]
    tm, tn = _tile(m, 1024), _tile(d_model, 256)
    o_spec = lambda a: pl.BlockSpec((tm, a.shape[1]), lambda i, j: (i, 0))

    def g_spec(off):
        assert off % tn == 0
        return pl.BlockSpec((tm, tn), lambda i, j: (i, off // tn + j))

    return pl.pallas_call(
        _merge_kernel,
        grid=(m // tm, d_model // tn),
        in_specs=[o_spec(o_gla), o_spec(o_dsa), o_spec(o_hg),
                  _w_spec(w_gla, layer, w_gla.shape[-2], tn), _w_spec(w_dsa, layer, w_dsa.shape[-2], tn),
                  _w_spec(w_hg, layer, w_hg.shape[-2], tn),
                  g_spec(gate_offs[0]), g_spec(gate_offs[1]), g_spec(gate_offs[2])],
        out_specs=pl.BlockSpec((tm, tn), lambda i, j: (i, j)),
        out_shape=jax.ShapeDtypeStruct((m, d_model), BF16),
        compiler_params=_cparams(("parallel", "arbitrary"), VMEM_LIMIT_LARGE),
        name="merge",
    )(o_gla, o_dsa, o_hg, w_gla, w_dsa, w_hg, gates, gates, gates)


def _matmul_residual_kernel(x_ref, w_ref, r_ref, o_ref):
    o_ref[...] = r_ref[...] + jnp.dot(x_ref[...], w_ref[...].astype(BF16), preferred_element_type=F32)


def _matmul_residual(x, w, layer, res, tm_pref=1024, tn_pref=512):
    m, k = x.shape
    n = w.shape[-1]
    tm, tn = _tile(m, tm_pref), _tile(n, tn_pref)
    return pl.pallas_call(
        _matmul_residual_kernel,
        grid=(m // tm, n // tn),
        in_specs=[pl.BlockSpec((tm, k), lambda i, j: (i, 0)), _w_spec(w, layer, k, tn),
                  pl.BlockSpec((tm, tn), lambda i, j: (i, j))],
        out_specs=pl.BlockSpec((tm, tn), lambda i, j: (i, j)),
        out_shape=jax.ShapeDtypeStruct((m, n), F32),
        compiler_params=_cparams(("parallel", "arbitrary")),
        name="out_proj",
    )(x, w, res)


def _mlp_kernel(h_ref, nw_ref, wu_ref, wd_ref, nnw_ref, o_ref, *rest, last):
    u_ref = rest[-1]
    j = pl.program_id(1)

    @pl.when(j == 0)
    def _():
        x = h_ref[...]
        u_ref[...] = _rms(x, nw_ref[...]).astype(BF16)
        o_ref[...] = x

    a = jnp.dot(u_ref[...], wu_ref[...], preferred_element_type=F32)
    a = jnp.square(jnp.maximum(a, 0.0)).astype(BF16)
    o_ref[...] += jnp.dot(a, wd_ref[...], preferred_element_type=F32)

    @pl.when(j == pl.num_programs(1) - 1)
    def _():
        y = _rms(o_ref[...], nnw_ref[...])
        if last:
            o_ref[...] = y
        else:
            rest[0][...] = y.astype(BF16)


def _mlp(h, norm_w, w_up, w_down, next_norm_w, last):
    m, d = h.shape
    f = w_up.shape[-1]
    tm, tf = _tile(m, 512), _tile(f, 1024)
    row = pl.BlockSpec((tm, d), lambda i, j: (i, 0))
    vec = pl.BlockSpec((1, d), lambda i, j: (0, 0))
    out_shape = [jax.ShapeDtypeStruct((m, d), F32)] + ([] if last else [jax.ShapeDtypeStruct((m, d), BF16)])
    res = pl.pallas_call(
        functools.partial(_mlp_kernel, last=last),
        grid=(m // tm, f // tf),
        in_specs=[row, vec, pl.BlockSpec((d, tf), lambda i, j: (0, j)), pl.BlockSpec((tf, d), lambda i, j: (j, 0)),
                  vec],
        out_specs=[row] * len(out_shape),
        out_shape=out_shape,
        scratch_shapes=[pltpu.VMEM((tm, d), BF16)],
        compiler_params=_cparams(("parallel", "arbitrary"), VMEM_LIMIT_LARGE),
        name="mlp",
    )(h, norm_w.reshape(1, d), w_up, w_down, next_norm_w.reshape(1, d))
    return res[0] if last else res


def _chunk_head(qs, kk, vv, b, st_ref, head, factored, kb_ref):
    c, kdim = qs.shape
    row = lax.broadcasted_iota(jnp.int32, (c, c), 0)
    col = lax.broadcasted_iota(jnp.int32, (c, c), 1)
    b_last = b[c - 1:c, :]
    st = st_ref[head]
    nt = (((1,), (1,)), ((), ()))
    if factored:
        ref_row = b[c // 2 - 1:c // 2, :]
        qd = qs * jnp.exp(b - ref_row)
        kd = kk * jnp.exp(ref_row - b)
        q_in = qd * jnp.exp(ref_row)
        k_dec = kd * jnp.exp(b_last - ref_row)
        att = lax.dot_general(qd.astype(BF16), kd.astype(BF16), nt, preferred_element_type=F32)
    else:
        q_in = qs * jnp.exp(b)
        k_dec = kk * jnp.exp(b_last - b)
        kb_ref[0, :, 0:kdim] = kk
        kb_ref[1, :, 0:kdim] = b

        def body(s, att):
            k_row = kb_ref[0, pl.ds(s, 1), 0:kdim]
            b_row = kb_ref[1, pl.ds(s, 1), 0:kdim]
            w = jnp.sum(qs * k_row * jnp.exp(jnp.minimum(b - b_row, 0.0)), axis=-1, keepdims=True)
            return jnp.where(col == s, w, att)

        att = lax.fori_loop(0, c, body, jnp.zeros((c, c), F32))
    att = jnp.where(col <= row, att, 0.0)
    o = lax.dot_general(q_in.astype(BF16), st.astype(BF16), nt, preferred_element_type=F32)
    o = o + jnp.dot(att.astype(BF16), vv.astype(BF16), preferred_element_type=F32)
    upd = lax.dot_general(vv.astype(BF16), k_dec.astype(BF16), (((0,), (0,)), ((), ())),
                          preferred_element_type=F32)
    st_ref[head] = st * jnp.exp(b_last) + upd
    return o


def _cumsum_chunks(x, nb, c):
    row = lax.broadcasted_iota(jnp.int32, (c, c), 0)
    col = lax.broadcasted_iota(jnp.int32, (c, c), 1)
    tri = jnp.where(col <= row, 1.0, 0.0).astype(F32)
    return [jnp.dot(tri, x[bi * c:(bi + 1) * c, :], preferred_element_type=F32, precision=HIGHEST)
            for bi in range(nb)]


def _chunk_is_safe(b_all):
    c = b_all[0].shape[0]
    worst = None
    for b in b_all:
        ref_row = b[c // 2 - 1:c // 2, :]
        span = jnp.maximum(-ref_row, ref_row - b[c - 1:c, :])
        worst = span if worst is None else jnp.maximum(worst, span)
    return jnp.max(worst) <= SAFE_HALF_DECAY


def _log_sigmoid(x):
    return jnp.minimum(x, 0.0) - jnp.log(1.0 + jnp.exp(-jnp.abs(x)))


def _gla_kernel(q_ref, k_ref, v_ref, g_ref, a_ref, w2_ref, b2_ref, nw_ref, o_ref, st_ref, kb_ref, *,
                heads, dk, dv):
    nb, c = q_ref.shape[0], q_ref.shape[1]

    @pl.when(pl.program_id(0) == 0)
    def _():
        st_ref[...] = jnp.zeros_like(st_ref)

    a = a_ref[...].reshape(nb * c, a_ref.shape[2])
    z = jnp.dot(a, w2_ref[...], preferred_element_type=F32, precision=HIGHEST) + b2_ref[...]
    b_all = _cumsum_chunks(_log_sigmoid(z) * (1.0 / GLA_GATE_NORMALIZER), nb, c)

    def step(factored):
        for bi in range(nb):
            for h in range(heads):
                qs = q_ref[bi, :, h * dk:(h + 1) * dk].astype(F32) * (dk ** -0.5)
                kk = k_ref[bi, :, h * dk:(h + 1) * dk].astype(F32)
                vv = v_ref[bi, :, h * dv:(h + 1) * dv]
                o = _chunk_head(qs, kk, vv, b_all[bi][:, h * dk:(h + 1) * dk], st_ref, bi * heads + h, factored,
                                kb_ref)
                g = g_ref[bi, :, h * dv:(h + 1) * dv].astype(F32)
                o = _rms(o, nw_ref[...]) * (g * _sigmoid(g))
                o_ref[bi, :, h * dv:(h + 1) * dv] = o.astype(o_ref.dtype)

    lax.cond(_chunk_is_safe(b_all), functools.partial(step, True), functools.partial(step, False))


def _chunk_specs(nb, c, arr, offs):
    def seg(name, width):
        assert offs[name] % width == 0
        blk = offs[name] // width
        return pl.BlockSpec((nb, c, width), lambda ci: (0, ci, blk))

    return seg, arr.reshape(nb, -1, arr.shape[-1])


def _gla(p_gla, a_low, w2p, b2, onorm_w, dm, offs):
    nb, c = dm.batch, dm.chunk
    hk, hv = dm.gla_heads * dm.gla_dk, dm.gla_heads * dm.gla_dv
    seg, p3 = _chunk_specs(nb, c, p_gla, offs)
    a3 = a_low.reshape(nb, dm.seq, LANES)
    const = lambda shape: pl.BlockSpec(shape, lambda ci: (0, 0))
    out = pl.pallas_call(
        functools.partial(_gla_kernel, heads=dm.gla_heads, dk=dm.gla_dk, dv=dm.gla_dv),
        grid=(dm.seq // c,),
        in_specs=[seg("gla_q", hk), seg("gla_k", hk), seg("gla_v", hv), seg("gla_g", hv),
                  pl.BlockSpec((nb, c, LANES), lambda ci: (0, ci, 0)),
                  const((LANES, hk)), const((1, hk)), const((1, dm.gla_dv))],
        out_specs=pl.BlockSpec((nb, c, hv), lambda ci: (0, ci, 0)),
        out_shape=jax.ShapeDtypeStruct((nb, dm.seq, hv), BF16),
        scratch_shapes=[pltpu.VMEM((nb * dm.gla_heads, dm.gla_dv, dm.gla_dk), F32),
                        pltpu.VMEM((2, c, dm.gla_dk), F32)],
        compiler_params=_cparams(("arbitrary",)),
        name="gla",
    )(p3, p3, p3, p3, a3, w2p, b2.reshape(1, hk), onorm_w.reshape(1, dm.gla_dv))
    return out.reshape(nb * dm.seq, hv)


def _hgrn_kernel(q_ref, f_ref, i_ref, g_ref, lbp_ref, nw_ref, o_ref, st_ref, kb_ref, *, layer, heads, dk, dv):
    nb, c = q_ref.shape[0], q_ref.shape[1]

    @pl.when(pl.program_id(0) == 0)
    def _():
        st_ref[...] = jnp.zeros_like(st_ref)

    lbp = lbp_ref[...]
    e = jnp.exp(lbp - jnp.max(lbp, axis=0, keepdims=True))
    p = e / jnp.sum(e, axis=0, keepdims=True)
    lb = jnp.zeros_like(p[0:1, :])
    for j in range(1, layer + 1):
        lb = lb + p[j:j + 1, :]
    sig_f = _sigmoid(f_ref[...].reshape(nb * c, f_ref.shape[2]).astype(F32))
    f_gate = lb + (1.0 - lb) * sig_f
    b_all = _cumsum_chunks(jnp.log(jnp.maximum(f_gate, F_MIN)), nb, c)
    k_all = (1.0 - lb) * (1.0 - sig_f)

    def step(factored):
        for bi in range(nb):
            for h in range(heads):
                q = q_ref[bi, :, h * dk:(h + 1) * dk].astype(F32)
                qs = q * _sigmoid(q) * (dk ** -0.5)
                vv = i_ref[bi, :, h * dv:(h + 1) * dv]
                o = _chunk_head(qs, k_all[bi * c:(bi + 1) * c, h * dk:(h + 1) * dk], vv,
                                b_all[bi][:, h * dk:(h + 1) * dk], st_ref, bi * heads + h, factored, kb_ref)
                g = g_ref[bi, :, h * dv:(h + 1) * dv].astype(F32)
                o = _rms(o, nw_ref[...]) * _sigmoid(g)
                o_ref[bi, :, h * dv:(h + 1) * dv] = o.astype(o_ref.dtype)

    lax.cond(_chunk_is_safe(b_all), functools.partial(step, True), functools.partial(step, False))


def _hgrn(p_hg, lower_bounds, onorm_w, layer, dm, offs):
    nb, c = dm.batch, dm.chunk
    hk, hv = dm.hg_heads * dm.hg_dk, dm.hg_heads * dm.hg_dv
    seg, p3 = _chunk_specs(nb, c, p_hg, offs)
    const = lambda shape: pl.BlockSpec(shape, lambda ci: (0, 0))
    out = pl.pallas_call(
        functools.partial(_hgrn_kernel, layer=layer, heads=dm.hg_heads, dk=dm.hg_dk, dv=dm.hg_dv),
        grid=(dm.seq // c,),
        in_specs=[seg("hg_q", hk), seg("hg_f", hk), seg("hg_i", hv), seg("hg_g", hv),
                  const((dm.depth, hk)), const((1, dm.hg_dv))],
        out_specs=pl.BlockSpec((nb, c, hv), lambda ci: (0, ci, 0)),
        out_shape=jax.ShapeDtypeStruct((nb, dm.seq, hv), BF16),
        scratch_shapes=[pltpu.VMEM((nb * dm.hg_heads, dm.hg_dv, dm.hg_dk), F32),
                        pltpu.VMEM((2, c, dm.hg_dk), F32)],
        compiler_params=_cparams(("arbitrary",)),
        name="hgrn",
    )(p3, p3, p3, p3, lower_bounds, onorm_w.reshape(1, dm.hg_dv))
    return out.reshape(nb * dm.seq, hv)


def _rope_table_kernel(pos_ref, f_ref, sgn_ref, cos_ref, sin_ref):
    ang = pos_ref[...].astype(F32) * f_ref[...]
    cos_ref[...] = jnp.cos(ang)
    sin_ref[...] = jnp.sin(ang) * sgn_ref[...]


def _rope_tables(pos_col, inv_freq_lanes, sign_lanes):
    m = pos_col.shape[0]
    tm = _tile(m, 512)
    lane = pl.BlockSpec((1, LANES), lambda i: (0, 0))
    tab = pl.BlockSpec((tm, LANES), lambda i: (i, 0))
    return pl.pallas_call(
        _rope_table_kernel,
        grid=(m // tm,),
        in_specs=[pl.BlockSpec((tm, 1), lambda i: (i, 0)), lane, lane],
        out_specs=[tab, tab],
        out_shape=[jax.ShapeDtypeStruct((m, LANES), F32)] * 2,
        compiler_params=_cparams(("parallel",)),
        name="rope_tables",
    )(pos_col, inv_freq_lanes, sign_lanes)


def _swap_halves(x, half):
    n = x.shape[-1]
    if 2 * half == n:
        return pltpu.roll(x, half, axis=1)
    lane = lax.broadcasted_iota(jnp.int32, x.shape, 1)
    return jnp.where((lane & half) == 0, pltpu.roll(x, n - half, axis=1), pltpu.roll(x, half, axis=1))


def _rope_epilogue(y, cos_ref, sin_ref, o_ref, *, half, scale, copy_groups=()):
    c, s = cos_ref[...], sin_ref[...]
    for g in range(y.shape[1] // LANES):
        yg = y[:, g * LANES:(g + 1) * LANES]
        if g not in copy_groups:
            yg = (yg * c + _swap_halves(yg, half) * s) * scale
        o_ref[:, g * LANES:(g + 1) * LANES] = yg.astype(o_ref.dtype)


def _idx_epilogue(y, cos_ref, sin_ref, iq_ref, ik_ref, iw_ref, *, idx_heads, idx_dim):
    c, s = cos_ref[...], sin_ref[...]
    nq = idx_heads * idx_dim // LANES
    for g in range(nq):
        yg = y[:, g * LANES:(g + 1) * LANES]
        r = (yg * c + _swap_halves(yg, idx_dim // 2) * s) * (idx_dim ** -0.5)
        iq_ref[:, g * LANES:(g + 1) * LANES] = r.astype(iq_ref.dtype)
    x = y[:, nq * LANES:(nq + 1) * LANES]
    r = x * c + _swap_halves(x, idx_dim // 2) * s
    lane = lax.broadcasted_iota(jnp.int32, x.shape, 1)
    ik_ref[...] = jnp.where(lane < idx_dim, r, pltpu.roll(r, idx_dim, axis=1)).astype(ik_ref.dtype)
    iw_ref[...] = jnp.where(lane < idx_heads, pltpu.roll(x, LANES - idx_dim, axis=1), 0.0) * (idx_heads ** -0.5)


def _ind(mask):
    return jnp.where(mask, 1.0, 0.0)


def _row_sum(x):
    return jnp.sum(x, axis=-1, keepdims=True)


def _float_of_ordered(u):
    k = u ^ jnp.int32(INT_MIN)
    return pltpu.bitcast(jnp.where(k < 0, k ^ jnp.int32(0x7FFFFFFF), k), F32)


def _dsa_kernel(q_ref, iq_ref, iw_ref, k_ref, v_ref, ik_ref, o_ref, *, heads, dim, idx_heads, idx_dim, topk,
                q_start, n_hidden):
    nbb, tq = q_ref.shape[0], q_ref.shape[1]
    tk = k_ref.shape[1]
    rows = nbb * tq
    nt = (((1,), (1,)), ((), ()))
    stack = lambda parts: parts[0] if len(parts) == 1 else jnp.concatenate(parts, axis=0)
    lane = lax.broadcasted_iota(jnp.int32, (tq, LANES), 1)
    scores = []
    for bi in range(nbb):
        ik = ik_ref[bi]
        score = jnp.zeros((tq, tk), F32)
        for h in range(idx_heads):
            g = (h * idx_dim) // LANES
            lo = (h * idx_dim) % LANES
            x = iq_ref[bi, :, g * LANES:(g + 1) * LANES]
            x = jnp.where(lane >= lo, jnp.where(lane < lo + idx_dim, x, jnp.zeros_like(x)), jnp.zeros_like(x))
            rel = lax.dot_general(x, ik, nt, preferred_element_type=F32)
            score = score + iw_ref[bi, :, h:h + 1] * jnp.maximum(rel, 0.0)
        scores.append(score)
    qpos = stack([q_start + lax.broadcasted_iota(jnp.int32, (tq, 1), 0)] * nbb)
    kpos = lax.broadcasted_iota(jnp.int32, (1, tk), 1)
    allowed = kpos <= qpos
    score = jnp.where(allowed, stack(scores), MASK_VALUE)
    hidden = float(n_hidden)

    def thr_body(it, prefix):
        trial = prefix | lax.shift_left(jnp.int32(1), 31 - it)
        cand = _float_of_ordered(trial)
        cnt = _row_sum(_ind(score >= cand)) + jnp.where(MASK_VALUE >= cand, hidden, 0.0)
        return jnp.where(cnt >= topk, trial, prefix)

    thr = _float_of_ordered(lax.fori_loop(0, 32, thr_body, jnp.zeros((rows, 1), jnp.int32)))
    above = _ind(score > thr)
    need = topk - _row_sum(above) - jnp.where(MASK_VALUE > thr, hidden, 0.0)
    tie = jnp.where(allowed, _ind(score == thr), 0.0)

    def all_ties():
        return jnp.where(allowed, _ind(score >= thr), 0.0)

    def ordered_ties():
        nbits = (tk - 1).bit_length()

        def tie_body(it, j):
            trial = j | lax.shift_left(jnp.int32(1), nbits - 1 - it)
            cnt = _row_sum(jnp.where(kpos < trial, tie, 0.0))
            return jnp.where(cnt < need, trial, j)

        j_last = lax.fori_loop(0, nbits, tie_body, jnp.zeros((rows, 1), jnp.int32))
        return jnp.where(allowed, above, 0.0) + jnp.where(kpos <= j_last, tie, 0.0)

    selected = lax.cond(jnp.max(_row_sum(tie) - need) > 0.0, ordered_ties, all_ties)

    for bi in range(nbb):
        valid = selected[bi * tq:(bi + 1) * tq, :] > 0.0
        k = k_ref[bi]
        v_ones = jnp.concatenate([v_ref[bi], jnp.ones((tk, dim), BF16)], axis=1)
        for h in range(heads):
            s = lax.dot_general(q_ref[bi, :, h * dim:(h + 1) * dim], k, nt, preferred_element_type=F32)
            s = jnp.where(valid, s, MASK_VALUE)
            p = jnp.exp(s - jnp.max(s, axis=-1, keepdims=True)).astype(BF16)
            o = jnp.dot(p, v_ones, preferred_element_type=F32)
            o_ref[bi, :, h * dim:(h + 1) * dim] = (o[:, 0:dim] / o[:, dim:dim + 1]).astype(o_ref.dtype)


def _with_carried_output(kern, q_ref, iq_ref, iw_ref, k_ref, v_ref, ik_ref, carried_ref, o_ref):
    del carried_ref
    kern(q_ref, iq_ref, iw_ref, k_ref, v_ref, ik_ref, o_ref)


def _zero_fill_kernel(o_ref):
    o_ref[...] = jnp.zeros_like(o_ref)


def _dsa(q_rot, kv, iq_rot, ik_rot, iw, dm):
    tq, t, nb = dm.q_block, dm.seq, dm.batch
    hd = dm.dsa_heads * dm.dsa_dim
    assert t % tq == 0
    per_batch = lambda a: a.reshape(nb, t, a.shape[-1])
    q3, kv3, iq3, ik3, iw3 = map(per_batch, (q_rot, kv, iq_rot, ik_rot, iw))
    out = pl.pallas_call(
        _zero_fill_kernel,
        grid=(nb,),
        out_specs=pl.BlockSpec((None, t, hd), lambda bi: (bi, 0, 0)),
        out_shape=jax.ShapeDtypeStruct((nb, t, hd), BF16),
        compiler_params=_cparams(("parallel",)),
        name="dsa_out_init",
    )()
    for g in range(t // tq):
        tk = (g + 1) * tq
        nbb = 2 if (2 * tk <= t and nb % 2 == 0) else 1
        qrow = lambda w, g=g, nbb=nbb: pl.BlockSpec((nbb, tq, w), lambda bi: (bi, g, 0))
        krow = lambda w, blk=0, tk=tk, nbb=nbb: pl.BlockSpec((nbb, tk, w), lambda bi: (bi, 0, blk))
        kern = functools.partial(_dsa_kernel, heads=dm.dsa_heads, dim=dm.dsa_dim, idx_heads=dm.idx_heads,
                                 idx_dim=dm.idx_dim, topk=dm.topk, q_start=g * tq, n_hidden=t - tk)
        out = pl.pallas_call(
            functools.partial(_with_carried_output, kern),
            grid=(nb // nbb,),
            in_specs=[qrow(hd), qrow(iq3.shape[-1]), qrow(LANES), krow(dm.dsa_dim), krow(dm.dsa_dim, 1),
                      krow(LANES), pl.BlockSpec(memory_space=pl.ANY)],
            out_specs=qrow(hd),
            out_shape=jax.ShapeDtypeStruct((nb, t, hd), BF16),
            input_output_aliases={6: 0},
            compiler_params=_cparams(("parallel",)),
            name=f"dsa_q{g}",
        )(q3, iq3, iw3, kv3, kv3, ik3, out)
    return out.reshape(nb * t, hd)


def _forward(dm, x, positions, norm1_w, w_in, gla_gate_w2, gla_gate_b, gla_onorm_w, hgrn_lower_bounds,
             hgrn_onorm_w, w_branch_gla, w_branch_dsa, w_branch_hgrn, w_out, norm2_w, w_mlp_up, w_mlp_down,
             final_norm_w):
    m = dm.batch * dm.seq
    h = x.reshape(m, dm.d_model)
    src, ((gla_start, gla_width, gla_offs), (dsa_start, dsa_width, dsa_offs), (hg_start, hg_width, hg_offs)) = \
        _layout(dm)
    assert dm.dsa_dim == LANES and 2 * dm.idx_dim == LANES and dm.idx_heads <= dm.idx_dim
    idx_start = src["idx_q"][0]
    idx_width = -(-(src["idx_w"][0] + src["idx_w"][1] - idx_start) // LANES) * LANES
    w_t = jnp.swapaxes(w_in, 1, 2)

    def lanes(d):
        inv = ROPE_THETA ** (-jnp.arange(0, d, 2, dtype=F32) / d)
        reps = LANES // d
        f = jnp.tile(jnp.concatenate([inv, inv]), reps).reshape(1, LANES)
        sgn = jnp.tile(jnp.concatenate([-jnp.ones(d // 2, F32), jnp.ones(d // 2, F32)]), reps).reshape(1, LANES)
        return f, sgn

    pos_col = positions.reshape(m, 1)
    tabs = _rope_tables(pos_col, *lanes(dm.dsa_dim)) + _rope_tables(pos_col, *lanes(dm.idx_dim))

    u = _rmsnorm(h, norm1_w[0], BF16)
    for l in range(dm.depth):
        w2p = jnp.concatenate(
            [gla_gate_w2[l], jnp.zeros((LANES - dm.gla_rank, gla_gate_w2.shape[2]), F32)], axis=0)
        p_gla = _proj(u, w_t, l, gla_start, gla_width, [(None, BF16)], name="in_proj_gla")
        p_hg, w_up_bf16, w_down_bf16 = _proj(u, w_t, l, hg_start, hg_width, [(None, BF16)], vmem=VMEM_LIMIT_LARGE,
                                             casts=((w_mlp_up, 0), (w_mlp_down, 1)), name="in_proj_hg")
        a_low = _proj(u, w_t, l, src["gla_a"][0], LANES, [(LANES, F32)], name="in_proj_gate")
        hd = dm.dsa_heads * dm.dsa_dim
        q_rot = _proj(u, w_t, l, dsa_start, hd, [(None, BF16)], tables=tabs[:2], tm_pref=1024, name="in_proj_dsa_q",
                      epilogue=functools.partial(_rope_epilogue, half=dm.dsa_dim // 2, scale=dm.dsa_dim ** -0.5))
        kv = _proj(u, w_t, l, dsa_start + dsa_offs["dsa_k"], 2 * dm.dsa_dim, [(2 * dm.dsa_dim, BF16)],
                   tables=tabs[:2], tm_pref=1024, tn_pref=2 * dm.dsa_dim, name="in_proj_dsa_kv",
                   epilogue=functools.partial(_rope_epilogue, half=dm.dsa_dim // 2, scale=1.0, copy_groups=(1,)))
        iq_rot, ik_rot, iw = _proj(
            u, w_t, l, idx_start, idx_width, [(dm.idx_heads * dm.idx_dim, BF16), (LANES, BF16), (LANES, F32)],
            tables=tabs[2:], tm_pref=1024, tn_pref=idx_width, name="in_proj_idx",
            epilogue=functools.partial(_idx_epilogue, idx_heads=dm.idx_heads, idx_dim=dm.idx_dim))
        o_gla = _gla(p_gla, a_low, w2p, gla_gate_b[l], gla_onorm_w[l], dm, gla_offs)
        o_hg = _hgrn(p_hg, hgrn_lower_bounds, hgrn_onorm_w[l], l, dm, hg_offs)
        o_dsa = _dsa(q_rot, kv, iq_rot, ik_rot, iw, dm)
        merged = _merge(o_gla, o_dsa, o_hg, w_branch_gla, w_branch_dsa, w_branch_hgrn, l, p_hg,
                        (hg_offs["gate_a"], hg_offs["gate_b"], hg_offs["gate_c"]), dm.d_model)
        h = _matmul_residual(merged, w_out, l, h)
        last = l == dm.depth - 1
        res = _mlp(h, norm2_w[l], w_up_bf16, w_down_bf16, final_norm_w if last else norm1_w[l + 1], last)
        if not last:
            h, u = res
    return res.reshape(dm.batch, dm.seq, dm.d_model)


def kernel(x, positions, norm1_w, w_in, gla_gate_w2, gla_gate_b, gla_onorm_w, hgrn_lower_bounds, hgrn_onorm_w,
           w_branch_gla, w_branch_dsa, w_branch_hgrn, w_out, norm2_w, w_mlp_up, w_mlp_down, final_norm_w):
    return _forward(_prod_dims(), x, positions, norm1_w, w_in, gla_gate_w2, gla_gate_b, gla_onorm_w,
                    hgrn_lower_bounds, hgrn_onorm_w, w_branch_gla, w_branch_dsa, w_branch_hgrn, w_out, norm2_w,
                    w_mlp_up, w_mlp_down, final_norm_w)
```

```python
import functools
from typing import NamedTuple

import jax
import jax.numpy as jnp
from jax import lax
from jax.experimental import pallas as pl
from jax.experimental.pallas import tpu as pltpu

F32 = jnp.float32
BF16 = jnp.bfloat16
HIGHEST = lax.Precision.HIGHEST

ROPE_THETA = 10000.0
NORM_EPS = 1e-6
MASK_VALUE = -1e30
F_MIN = 1e-12
GLA_GATE_NORMALIZER = 16.0
LANES = 128
SUBLANES = 8
ROW_BLOCK = 128
INT_MIN = -2 ** 31
SAFE_HALF_DECAY = 80.0
VMEM_LIMIT = 48 * 1024 * 1024
VMEM_LIMIT_LARGE = 56 * 1024 * 1024


class _Dims(NamedTuple):
    d_model: int
    batch: int
    seq: int
    depth: int
    gla_heads: int
    gla_dk: int
    gla_dv: int
    gla_rank: int
    dsa_heads: int
    dsa_dim: int
    idx_heads: int
    idx_dim: int
    topk: int
    hg_heads: int
    hg_dk: int
    hg_dv: int
    d_ff: int
    chunk: int
    q_block: int


def _prod_dims():
    d = 2048
    return _Dims(d_model=d, batch=4, seq=2048, depth=2,
                 gla_heads=4, gla_dk=d // 2 // 4, gla_dv=d // 4, gla_rank=16,
                 dsa_heads=16, dsa_dim=128, idx_heads=8, idx_dim=64, topk=min(256, 2048 // 4),
                 hg_heads=d // 128, hg_dk=128, hg_dv=128, d_ff=4 * d, chunk=128, q_block=256)


def _in_sizes(dm):
    return (dm.gla_heads * dm.gla_dk, dm.gla_heads * dm.gla_dk, dm.gla_heads * dm.gla_dv,
            dm.gla_heads * dm.gla_dv, dm.gla_rank,
            dm.dsa_heads * dm.dsa_dim, dm.dsa_dim, dm.dsa_dim, dm.idx_heads * dm.idx_dim, dm.idx_dim,
            dm.idx_heads,
            dm.hg_heads * dm.hg_dk, dm.hg_heads * dm.hg_dk, dm.hg_heads * dm.hg_dv, dm.hg_heads * dm.hg_dv,
            dm.d_model, dm.d_model, dm.d_model)


_IN_NAMES = ("gla_q", "gla_k", "gla_v", "gla_g", "gla_a", "dsa_q", "dsa_k", "dsa_v", "idx_q", "idx_k",
             "idx_w", "hg_q", "hg_f", "hg_i", "hg_g", "gate_a", "gate_b", "gate_c")
_GROUPS = (("gla_q", "gla_k", "gla_v", "gla_g"), ("dsa_q", "dsa_k", "dsa_v"),
           ("hg_q", "hg_f", "hg_i", "hg_g", "gate_a", "gate_b", "gate_c"))


def _tile(n, pref):
    t = min(n, pref)
    while n % t:
        t //= 2
    return t


def _layout(dm):
    sizes = dict(zip(_IN_NAMES, _in_sizes(dm)))
    src, off = {}, 0
    for name in _IN_NAMES:
        src[name] = (off, sizes[name])
        off += sizes[name]
    groups = []
    for names in _GROUPS:
        start = src[names[0]][0]
        offs = {n: src[n][0] - start for n in names}
        width = src[names[-1]][0] + src[names[-1]][1] - start
        for n in names:
            assert offs[n] % min(sizes[n], 2 * LANES) == 0, n
        groups.append((start, width, offs))
    return src, groups


def _cparams(sem, vmem=VMEM_LIMIT):
    return pltpu.CompilerParams(dimension_semantics=sem, vmem_limit_bytes=vmem)


def _rms(x, w):
    ms = jnp.mean(x * x, axis=-1, keepdims=True)
    return x * lax.rsqrt(ms + NORM_EPS) * w


def _rmsnorm_kernel(x_ref, w_ref, o_ref):
    o_ref[...] = _rms(x_ref[...], w_ref[...]).astype(o_ref.dtype)


def _rmsnorm(x, w, out_dtype):
    m, d = x.shape
    tm = _tile(m, 512)
    return pl.pallas_call(
        _rmsnorm_kernel,
        grid=(m // tm,),
        in_specs=[pl.BlockSpec((tm, d), lambda i: (i, 0)), pl.BlockSpec((1, d), lambda i: (0, 0))],
        out_specs=pl.BlockSpec((tm, d), lambda i: (i, 0)),
        out_shape=jax.ShapeDtypeStruct((m, d), out_dtype),
        compiler_params=_cparams(("parallel",)),
        name="rmsnorm",
    )(x, w.reshape(1, d))


def _w_spec(w, layer, k, tn):
    if w.ndim == 3:
        return pl.BlockSpec((None, k, tn), lambda i, j: (layer, 0, j))
    return pl.BlockSpec((k, tn), lambda i, j: (0, j))


def _store(y, o_ref):
    o_ref[...] = y.astype(o_ref.dtype)


def _proj_kernel(x_ref, *refs, shift, n_w, n_tab, n_cast, tn, epilogue):
    w_refs, rest = refs[:n_w], refs[n_w:]
    tabs, cast_in, rest = rest[:n_tab], rest[n_tab:n_tab + n_cast], rest[n_tab + n_cast:]
    outs, cast_out = rest[:len(rest) - n_cast], rest[len(rest) - n_cast:]
    if n_w == 1:
        w = w_refs[0][...]
    else:
        w = jnp.concatenate([r[...] for r in w_refs], axis=0)[shift:shift + tn, :]
    y = lax.dot_general(x_ref[...], w.astype(BF16), (((1,), (1,)), ((), ())), preferred_element_type=F32)
    epilogue(y, *tabs, *outs)
    for src_ref, dst_ref in zip(cast_in, cast_out):
        dst_ref[...] = src_ref[...].astype(dst_ref.dtype)


def _cast_job(w, layer, gm, gn, axis_i):
    dims = list(w.shape[1:])
    other = dims[1 - axis_i]
    nj = max(d for d in range(1, gn + 1) if other % d == 0 and (other // d) % LANES == 0)
    blk = [0, 0]
    blk[axis_i], blk[1 - axis_i] = dims[axis_i] // gm, other // nj
    assert dims[axis_i] % gm == 0 and blk[axis_i] % LANES == 0

    def imap(i, j):
        jj = jnp.minimum(j, nj - 1)
        return (i, jj) if axis_i == 0 else (jj, i)

    return (pl.BlockSpec((None, *blk), lambda i, j: (layer, *imap(i, j))), pl.BlockSpec(tuple(blk), imap),
            jax.ShapeDtypeStruct(tuple(dims), BF16))


def _proj(x, w_t, layer, start, n, outs, epilogue=_store, tables=(), casts=(), tm_pref=2048, tn_pref=512,
          vmem=VMEM_LIMIT, name="proj"):
    m, k = x.shape
    tm, tn = _tile(m, tm_pref), _tile(n, tn_pref)
    outs = [(tn if cols is None else cols, dtype) for cols, dtype in outs]
    jobs = [_cast_job(w, layer, m // tm, n // tn, axis_i) for w, axis_i in casts]
    if start % tn == 0:
        shift = 0
        specs = [pl.BlockSpec((None, tn, k), lambda i, j: (layer, start // tn + j, 0))]
    else:
        assert tn % ROW_BLOCK == 0 and start % SUBLANES == 0
        base, shift = divmod(start, ROW_BLOCK)
        per_tile = tn // ROW_BLOCK
        blk = lambda b: pl.BlockSpec((None, ROW_BLOCK, k), lambda i, j: (layer, base + j * per_tile + b, 0))
        specs = [blk(b) for b in range(per_tile + 1)]
    res = pl.pallas_call(
        functools.partial(_proj_kernel, shift=shift, n_w=len(specs), n_tab=len(tables), n_cast=len(jobs), tn=tn,
                          epilogue=epilogue),
        grid=(m // tm, n // tn),
        in_specs=[pl.BlockSpec((tm, k), lambda i, j: (i, 0))] + specs
                 + [pl.BlockSpec((tm, LANES), lambda i, j: (i, 0)) for _ in tables] + [job[0] for job in jobs],
        out_specs=[pl.BlockSpec((tm, cols), lambda i, j: (i, j)) for cols, _ in outs] + [job[1] for job in jobs],
        out_shape=[jax.ShapeDtypeStruct((m, cols * (n // tn)), dtype) for cols, dtype in outs]
                  + [job[2] for job in jobs],
        compiler_params=_cparams(("parallel", "arbitrary"), vmem),
        name=name,
    )(x, *([w_t] * len(specs)), *tables, *[w for w, _ in casts])
    return res[0] if len(res) == 1 else res


def _sigmoid(x):
    return 1.0 / (1.0 + jnp.exp(-x))


def _merge_kernel(oa_ref, ob_ref, oc_ref, wa_ref, wb_ref, wc_ref, ga_ref, gb_ref, gc_ref, o_ref):
    def branch(o_ref_, w_ref_, g_ref_):
        y = jnp.dot(o_ref_[...], w_ref_[...].astype(BF16), preferred_element_type=F32)
        return _sigmoid(g_ref_[...].astype(F32)) * y

    acc = branch(oa_ref, wa_ref, ga_ref) + branch(ob_ref, wb_ref, gb_ref) + branch(oc_ref, wc_ref, gc_ref)
    o_ref[...] = acc.astype(o_ref.dtype)


def _merge(o_gla, o_dsa, o_hg, w_gla, w_dsa, w_hg, layer, gates, gate_offs, d_model):
    m = o_gla.shape[0]
    tm, tn = _tile(m, 1024), _tile(d_model, 256)
    o_spec = lambda a: pl.BlockSpec((tm, a.shape[1]), lambda i, j: (i, 0))

    def g_spec(off):
        assert off % tn == 0
        return pl.BlockSpec((tm, tn), lambda i, j: (i, off // tn + j))

    return pl.pallas_call(
        _merge_kernel,
        grid=(m // tm, d_model // tn),
        in_specs=[o_spec(o_gla), o_spec(o_dsa), o_spec(o_hg),
                  _w_spec(w_gla, layer, w_gla.shape[-2], tn), _w_spec(w_dsa, layer, w_dsa.shape[-2], tn),
                  _w_spec(w_hg, layer, w_hg.shape[-2], tn),
                  g_spec(gate_offs[0]), g_spec(gate_offs[1]), g_spec(gate_offs[2])],
        out_specs=pl.BlockSpec((tm, tn), lambda i, j: (i, j)),
        out_shape=jax.ShapeDtypeStruct((m, d_model), BF16),
        compiler_params=_cparams(("parallel", "arbitrary"), VMEM_LIMIT_LARGE),
        name="merge",
    )(o_gla, o_dsa, o_hg, w_gla, w_dsa, w_hg, gates, gates, gates)


def _matmul_residual_kernel(x_ref, w_ref, r_ref, o_ref):
    o_ref[...] = r_ref[...] + jnp.dot(x_ref[...], w_ref[...].astype(BF16), preferred_element_type=F32)


def _matmul_residual(x, w, layer, res, tm_pref=1024, tn_pref=512):
    m, k = x.shape
    n = w.shape[-1]
    tm, tn = _tile(m, tm_pref), _tile(n, tn_pref)
    return pl.pallas_call(
        _matmul_residual_kernel,
        grid=(m // tm, n // tn),
        in_specs=[pl.BlockSpec((tm, k), lambda i, j: (i, 0)), _w_spec(w, layer, k, tn),
                  pl.BlockSpec((tm, tn), lambda i, j: (i, j))],
        out_specs=pl.BlockSpec((tm, tn), lambda i, j: (i, j)),
        out_shape=jax.ShapeDtypeStruct((m, n), F32),
        compiler_params=_cparams(("parallel", "arbitrary")),
        name="out_proj",
    )(x, w, res)


def _mlp_kernel(h_ref, nw_ref, wu_ref, wd_ref, nnw_ref, o_ref, *rest, last):
    u_ref = rest[-1]
    j = pl.program_id(1)

    @pl.when(j == 0)
    def _():
        x = h_ref[...]
        u_ref[...] = _rms(x, nw_ref[...]).astype(BF16)
        o_ref[...] = x

    a = jnp.dot(u_ref[...], wu_ref[...], preferred_element_type=F32)
    a = jnp.square(jnp.maximum(a, 0.0)).astype(BF16)
    o_ref[...] += jnp.dot(a, wd_ref[...], preferred_element_type=F32)

    @pl.when(j == pl.num_programs(1) - 1)
    def _():
        y = _rms(o_ref[...], nnw_ref[...])
        if last:
            o_ref[...] = y
        else:
            rest[0][...] = y.astype(BF16)


def _mlp(h, norm_w, w_up, w_down, next_norm_w, last):
    m, d = h.shape
    f = w_up.shape[-1]
    tm, tf = _tile(m, 512), _tile(f, 1024)
    row = pl.BlockSpec((tm, d), lambda i, j: (i, 0))
    vec = pl.BlockSpec((1, d), lambda i, j: (0, 0))
    out_shape = [jax.ShapeDtypeStruct((m, d), F32)] + ([] if last else [jax.ShapeDtypeStruct((m, d), BF16)])
    res = pl.pallas_call(
        functools.partial(_mlp_kernel, last=last),
        grid=(m // tm, f // tf),
        in_specs=[row, vec, pl.BlockSpec((d, tf), lambda i, j: (0, j)), pl.BlockSpec((tf, d), lambda i, j: (j, 0)),
                  vec],
        out_specs=[row] * len(out_shape),
        out_shape=out_shape,
        scratch_shapes=[pltpu.VMEM((tm, d), BF16)],
        compiler_params=_cparams(("parallel", "arbitrary"), VMEM_LIMIT_LARGE),
        name="mlp",
    )(h, norm_w.reshape(1, d), w_up, w_down, next_norm_w.reshape(1, d))
    return res[0] if last else res


def _chunk_head(qs, kk, vv, b, st_ref, head, factored, kb_ref):
    c, kdim = qs.shape
    row = lax.broadcasted_iota(jnp.int32, (c, c), 0)
    col = lax.broadcasted_iota(jnp.int32, (c, c), 1)
    b_last = b[c - 1:c, :]
    st = st_ref[head]
    nt = (((1,), (1,)), ((), ()))
    if factored:
        ref_row = b[c // 2 - 1:c // 2, :]
        qd = qs * jnp.exp(b - ref_row)
        kd = kk * jnp.exp(ref_row - b)
        q_in = qd * jnp.exp(ref_row)
        k_dec = kd * jnp.exp(b_last - ref_row)
        att = lax.dot_general(qd.astype(BF16), kd.astype(BF16), nt, preferred_element_type=F32)
    else:
        q_in = qs * jnp.exp(b)
        k_dec = kk * jnp.exp(b_last - b)
        kb_ref[0, :, 0:kdim] = kk
        kb_ref[1, :, 0:kdim] = b

        def body(s, att):
            k_row = kb_ref[0, pl.ds(s, 1), 0:kdim]
            b_row = kb_ref[1, pl.ds(s, 1), 0:kdim]
            w = jnp.sum(qs * k_row * jnp.exp(jnp.minimum(b - b_row, 0.0)), axis=-1, keepdims=True)
            return jnp.where(col == s, w, att)

        att = lax.fori_loop(0, c, body, jnp.zeros((c, c), F32))
    att = jnp.where(col <= row, att, 0.0)
    o = lax.dot_general(q_in.astype(BF16), st.astype(BF16), nt, preferred_element_type=F32)
    o = o + jnp.dot(att.astype(BF16), vv.astype(BF16), preferred_element_type=F32)
    upd = lax.dot_general(vv.astype(BF16), k_dec.astype(BF16), (((0,), (0,)), ((), ())),
                          preferred_element_type=F32)
    st_ref[head] = st * jnp.exp(b_last) + upd
    return o


def _cumsum_chunks(x, nb, c):
    row = lax.broadcasted_iota(jnp.int32, (c, c), 0)
    col = lax.broadcasted_iota(jnp.int32, (c, c), 1)
    tri = jnp.where(col <= row, 1.0, 0.0).astype(F32)
    return [jnp.dot(tri, x[bi * c:(bi + 1) * c, :], preferred_element_type=F32, precision=HIGHEST)
            for bi in range(nb)]


def _chunk_is_safe(b_all):
    c = b_all[0].shape[0]
    worst = None
    for b in b_all:
        ref_row = b[c // 2 - 1:c // 2, :]
        span = jnp.maximum(-ref_row, ref_row - b[c - 1:c, :])
        worst = span if worst is None else jnp.maximum(worst, span)
    return jnp.max(worst) <= SAFE_HALF_DECAY


def _log_sigmoid(x):
    return jnp.minimum(x, 0.0) - jnp.log(1.0 + jnp.exp(-jnp.abs(x)))


def _gla_kernel(q_ref, k_ref, v_ref, g_ref, a_ref, w2_ref, b2_ref, nw_ref, o_ref, st_ref, kb_ref, *,
                heads, dk, dv):
    nb, c = q_ref.shape[0], q_ref.shape[1]

    @pl.when(pl.program_id(0) == 0)
    def _():
        st_ref[...] = jnp.zeros_like(st_ref)

    a = a_ref[...].reshape(nb * c, a_ref.shape[2])
    z = jnp.dot(a, w2_ref[...], preferred_element_type=F32, precision=HIGHEST) + b2_ref[...]
    b_all = _cumsum_chunks(_log_sigmoid(z) * (1.0 / GLA_GATE_NORMALIZER), nb, c)

    def step(factored):
        for bi in range(nb):
            for h in range(heads):
                qs = q_ref[bi, :, h * dk:(h + 1) * dk].astype(F32) * (dk ** -0.5)
                kk = k_ref[bi, :, h * dk:(h + 1) * dk].astype(F32)
                vv = v_ref[bi, :, h * dv:(h + 1) * dv]
                o = _chunk_head(qs, kk, vv, b_all[bi][:, h * dk:(h + 1) * dk], st_ref, bi * heads + h, factored,
                                kb_ref)
                g = g_ref[bi, :, h * dv:(h + 1) * dv].astype(F32)
                o = _rms(o, nw_ref[...]) * (g * _sigmoid(g))
                o_ref[bi, :, h * dv:(h + 1) * dv] = o.astype(o_ref.dtype)

    lax.cond(_chunk_is_safe(b_all), functools.partial(step, True), functools.partial(step, False))


def _chunk_specs(nb, c, arr, offs):
    def seg(name, width):
        assert offs[name] % width == 0
        blk = offs[name] // width
        return pl.BlockSpec((nb, c, width), lambda ci: (0, ci, blk))

    return seg, arr.reshape(nb, -1, arr.shape[-1])


def _gla(p_gla, a_low, w2p, b2, onorm_w, dm, offs):
    nb, c = dm.batch, dm.chunk
    hk, hv = dm.gla_heads * dm.gla_dk, dm.gla_heads * dm.gla_dv
    seg, p3 = _chunk_specs(nb, c, p_gla, offs)
    a3 = a_low.reshape(nb, dm.seq, LANES)
    const = lambda shape: pl.BlockSpec(shape, lambda ci: (0, 0))
    out = pl.pallas_call(
        functools.partial(_gla_kernel, heads=dm.gla_heads, dk=dm.gla_dk, dv=dm.gla_dv),
        grid=(dm.seq // c,),
        in_specs=[seg("gla_q", hk), seg("gla_k", hk), seg("gla_v", hv), seg("gla_g", hv),
                  pl.BlockSpec((nb, c, LANES), lambda ci: (0, ci, 0)),
                  const((LANES, hk)), const((1, hk)), const((1, dm.gla_dv))],
        out_specs=pl.BlockSpec((nb, c, hv), lambda ci: (0, ci, 0)),
        out_shape=jax.ShapeDtypeStruct((nb, dm.seq, hv), BF16),
        scratch_shapes=[pltpu.VMEM((nb * dm.gla_heads, dm.gla_dv, dm.gla_dk), F32),
                        pltpu.VMEM((2, c, dm.gla_dk), F32)],
        compiler_params=_cparams(("arbitrary",)),
        name="gla",
    )(p3, p3, p3, p3, a3, w2p, b2.reshape(1, hk), onorm_w.reshape(1, dm.gla_dv))
    return out.reshape(nb * dm.seq, hv)


def _hgrn_kernel(q_ref, f_ref, i_ref, g_ref, lbp_ref, nw_ref, o_ref, st_ref, kb_ref, *, layer, heads, dk, dv):
    nb, c = q_ref.shape[0], q_ref.shape[1]

    @pl.when(pl.program_id(0) == 0)
    def _():
        st_ref[...] = jnp.zeros_like(st_ref)

    lbp = lbp_ref[...]
    e = jnp.exp(lbp - jnp.max(lbp, axis=0, keepdims=True))
    p = e / jnp.sum(e, axis=0, keepdims=True)
    lb = jnp.zeros_like(p[0:1, :])
    for j in range(1, layer + 1):
        lb = lb + p[j:j + 1, :]
    sig_f = _sigmoid(f_ref[...].reshape(nb * c, f_ref.shape[2]).astype(F32))
    f_gate = lb + (1.0 - lb) * sig_f
    b_all = _cumsum_chunks(jnp.log(jnp.maximum(f_gate, F_MIN)), nb, c)
    k_all = (1.0 - lb) * (1.0 - sig_f)

    def step(factored):
        for bi in range(nb):
            for h in range(heads):
                q = q_ref[bi, :, h * dk:(h + 1) * dk].astype(F32)
                qs = q * _sigmoid(q) * (dk ** -0.5)
                vv = i_ref[bi, :, h * dv:(h + 1) * dv]
                o = _chunk_head(qs, k_all[bi * c:(bi + 1) * c, h * dk:(h + 1) * dk], vv,
                                b_all[bi][:, h * dk:(h + 1) * dk], st_ref, bi * heads + h, factored, kb_ref)
                g = g_ref[bi, :, h * dv:(h + 1) * dv].astype(F32)
                o = _rms(o, nw_ref[...]) * _sigmoid(g)
                o_ref[bi, :, h * dv:(h + 1) * dv] = o.astype(o_ref.dtype)

    lax.cond(_chunk_is_safe(b_all), functools.partial(step, True), functools.partial(step, False))


def _hgrn(p_hg, lower_bounds, onorm_w, layer, dm, offs):
    nb, c = dm.batch, dm.chunk
    hk, hv = dm.hg_heads * dm.hg_dk, dm.hg_heads * dm.hg_dv
    seg, p3 = _chunk_specs(nb, c, p_hg, offs)
    const = lambda shape: pl.BlockSpec(shape, lambda ci: (0, 0))
    out = pl.pallas_call(
        functools.partial(_hgrn_kernel, layer=layer, heads=dm.hg_heads, dk=dm.hg_dk, dv=dm.hg_dv),
        grid=(dm.seq // c,),
        in_specs=[seg("hg_q", hk), seg("hg_f", hk), seg("hg_i", hv), seg("hg_g", hv),
                  const((dm.depth, hk)), const((1, dm.hg_dv))],
        out_specs=pl.BlockSpec((nb, c, hv), lambda ci: (0, ci, 0)),
        out_shape=jax.ShapeDtypeStruct((nb, dm.seq, hv), BF16),
        scratch_shapes=[pltpu.VMEM((nb * dm.hg_heads, dm.hg_dv, dm.hg_dk), F32),
                        pltpu.VMEM((2, c, dm.hg_dk), F32)],
        compiler_params=_cparams(("arbitrary",)),
        name="hgrn",
    )(p3, p3, p3, p3, lower_bounds, onorm_w.reshape(1, dm.hg_dv))
    return out.reshape(nb * dm.seq, hv)


def _rope_table_kernel(pos_ref, *refs):
    n = len(refs) // 4
    pos = pos_ref[...].astype(F32)
    for t in range(n):
        f_ref, sgn_ref, cos_ref, sin_ref = refs[2 * t], refs[2 * t + 1], refs[2 * n + 2 * t], refs[2 * n + 2 * t + 1]
        ang = pos * f_ref[...]
        cos_ref[...] = jnp.cos(ang)
        sin_ref[...] = jnp.sin(ang) * sgn_ref[...]


def _rope_tables(pos_col, lane_params):
    m = pos_col.shape[0]
    tm = _tile(m, 512)
    lane = pl.BlockSpec((1, LANES), lambda i: (0, 0))
    tab = pl.BlockSpec((tm, LANES), lambda i: (i, 0))
    flat = [a for pair in lane_params for a in pair]
    return pl.pallas_call(
        _rope_table_kernel,
        grid=(m // tm,),
        in_specs=[pl.BlockSpec((tm, 1), lambda i: (i, 0))] + [lane] * len(flat),
        out_specs=[tab] * len(flat),
        out_shape=[jax.ShapeDtypeStruct((m, LANES), F32)] * len(flat),
        compiler_params=_cparams(("parallel",)),
        name="rope_tables",
    )(pos_col, *flat)


def _swap_halves(x, half):
    n = x.shape[-1]
    if 2 * half == n:
        return pltpu.roll(x, half, axis=1)
    lane = lax.broadcasted_iota(jnp.int32, x.shape, 1)
    return jnp.where((lane & half) == 0, pltpu.roll(x, n - half, axis=1), pltpu.roll(x, half, axis=1))


def _rope_epilogue(y, cos_ref, sin_ref, o_ref, *zero_refs, half, scale, copy_groups=()):
    for z_ref in zero_refs:
        z_ref[...] = jnp.zeros_like(z_ref)
    c, s = cos_ref[...], sin_ref[...]
    for g in range(y.shape[1] // LANES):
        yg = y[:, g * LANES:(g + 1) * LANES]
        if g not in copy_groups:
            yg = (yg * c + _swap_halves(yg, half) * s) * scale
        o_ref[:, g * LANES:(g + 1) * LANES] = yg.astype(o_ref.dtype)


def _idx_epilogue(y, cos_ref, sin_ref, iq_ref, ik_ref, iw_ref, *, idx_heads, idx_dim):
    c, s = cos_ref[...], sin_ref[...]
    nq = idx_heads * idx_dim // LANES
    for g in range(nq):
        yg = y[:, g * LANES:(g + 1) * LANES]
        r = (yg * c + _swap_halves(yg, idx_dim // 2) * s) * (idx_dim ** -0.5)
        iq_ref[:, g * LANES:(g + 1) * LANES] = r.astype(iq_ref.dtype)
    x = y[:, nq * LANES:(nq + 1) * LANES]
    r = x * c + _swap_halves(x, idx_dim // 2) * s
    lane = lax.broadcasted_iota(jnp.int32, x.shape, 1)
    ik_ref[...] = jnp.where(lane < idx_dim, r, pltpu.roll(r, idx_dim, axis=1)).astype(ik_ref.dtype)
    iw_ref[...] = jnp.where(lane < idx_heads, pltpu.roll(x, LANES - idx_dim, axis=1), 0.0) * (idx_heads ** -0.5)


def _ind(mask):
    return jnp.where(mask, 1.0, 0.0)


def _row_sum(x):
    return jnp.sum(x, axis=-1, keepdims=True)


def _float_of_ordered(u):
    k = u ^ jnp.int32(INT_MIN)
    return pltpu.bitcast(jnp.where(k < 0, k ^ jnp.int32(0x7FFFFFFF), k), F32)


def _dsa_kernel(q_ref, iq_ref, iw_ref, k_ref, v_ref, ik_ref, o_ref, *, heads, dim, idx_heads, idx_dim, topk,
                q_start, n_hidden):
    nbb, tq = q_ref.shape[0], q_ref.shape[1]
    tk = k_ref.shape[1]
    rows = nbb * tq
    nt = (((1,), (1,)), ((), ()))
    stack = lambda parts: parts[0] if len(parts) == 1 else jnp.concatenate(parts, axis=0)
    lane = lax.broadcasted_iota(jnp.int32, (tq, LANES), 1)
    scores = []
    for bi in range(nbb):
        ik = ik_ref[bi]
        score = jnp.zeros((tq, tk), F32)
        for h in range(idx_heads):
            g = (h * idx_dim) // LANES
            lo = (h * idx_dim) % LANES
            x = iq_ref[bi, :, g * LANES:(g + 1) * LANES]
            x = jnp.where(lane >= lo, jnp.where(lane < lo + idx_dim, x, jnp.zeros_like(x)), jnp.zeros_like(x))
            rel = lax.dot_general(x, ik, nt, preferred_element_type=F32)
            score = score + iw_ref[bi, :, h:h + 1] * jnp.maximum(rel, 0.0)
        scores.append(score)
    qpos = stack([q_start + lax.broadcasted_iota(jnp.int32, (tq, 1), 0)] * nbb)
    kpos = lax.broadcasted_iota(jnp.int32, (1, tk), 1)
    allowed = kpos <= qpos
    score = jnp.where(allowed, stack(scores), MASK_VALUE)
    hidden = float(n_hidden)

    def thr_body(it, prefix):
        trial = prefix | lax.shift_left(jnp.int32(1), 31 - it)
        cand = _float_of_ordered(trial)
        cnt = _row_sum(_ind(score >= cand)) + jnp.where(MASK_VALUE >= cand, hidden, 0.0)
        return jnp.where(cnt >= topk, trial, prefix)

    thr = _float_of_ordered(lax.fori_loop(0, 32, thr_body, jnp.zeros((rows, 1), jnp.int32)))
    above = _ind(score > thr)
    need = topk - _row_sum(above) - jnp.where(MASK_VALUE > thr, hidden, 0.0)
    tie = jnp.where(allowed, _ind(score == thr), 0.0)

    def all_ties():
        return jnp.where(allowed, _ind(score >= thr), 0.0)

    def ordered_ties():
        nbits = (tk - 1).bit_length()

        def tie_body(it, j):
            trial = j | lax.shift_left(jnp.int32(1), nbits - 1 - it)
            cnt = _row_sum(jnp.where(kpos < trial, tie, 0.0))
            return jnp.where(cnt < need, trial, j)

        j_last = lax.fori_loop(0, nbits, tie_body, jnp.zeros((rows, 1), jnp.int32))
        return jnp.where(allowed, above, 0.0) + jnp.where(kpos <= j_last, tie, 0.0)

    selected = lax.cond(jnp.max(_row_sum(tie) - need) > 0.0, ordered_ties, all_ties)

    for bi in range(nbb):
        valid = selected[bi * tq:(bi + 1) * tq, :] > 0.0
        k = k_ref[bi]
        v_ones = jnp.concatenate([v_ref[bi], jnp.ones((tk, dim), BF16)], axis=1)
        for h in range(heads):
            s = lax.dot_general(q_ref[bi, :, h * dim:(h + 1) * dim], k, nt, preferred_element_type=F32)
            s = jnp.where(valid, s, MASK_VALUE)
            p = jnp.exp(s - jnp.max(s, axis=-1, keepdims=True)).astype(BF16)
            o = jnp.dot(p, v_ones, preferred_element_type=F32)
            o_ref[bi, :, h * dim:(h + 1) * dim] = (o[:, 0:dim] / o[:, dim:dim + 1]).astype(o_ref.dtype)


def _with_carried_output(kern, q_ref, iq_ref, iw_ref, k_ref, v_ref, ik_ref, carried_ref, o_ref):
    del carried_ref
    kern(q_ref, iq_ref, iw_ref, k_ref, v_ref, ik_ref, o_ref)


def _dsa(q_rot, kv, iq_rot, ik_rot, iw, out_init, dm):
    tq, t, nb = dm.q_block, dm.seq, dm.batch
    hd = dm.dsa_heads * dm.dsa_dim
    assert t % tq == 0
    per_batch = lambda a: a.reshape(nb, t, a.shape[-1])
    q3, kv3, iq3, ik3, iw3, out = map(per_batch, (q_rot, kv, iq_rot, ik_rot, iw, out_init))
    for g in range(t // tq):
        tk = (g + 1) * tq
        nbb = 2 if (2 * tk <= t and nb % 2 == 0) else 1
        qrow = lambda w, g=g, nbb=nbb: pl.BlockSpec((nbb, tq, w), lambda bi: (bi, g, 0))
        krow = lambda w, blk=0, tk=tk, nbb=nbb: pl.BlockSpec((nbb, tk, w), lambda bi: (bi, 0, blk))
        kern = functools.partial(_dsa_kernel, heads=dm.dsa_heads, dim=dm.dsa_dim, idx_heads=dm.idx_heads,
                                 idx_dim=dm.idx_dim, topk=dm.topk, q_start=g * tq, n_hidden=t - tk)
        out = pl.pallas_call(
            functools.partial(_with_carried_output, kern),
            grid=(nb // nbb,),
            in_specs=[qrow(hd), qrow(iq3.shape[-1]), qrow(LANES), krow(dm.dsa_dim), krow(dm.dsa_dim, 1),
                      krow(LANES), pl.BlockSpec(memory_space=pl.ANY)],
            out_specs=qrow(hd),
            out_shape=jax.ShapeDtypeStruct((nb, t, hd), BF16),
            input_output_aliases={6: 0},
            compiler_params=_cparams(("parallel",)),
            name=f"dsa_q{g}",
        )(q3, iq3, iw3, kv3, kv3, ik3, out)
    return out.reshape(nb * t, hd)


def _forward(dm, x, positions, norm1_w, w_in, gla_gate_w2, gla_gate_b, gla_onorm_w, hgrn_lower_bounds,
             hgrn_onorm_w, w_branch_gla, w_branch_dsa, w_branch_hgrn, w_out, norm2_w, w_mlp_up, w_mlp_down,
             final_norm_w):
    m = dm.batch * dm.seq
    h = x.reshape(m, dm.d_model)
    src, ((gla_start, gla_width, gla_offs), (dsa_start, dsa_width, dsa_offs), (hg_start, hg_width, hg_offs)) = \
        _layout(dm)
    assert dm.dsa_dim == LANES and 2 * dm.idx_dim == LANES and dm.idx_heads <= dm.idx_dim
    idx_start = src["idx_q"][0]
    idx_width = -(-(src["idx_w"][0] + src["idx_w"][1] - idx_start) // LANES) * LANES
    w_t = jnp.swapaxes(w_in, 1, 2)

    def lanes(d):
        inv = ROPE_THETA ** (-jnp.arange(0, d, 2, dtype=F32) / d)
        reps = LANES // d
        f = jnp.tile(jnp.concatenate([inv, inv]), reps).reshape(1, LANES)
        sgn = jnp.tile(jnp.concatenate([-jnp.ones(d // 2, F32), jnp.ones(d // 2, F32)]), reps).reshape(1, LANES)
        return f, sgn

    pos_col = positions.reshape(m, 1)
    tabs = _rope_tables(pos_col, [lanes(dm.dsa_dim), lanes(dm.idx_dim)])

    u = _rmsnorm(h, norm1_w[0], BF16)
    for l in range(dm.depth):
        w2p = jnp.concatenate(
            [gla_gate_w2[l], jnp.zeros((LANES - dm.gla_rank, gla_gate_w2.shape[2]), F32)], axis=0)
        p_gla = _proj(u, w_t, l, gla_start, gla_width, [(None, BF16)], name="in_proj_gla")
        p_hg, w_up_bf16, w_down_bf16 = _proj(u, w_t, l, hg_start, hg_width, [(None, BF16)], vmem=VMEM_LIMIT_LARGE,
                                             casts=((w_mlp_up, 0), (w_mlp_down, 1)), name="in_proj_hg")
        a_low = _proj(u, w_t, l, src["gla_a"][0], LANES, [(LANES, F32)], name="in_proj_gate")
        hd = dm.dsa_heads * dm.dsa_dim
        q_rot, o_dsa_init = _proj(
            u, w_t, l, dsa_start, hd, [(None, BF16), (None, BF16)], tables=tabs[:2], tm_pref=1024,
            name="in_proj_dsa_q",
            epilogue=functools.partial(_rope_epilogue, half=dm.dsa_dim // 2, scale=dm.dsa_dim ** -0.5))
        kv = _proj(u, w_t, l, dsa_start + dsa_offs["dsa_k"], 2 * dm.dsa_dim, [(2 * dm.dsa_dim, BF16)],
                   tables=tabs[:2], tm_pref=1024, tn_pref=2 * dm.dsa_dim, name="in_proj_dsa_kv",
                   epilogue=functools.partial(_rope_epilogue, half=dm.dsa_dim // 2, scale=1.0, copy_groups=(1,)))
        iq_rot, ik_rot, iw = _proj(
            u, w_t, l, idx_start, idx_width, [(dm.idx_heads * dm.idx_dim, BF16), (LANES, BF16), (LANES, F32)],
            tables=tabs[2:], tm_pref=1024, tn_pref=idx_width, name="in_proj_idx",
            epilogue=functools.partial(_idx_epilogue, idx_heads=dm.idx_heads, idx_dim=dm.idx_dim))
        o_gla = _gla(p_gla, a_low, w2p, gla_gate_b[l], gla_onorm_w[l], dm, gla_offs)
        o_hg = _hgrn(p_hg, hgrn_lower_bounds, hgrn_onorm_w[l], l, dm, hg_offs)
        o_dsa = _dsa(q_rot, kv, iq_rot, ik_rot, iw, o_dsa_init, dm)
        merged = _merge(o_gla, o_dsa, o_hg, w_branch_gla, w_branch_dsa, w_branch_hgrn, l, p_hg,
                        (hg_offs["gate_a"], hg_offs["gate_b"], hg_offs["gate_c"]), dm.d_model)
        h = _matmul_residual(merged, w_out, l, h)
        last = l == dm.depth - 1
        res = _mlp(h, norm2_w[l], w_up_bf16, w_down_bf16, final_norm_w if last else norm1_w[l + 1], last)
        if not last:
            h, u = res
    return res.reshape(dm.batch, dm.seq, dm.d_model)


def kernel(x, positions, norm1_w, w_in, gla_gate_w2, gla_gate_b, gla_onorm_w, hgrn_lower_bounds, hgrn_onorm_w,
           w_branch_gla, w_branch_dsa, w_branch_hgrn, w_out, norm2_w, w_mlp_up, w_mlp_down, final_norm_w):
    return _forward(_prod_dims(), x, positions, norm1_w, w_in, gla_gate_w2, gla_gate_b, gla_onorm_w,
                    hgrn_lower_bounds, hgrn_onorm_w, w_branch_gla, w_branch_dsa, w_branch_hgrn, w_out, norm2_w,
                    w_mlp_up, w_mlp_down, final_norm_w)
```

```python
import functools
from typing import NamedTuple

import jax
import jax.numpy as jnp
from jax import lax
from jax.experimental import pallas as pl
from jax.experimental.pallas import tpu as pltpu

F32 = jnp.float32
BF16 = jnp.bfloat16
HIGHEST = lax.Precision.HIGHEST

ROPE_THETA = 10000.0
NORM_EPS = 1e-6
MASK_VALUE = -1e30
F_MIN = 1e-12
GLA_GATE_NORMALIZER = 16.0
LANES = 128
SUBLANES = 8
ROW_BLOCK = 128
INT_MIN = -2 ** 31
SAFE_HALF_DECAY = 80.0
VMEM_LIMIT = 48 * 1024 * 1024
VMEM_LIMIT_LARGE = 56 * 1024 * 1024


class _Dims(NamedTuple):
    d_model: int
    batch: int
    seq: int
    depth: int
    gla_heads: int
    gla_dk: int
    gla_dv: int
    gla_rank: int
    dsa_heads: int
    dsa_dim: int
    idx_heads: int
    idx_dim: int
    topk: int
    hg_heads: int
    hg_dk: int
    hg_dv: int
    d_ff: int
    chunk: int
    q_block: int


def _prod_dims():
    d = 2048
    return _Dims(d_model=d, batch=4, seq=2048, depth=2,
                 gla_heads=4, gla_dk=d // 2 // 4, gla_dv=d // 4, gla_rank=16,
                 dsa_heads=16, dsa_dim=128, idx_heads=8, idx_dim=64, topk=min(256, 2048 // 4),
                 hg_heads=d // 128, hg_dk=128, hg_dv=128, d_ff=4 * d, chunk=128, q_block=256)


def _in_sizes(dm):
    return (dm.gla_heads * dm.gla_dk, dm.gla_heads * dm.gla_dk, dm.gla_heads * dm.gla_dv,
            dm.gla_heads * dm.gla_dv, dm.gla_rank,
            dm.dsa_heads * dm.dsa_dim, dm.dsa_dim, dm.dsa_dim, dm.idx_heads * dm.idx_dim, dm.idx_dim,
            dm.idx_heads,
            dm.hg_heads * dm.hg_dk, dm.hg_heads * dm.hg_dk, dm.hg_heads * dm.hg_dv, dm.hg_heads * dm.hg_dv,
            dm.d_model, dm.d_model, dm.d_model)


_IN_NAMES = ("gla_q", "gla_k", "gla_v", "gla_g", "gla_a", "dsa_q", "dsa_k", "dsa_v", "idx_q", "idx_k",
             "idx_w", "hg_q", "hg_f", "hg_i", "hg_g", "gate_a", "gate_b", "gate_c")
_GROUPS = (("gla_q", "gla_k", "gla_v", "gla_g"), ("dsa_q", "dsa_k", "dsa_v"),
           ("hg_q", "hg_f", "hg_i", "hg_g", "gate_a", "gate_b", "gate_c"))


def _tile(n, pref):
    t = min(n, pref)
    while n % t:
        t //= 2
    return t


def _layout(dm):
    sizes = dict(zip(_IN_NAMES, _in_sizes(dm)))
    src, off = {}, 0
    for name in _IN_NAMES:
        src[name] = (off, sizes[name])
        off += sizes[name]
    groups = []
    for names in _GROUPS:
        start = src[names[0]][0]
        offs = {n: src[n][0] - start for n in names}
        width = src[names[-1]][0] + src[names[-1]][1] - start
        for n in names:
            assert offs[n] % min(sizes[n], 2 * LANES) == 0, n
        groups.append((start, width, offs))
    return src, groups


def _cparams(sem, vmem=VMEM_LIMIT):
    return pltpu.CompilerParams(dimension_semantics=sem, vmem_limit_bytes=vmem)


def _rms(x, w):
    ms = jnp.mean(x * x, axis=-1, keepdims=True)
    return x * lax.rsqrt(ms + NORM_EPS) * w


def _rmsnorm_kernel(x_ref, w_ref, o_ref):
    o_ref[...] = _rms(x_ref[...], w_ref[...]).astype(o_ref.dtype)


def _rmsnorm(x, w, out_dtype):
    m, d = x.shape
    tm = _tile(m, 512)
    return pl.pallas_call(
        _rmsnorm_kernel,
        grid=(m // tm,),
        in_specs=[pl.BlockSpec((tm, d), lambda i: (i, 0)), pl.BlockSpec((1, d), lambda i: (0, 0))],
        out_specs=pl.BlockSpec((tm, d), lambda i: (i, 0)),
        out_shape=jax.ShapeDtypeStruct((m, d), out_dtype),
        compiler_params=_cparams(("parallel",)),
        name="rmsnorm",
    )(x, w.reshape(1, d))


def _w_spec(w, layer, k, tn):
    if w.ndim == 3:
        return pl.BlockSpec((None, k, tn), lambda i, j: (layer, 0, j))
    return pl.BlockSpec((k, tn), lambda i, j: (0, j))


def _store(y, o_ref):
    o_ref[...] = y.astype(o_ref.dtype)


def _proj_kernel(x_ref, *refs, shift, n_w, n_tab, n_cast, tn, epilogue):
    w_refs, rest = refs[:n_w], refs[n_w:]
    tabs, cast_in, rest = rest[:n_tab], rest[n_tab:n_tab + n_cast], rest[n_tab + n_cast:]
    outs, cast_out = rest[:len(rest) - n_cast], rest[len(rest) - n_cast:]
    if n_w == 1:
        w = w_refs[0][...]
    else:
        w = jnp.concatenate([r[...] for r in w_refs], axis=0)[shift:shift + tn, :]
    y = lax.dot_general(x_ref[...], w.astype(BF16), (((1,), (1,)), ((), ())), preferred_element_type=F32)
    epilogue(y, *tabs, *outs)
    for src_ref, dst_ref in zip(cast_in, cast_out):
        dst_ref[...] = src_ref[...].astype(dst_ref.dtype)


def _cast_job(w, layer, gm, gn, axis_i):
    dims = list(w.shape[1:])
    other = dims[1 - axis_i]
    nj = max(d for d in range(1, gn + 1) if other % d == 0 and (other // d) % LANES == 0)
    blk = [0, 0]
    blk[axis_i], blk[1 - axis_i] = dims[axis_i] // gm, other // nj
    assert dims[axis_i] % gm == 0 and blk[axis_i] % LANES == 0

    def imap(i, j):
        jj = jnp.minimum(j, nj - 1)
        return (i, jj) if axis_i == 0 else (jj, i)

    return (pl.BlockSpec((None, *blk), lambda i, j: (layer, *imap(i, j))), pl.BlockSpec(tuple(blk), imap),
            jax.ShapeDtypeStruct(tuple(dims), BF16))


def _proj(x, w_t, layer, start, n, outs, epilogue=_store, tables=(), casts=(), tm_pref=2048, tn_pref=512,
          vmem=VMEM_LIMIT, name="proj"):
    m, k = x.shape
    tm, tn = _tile(m, tm_pref), _tile(n, tn_pref)
    outs = [(tn if cols is None else cols, dtype) for cols, dtype in outs]
    jobs = [_cast_job(w, layer, m // tm, n // tn, axis_i) for w, axis_i in casts]
    if start % tn == 0:
        shift = 0
        specs = [pl.BlockSpec((None, tn, k), lambda i, j: (layer, start // tn + j, 0))]
    else:
        assert tn % ROW_BLOCK == 0 and start % SUBLANES == 0
        base, shift = divmod(start, ROW_BLOCK)
        per_tile = tn // ROW_BLOCK
        blk = lambda b: pl.BlockSpec((None, ROW_BLOCK, k), lambda i, j: (layer, base + j * per_tile + b, 0))
        specs = [blk(b) for b in range(per_tile + 1)]
    res = pl.pallas_call(
        functools.partial(_proj_kernel, shift=shift, n_w=len(specs), n_tab=len(tables), n_cast=len(jobs), tn=tn,
                          epilogue=epilogue),
        grid=(m // tm, n // tn),
        in_specs=[pl.BlockSpec((tm, k), lambda i, j: (i, 0))] + specs
                 + [pl.BlockSpec((tm, LANES), lambda i, j: (i, 0)) for _ in tables] + [job[0] for job in jobs],
        out_specs=[pl.BlockSpec((tm, cols), lambda i, j: (i, j)) for cols, _ in outs] + [job[1] for job in jobs],
        out_shape=[jax.ShapeDtypeStruct((m, cols * (n // tn)), dtype) for cols, dtype in outs]
                  + [job[2] for job in jobs],
        compiler_params=_cparams(("parallel", "arbitrary"), vmem),
        name=name,
    )(x, *([w_t] * len(specs)), *tables, *[w for w, _ in casts])
    return res[0] if len(res) == 1 else res


def _sigmoid(x):
    return 1.0 / (1.0 + jnp.exp(-x))


def _gate_sigmoid(x):
    return 0.5 * jnp.tanh(0.5 * x) + 0.5


def _merge_kernel(oa_ref, ob_ref, oc_ref, wa_ref, wb_ref, wc_ref, ga_ref, gb_ref, gc_ref, o_ref):
    def branch(o_ref_, w_ref_, g_ref_):
        y = jnp.dot(o_ref_[...], w_ref_[...].astype(BF16), preferred_element_type=F32)
        return _gate_sigmoid(g_ref_[...].astype(F32)) * y

    acc = branch(oa_ref, wa_ref, ga_ref) + branch(ob_ref, wb_ref, gb_ref) + branch(oc_ref, wc_ref, gc_ref)
    o_ref[...] = acc.astype(o_ref.dtype)


def _merge(o_gla, o_dsa, o_hg, w_gla, w_dsa, w_hg, layer, gates, gate_offs, d_model):
    m = o_gla.shape[0]
    tm, tn = _tile(m, 1024), _tile(d_model, 256)
    o_spec = lambda a: pl.BlockSpec((tm, a.shape[1]), lambda i, j: (i, 0))

    def g_spec(off):
        assert off % tn == 0
        return pl.BlockSpec((tm, tn), lambda i, j: (i, off // tn + j))

    return pl.pallas_call(
        _merge_kernel,
        grid=(m // tm, d_model // tn),
        in_specs=[o_spec(o_gla), o_spec(o_dsa), o_spec(o_hg),
                  _w_spec(w_gla, layer, w_gla.shape[-2], tn), _w_spec(w_dsa, layer, w_dsa.shape[-2], tn),
                  _w_spec(w_hg, layer, w_hg.shape[-2], tn),
                  g_spec(gate_offs[0]), g_spec(gate_offs[1]), g_spec(gate_offs[2])],
        out_specs=pl.BlockSpec((tm, tn), lambda i, j: (i, j)),
        out_shape=jax.ShapeDtypeStruct((m, d_model), BF16),
        compiler_params=_cparams(("parallel", "arbitrary"), VMEM_LIMIT_LARGE),
        name="merge",
    )(o_gla, o_dsa, o_hg, w_gla, w_dsa, w_hg, gates, gates, gates)


def _matmul_residual_kernel(x_ref, w_ref, r_ref, o_ref):
    o_ref[...] = r_ref[...] + jnp.dot(x_ref[...], w_ref[...].astype(BF16), preferred_element_type=F32)


def _matmul_residual(x, w, layer, res, tm_pref=1024, tn_pref=512):
    m, k = x.shape
    n = w.shape[-1]
    tm, tn = _tile(m, tm_pref), _tile(n, tn_pref)
    return pl.pallas_call(
        _matmul_residual_kernel,
        grid=(m // tm, n // tn),
        in_specs=[pl.BlockSpec((tm, k), lambda i, j: (i, 0)), _w_spec(w, layer, k, tn),
                  pl.BlockSpec((tm, tn), lambda i, j: (i, j))],
        out_specs=pl.BlockSpec((tm, tn), lambda i, j: (i, j)),
        out_shape=jax.ShapeDtypeStruct((m, n), F32),
        compiler_params=_cparams(("parallel", "arbitrary")),
        name="out_proj",
    )(x, w, res)


def _mlp_kernel(h_ref, nw_ref, wu_ref, wd_ref, nnw_ref, o_ref, *rest, last):
    u_ref = rest[-1]
    j = pl.program_id(1)

    @pl.when(j == 0)
    def _():
        x = h_ref[...]
        u_ref[...] = _rms(x, nw_ref[...]).astype(BF16)
        o_ref[...] = x

    a = jnp.dot(u_ref[...], wu_ref[...], preferred_element_type=F32)
    a = jnp.square(jnp.maximum(a, 0.0)).astype(BF16)
    o_ref[...] += jnp.dot(a, wd_ref[...], preferred_element_type=F32)

    @pl.when(j == pl.num_programs(1) - 1)
    def _():
        y = _rms(o_ref[...], nnw_ref[...])
        if last:
            o_ref[...] = y
        else:
            rest[0][...] = y.astype(BF16)


def _mlp(h, norm_w, w_up, w_down, next_norm_w, last):
    m, d = h.shape
    f = w_up.shape[-1]
    tm, tf = _tile(m, 512), _tile(f, 1024)
    row = pl.BlockSpec((tm, d), lambda i, j: (i, 0))
    vec = pl.BlockSpec((1, d), lambda i, j: (0, 0))
    out_shape = [jax.ShapeDtypeStruct((m, d), F32)] + ([] if last else [jax.ShapeDtypeStruct((m, d), BF16)])
    res = pl.pallas_call(
        functools.partial(_mlp_kernel, last=last),
        grid=(m // tm, f // tf),
        in_specs=[row, vec, pl.BlockSpec((d, tf), lambda i, j: (0, j)), pl.BlockSpec((tf, d), lambda i, j: (j, 0)),
                  vec],
        out_specs=[row] * len(out_shape),
        out_shape=out_shape,
        scratch_shapes=[pltpu.VMEM((tm, d), BF16)],
        compiler_params=_cparams(("parallel", "arbitrary"), VMEM_LIMIT_LARGE),
        name="mlp",
    )(h, norm_w.reshape(1, d), w_up, w_down, next_norm_w.reshape(1, d))
    return res[0] if last else res


def _chunk_head(qs, kk, vv, b, st_ref, head, factored, kb_ref):
    c, kdim = qs.shape
    row = lax.broadcasted_iota(jnp.int32, (c, c), 0)
    col = lax.broadcasted_iota(jnp.int32, (c, c), 1)
    b_last = b[c - 1:c, :]
    st = st_ref[head]
    nt = (((1,), (1,)), ((), ()))
    if factored:
        ref_row = b[c // 2 - 1:c // 2, :]
        qd = qs * jnp.exp(b - ref_row)
        kd = kk * jnp.exp(ref_row - b)
        q_in = qd * jnp.exp(ref_row)
        k_dec = kd * jnp.exp(b_last - ref_row)
        att = lax.dot_general(qd.astype(BF16), kd.astype(BF16), nt, preferred_element_type=F32)
    else:
        q_in = qs * jnp.exp(b)
        k_dec = kk * jnp.exp(b_last - b)
        kb_ref[0, :, 0:kdim] = kk
        kb_ref[1, :, 0:kdim] = b

        def body(s, att):
            k_row = kb_ref[0, pl.ds(s, 1), 0:kdim]
            b_row = kb_ref[1, pl.ds(s, 1), 0:kdim]
            w = jnp.sum(qs * k_row * jnp.exp(jnp.minimum(b - b_row, 0.0)), axis=-1, keepdims=True)
            return jnp.where(col == s, w, att)

        att = lax.fori_loop(0, c, body, jnp.zeros((c, c), F32))
    att = jnp.where(col <= row, att, 0.0)
    o = lax.dot_general(q_in.astype(BF16), st.astype(BF16), nt, preferred_element_type=F32)
    o = o + jnp.dot(att.astype(BF16), vv.astype(BF16), preferred_element_type=F32)
    upd = lax.dot_general(vv.astype(BF16), k_dec.astype(BF16), (((0,), (0,)), ((), ())),
                          preferred_element_type=F32)
    st_ref[head] = st * jnp.exp(b_last) + upd
    return o


def _cumsum_chunks(x, nb, c):
    row = lax.broadcasted_iota(jnp.int32, (c, c), 0)
    col = lax.broadcasted_iota(jnp.int32, (c, c), 1)
    tri = jnp.where(col <= row, 1.0, 0.0).astype(F32)
    return [jnp.dot(tri, x[bi * c:(bi + 1) * c, :], preferred_element_type=F32, precision=HIGHEST)
            for bi in range(nb)]


def _chunk_is_safe(b_all):
    c = b_all[0].shape[0]
    worst = None
    for b in b_all:
        ref_row = b[c // 2 - 1:c // 2, :]
        span = jnp.maximum(-ref_row, ref_row - b[c - 1:c, :])
        worst = span if worst is None else jnp.maximum(worst, span)
    return jnp.max(worst) <= SAFE_HALF_DECAY


def _log_sigmoid(x):
    return jnp.minimum(x, 0.0) - jnp.log(1.0 + jnp.exp(-jnp.abs(x)))


def _gla_kernel(q_ref, k_ref, v_ref, g_ref, a_ref, w2_ref, b2_ref, nw_ref, o_ref, st_ref, kb_ref, *,
                heads, dk, dv):
    nb, c = q_ref.shape[0], q_ref.shape[1]

    @pl.when(pl.program_id(0) == 0)
    def _():
        st_ref[...] = jnp.zeros_like(st_ref)

    a = a_ref[...].reshape(nb * c, a_ref.shape[2])
    z = jnp.dot(a, w2_ref[...], preferred_element_type=F32, precision=HIGHEST) + b2_ref[...]
    b_all = _cumsum_chunks(_log_sigmoid(z) * (1.0 / GLA_GATE_NORMALIZER), nb, c)

    def step(factored):
        for bi in range(nb):
            for h in range(heads):
                qs = q_ref[bi, :, h * dk:(h + 1) * dk].astype(F32) * (dk ** -0.5)
                kk = k_ref[bi, :, h * dk:(h + 1) * dk].astype(F32)
                vv = v_ref[bi, :, h * dv:(h + 1) * dv]
                o = _chunk_head(qs, kk, vv, b_all[bi][:, h * dk:(h + 1) * dk], st_ref, bi * heads + h, factored,
                                kb_ref)
                g = g_ref[bi, :, h * dv:(h + 1) * dv].astype(F32)
                o = _rms(o, nw_ref[...]) * (g * _gate_sigmoid(g))
                o_ref[bi, :, h * dv:(h + 1) * dv] = o.astype(o_ref.dtype)

    lax.cond(_chunk_is_safe(b_all), functools.partial(step, True), functools.partial(step, False))


def _chunk_specs(nb, c, arr, offs):
    def seg(name, width):
        assert offs[name] % width == 0
        blk = offs[name] // width
        return pl.BlockSpec((nb, c, width), lambda ci: (0, ci, blk))

    return seg, arr.reshape(nb, -1, arr.shape[-1])


def _gla(p_gla, a_low, w2p, b2, onorm_w, dm, offs):
    nb, c = dm.batch, dm.chunk
    hk, hv = dm.gla_heads * dm.gla_dk, dm.gla_heads * dm.gla_dv
    seg, p3 = _chunk_specs(nb, c, p_gla, offs)
    a3 = a_low.reshape(nb, dm.seq, LANES)
    const = lambda shape: pl.BlockSpec(shape, lambda ci: (0, 0))
    out = pl.pallas_call(
        functools.partial(_gla_kernel, heads=dm.gla_heads, dk=dm.gla_dk, dv=dm.gla_dv),
        grid=(dm.seq // c,),
        in_specs=[seg("gla_q", hk), seg("gla_k", hk), seg("gla_v", hv), seg("gla_g", hv),
                  pl.BlockSpec((nb, c, LANES), lambda ci: (0, ci, 0)),
                  const((LANES, hk)), const((1, hk)), const((1, dm.gla_dv))],
        out_specs=pl.BlockSpec((nb, c, hv), lambda ci: (0, ci, 0)),
        out_shape=jax.ShapeDtypeStruct((nb, dm.seq, hv), BF16),
        scratch_shapes=[pltpu.VMEM((nb * dm.gla_heads, dm.gla_dv, dm.gla_dk), F32),
                        pltpu.VMEM((2, c, dm.gla_dk), F32)],
        compiler_params=_cparams(("arbitrary",)),
        name="gla",
    )(p3, p3, p3, p3, a3, w2p, b2.reshape(1, hk), onorm_w.reshape(1, dm.gla_dv))
    return out.reshape(nb * dm.seq, hv)


def _hgrn_kernel(q_ref, f_ref, i_ref, g_ref, lbp_ref, nw_ref, o_ref, st_ref, kb_ref, *, layer, heads, dk, dv):
    nb, c = q_ref.shape[0], q_ref.shape[1]

    @pl.when(pl.program_id(0) == 0)
    def _():
        st_ref[...] = jnp.zeros_like(st_ref)

    lbp = lbp_ref[...]
    e = jnp.exp(lbp - jnp.max(lbp, axis=0, keepdims=True))
    p = e / jnp.sum(e, axis=0, keepdims=True)
    lb = jnp.zeros_like(p[0:1, :])
    for j in range(1, layer + 1):
        lb = lb + p[j:j + 1, :]
    sig_f = _sigmoid(f_ref[...].reshape(nb * c, f_ref.shape[2]).astype(F32))
    f_gate = lb + (1.0 - lb) * sig_f
    b_all = _cumsum_chunks(jnp.log(jnp.maximum(f_gate, F_MIN)), nb, c)
    k_all = (1.0 - lb) * (1.0 - sig_f)

    def step(factored):
        for bi in range(nb):
            for h in range(heads):
                q = q_ref[bi, :, h * dk:(h + 1) * dk].astype(F32)
                qs = q * _gate_sigmoid(q) * (dk ** -0.5)
                vv = i_ref[bi, :, h * dv:(h + 1) * dv]
                o = _chunk_head(qs, k_all[bi * c:(bi + 1) * c, h * dk:(h + 1) * dk], vv,
                                b_all[bi][:, h * dk:(h + 1) * dk], st_ref, bi * heads + h, factored, kb_ref)
                g = g_ref[bi, :, h * dv:(h + 1) * dv].astype(F32)
                o = _rms(o, nw_ref[...]) * _gate_sigmoid(g)
                o_ref[bi, :, h * dv:(h + 1) * dv] = o.astype(o_ref.dtype)

    lax.cond(_chunk_is_safe(b_all), functools.partial(step, True), functools.partial(step, False))


def _hgrn(p_hg, lower_bounds, onorm_w, layer, dm, offs):
    nb, c = dm.batch, dm.chunk
    hk, hv = dm.hg_heads * dm.hg_dk, dm.hg_heads * dm.hg_dv
    seg, p3 = _chunk_specs(nb, c, p_hg, offs)
    const = lambda shape: pl.BlockSpec(shape, lambda ci: (0, 0))
    out = pl.pallas_call(
        functools.partial(_hgrn_kernel, layer=layer, heads=dm.hg_heads, dk=dm.hg_dk, dv=dm.hg_dv),
        grid=(dm.seq // c,),
        in_specs=[seg("hg_q", hk), seg("hg_f", hk), seg("hg_i", hv), seg("hg_g", hv),
                  const((dm.depth, hk)), const((1, dm.hg_dv))],
        out_specs=pl.BlockSpec((nb, c, hv), lambda ci: (0, ci, 0)),
        out_shape=jax.ShapeDtypeStruct((nb, dm.seq, hv), BF16),
        scratch_shapes=[pltpu.VMEM((nb * dm.hg_heads, dm.hg_dv, dm.hg_dk), F32),
                        pltpu.VMEM((2, c, dm.hg_dk), F32)],
        compiler_params=_cparams(("arbitrary",)),
        name="hgrn",
    )(p3, p3, p3, p3, lower_bounds, onorm_w.reshape(1, dm.hg_dv))
    return out.reshape(nb * dm.seq, hv)


def _rope_table_kernel(pos_ref, *refs):
    n = len(refs) // 4
    pos = pos_ref[...].astype(F32)
    for t in range(n):
        f_ref, sgn_ref, cos_ref, sin_ref = refs[2 * t], refs[2 * t + 1], refs[2 * n + 2 * t], refs[2 * n + 2 * t + 1]
        ang = pos * f_ref[...]
        cos_ref[...] = jnp.cos(ang)
        sin_ref[...] = jnp.sin(ang) * sgn_ref[...]


def _rope_tables(pos_col, lane_params):
    m = pos_col.shape[0]
    tm = _tile(m, 512)
    lane = pl.BlockSpec((1, LANES), lambda i: (0, 0))
    tab = pl.BlockSpec((tm, LANES), lambda i: (i, 0))
    flat = [a for pair in lane_params for a in pair]
    return pl.pallas_call(
        _rope_table_kernel,
        grid=(m // tm,),
        in_specs=[pl.BlockSpec((tm, 1), lambda i: (i, 0))] + [lane] * len(flat),
        out_specs=[tab] * len(flat),
        out_shape=[jax.ShapeDtypeStruct((m, LANES), F32)] * len(flat),
        compiler_params=_cparams(("parallel",)),
        name="rope_tables",
    )(pos_col, *flat)


def _swap_halves(x, half):
    n = x.shape[-1]
    if 2 * half == n:
        return pltpu.roll(x, half, axis=1)
    lane = lax.broadcasted_iota(jnp.int32, x.shape, 1)
    return jnp.where((lane & half) == 0, pltpu.roll(x, n - half, axis=1), pltpu.roll(x, half, axis=1))


def _rope_epilogue(y, cos_ref, sin_ref, o_ref, *zero_refs, half, scale, copy_groups=()):
    for z_ref in zero_refs:
        z_ref[...] = jnp.zeros_like(z_ref)
    c, s = cos_ref[...], sin_ref[...]
    for g in range(y.shape[1] // LANES):
        yg = y[:, g * LANES:(g + 1) * LANES]
        if g not in copy_groups:
            yg = (yg * c + _swap_halves(yg, half) * s) * scale
        o_ref[:, g * LANES:(g + 1) * LANES] = yg.astype(o_ref.dtype)


def _idx_epilogue(y, cos_ref, sin_ref, iq_ref, ik_ref, iw_ref, *, idx_heads, idx_dim):
    c, s = cos_ref[...], sin_ref[...]
    nq = idx_heads * idx_dim // LANES
    for g in range(nq):
        yg = y[:, g * LANES:(g + 1) * LANES]
        r = (yg * c + _swap_halves(yg, idx_dim // 2) * s) * (idx_dim ** -0.5)
        iq_ref[:, g * LANES:(g + 1) * LANES] = r.astype(iq_ref.dtype)
    x = y[:, nq * LANES:(nq + 1) * LANES]
    r = x * c + _swap_halves(x, idx_dim // 2) * s
    lane = lax.broadcasted_iota(jnp.int32, x.shape, 1)
    ik_ref[...] = jnp.where(lane < idx_dim, r, pltpu.roll(r, idx_dim, axis=1)).astype(ik_ref.dtype)
    iw_ref[...] = jnp.where(lane < idx_heads, pltpu.roll(x, LANES - idx_dim, axis=1), 0.0) * (idx_heads ** -0.5)


def _ind(mask):
    return jnp.where(mask, 1.0, 0.0)


def _row_sum(x):
    return jnp.sum(x, axis=-1, keepdims=True)


def _float_of_ordered(u):
    k = u ^ jnp.int32(INT_MIN)
    return pltpu.bitcast(jnp.where(k < 0, k ^ jnp.int32(0x7FFFFFFF), k), F32)


def _dsa_kernel(q_ref, iq_ref, iw_ref, k_ref, v_ref, ik_ref, o_ref, *, heads, dim, idx_heads, idx_dim, topk,
                q_start, n_hidden):
    nbb, tq = q_ref.shape[0], q_ref.shape[1]
    tk = k_ref.shape[1]
    rows = nbb * tq
    nt = (((1,), (1,)), ((), ()))
    stack = lambda parts: parts[0] if len(parts) == 1 else jnp.concatenate(parts, axis=0)
    lane = lax.broadcasted_iota(jnp.int32, (tq, LANES), 1)
    scores = []
    for bi in range(nbb):
        ik = ik_ref[bi]
        score = jnp.zeros((tq, tk), F32)
        for h in range(idx_heads):
            g = (h * idx_dim) // LANES
            lo = (h * idx_dim) % LANES
            x = iq_ref[bi, :, g * LANES:(g + 1) * LANES]
            x = jnp.where(lane >= lo, jnp.where(lane < lo + idx_dim, x, jnp.zeros_like(x)), jnp.zeros_like(x))
            rel = lax.dot_general(x, ik, nt, preferred_element_type=F32)
            score = score + iw_ref[bi, :, h:h + 1] * jnp.maximum(rel, 0.0)
        scores.append(score)
    qpos = stack([q_start + lax.broadcasted_iota(jnp.int32, (tq, 1), 0)] * nbb)
    kpos = lax.broadcasted_iota(jnp.int32, (1, tk), 1)
    allowed = kpos <= qpos
    score = jnp.where(allowed, stack(scores), MASK_VALUE)
    hidden = float(n_hidden)

    def thr_body(it, prefix):
        trial = prefix | lax.shift_left(jnp.int32(1), 31 - it)
        cand = _float_of_ordered(trial)
        cnt = _row_sum(_ind(score >= cand)) + jnp.where(MASK_VALUE >= cand, hidden, 0.0)
        return jnp.where(cnt >= topk, trial, prefix)

    thr = _float_of_ordered(lax.fori_loop(0, 32, thr_body, jnp.zeros((rows, 1), jnp.int32)))
    above = _ind(score > thr)
    need = topk - _row_sum(above) - jnp.where(MASK_VALUE > thr, hidden, 0.0)
    tie = jnp.where(allowed, _ind(score == thr), 0.0)

    def all_ties():
        return jnp.where(allowed, _ind(score >= thr), 0.0)

    def ordered_ties():
        nbits = (tk - 1).bit_length()

        def tie_body(it, j):
            trial = j | lax.shift_left(jnp.int32(1), nbits - 1 - it)
            cnt = _row_sum(jnp.where(kpos < trial, tie, 0.0))
            return jnp.where(cnt < need, trial, j)

        j_last = lax.fori_loop(0, nbits, tie_body, jnp.zeros((rows, 1), jnp.int32))
        return jnp.where(allowed, above, 0.0) + jnp.where(kpos <= j_last, tie, 0.0)

    selected = lax.cond(jnp.max(_row_sum(tie) - need) > 0.0, ordered_ties, all_ties)

    for bi in range(nbb):
        valid = selected[bi * tq:(bi + 1) * tq, :] > 0.0
        k = k_ref[bi]
        v_ones = jnp.concatenate([v_ref[bi], jnp.ones((tk, dim), BF16)], axis=1)
        for h in range(heads):
            s = lax.dot_general(q_ref[bi, :, h * dim:(h + 1) * dim], k, nt, preferred_element_type=F32)
            s = jnp.where(valid, s, MASK_VALUE)
            p = jnp.exp(s - jnp.max(s, axis=-1, keepdims=True)).astype(BF16)
            o = jnp.dot(p, v_ones, preferred_element_type=F32)
            o_ref[bi, :, h * dim:(h + 1) * dim] = (o[:, 0:dim] / o[:, dim:dim + 1]).astype(o_ref.dtype)


def _with_carried_output(kern, q_ref, iq_ref, iw_ref, k_ref, v_ref, ik_ref, carried_ref, o_ref):
    del carried_ref
    kern(q_ref, iq_ref, iw_ref, k_ref, v_ref, ik_ref, o_ref)


def _dsa(q_rot, kv, iq_rot, ik_rot, iw, out_init, dm):
    tq, t, nb = dm.q_block, dm.seq, dm.batch
    hd = dm.dsa_heads * dm.dsa_dim
    assert t % tq == 0
    per_batch = lambda a: a.reshape(nb, t, a.shape[-1])
    q3, kv3, iq3, ik3, iw3, out = map(per_batch, (q_rot, kv, iq_rot, ik_rot, iw, out_init))
    for g in range(t // tq):
        tk = (g + 1) * tq
        nbb = 2 if (2 * tk <= t and nb % 2 == 0) else 1
        qrow = lambda w, g=g, nbb=nbb: pl.BlockSpec((nbb, tq, w), lambda bi: (bi, g, 0))
        krow = lambda w, blk=0, tk=tk, nbb=nbb: pl.BlockSpec((nbb, tk, w), lambda bi: (bi, 0, blk))
        kern = functools.partial(_dsa_kernel, heads=dm.dsa_heads, dim=dm.dsa_dim, idx_heads=dm.idx_heads,
                                 idx_dim=dm.idx_dim, topk=dm.topk, q_start=g * tq, n_hidden=t - tk)
        out = pl.pallas_call(
            functools.partial(_with_carried_output, kern),
            grid=(nb // nbb,),
            in_specs=[qrow(hd), qrow(iq3.shape[-1]), qrow(LANES), krow(dm.dsa_dim), krow(dm.dsa_dim, 1),
                      krow(LANES), pl.BlockSpec(memory_space=pl.ANY)],
            out_specs=qrow(hd),
            out_shape=jax.ShapeDtypeStruct((nb, t, hd), BF16),
            input_output_aliases={6: 0},
            compiler_params=_cparams(("parallel",)),
            name=f"dsa_q{g}",
        )(q3, iq3, iw3, kv3, kv3, ik3, out)
    return out.reshape(nb * t, hd)


def _forward(dm, x, positions, norm1_w, w_in, gla_gate_w2, gla_gate_b, gla_onorm_w, hgrn_lower_bounds,
             hgrn_onorm_w, w_branch_gla, w_branch_dsa, w_branch_hgrn, w_out, norm2_w, w_mlp_up, w_mlp_down,
             final_norm_w):
    m = dm.batch * dm.seq
    h = x.reshape(m, dm.d_model)
    src, ((gla_start, gla_width, gla_offs), (dsa_start, dsa_width, dsa_offs), (hg_start, hg_width, hg_offs)) = \
        _layout(dm)
    assert dm.dsa_dim == LANES and 2 * dm.idx_dim == LANES and dm.idx_heads <= dm.idx_dim
    idx_start = src["idx_q"][0]
    idx_width = -(-(src["idx_w"][0] + src["idx_w"][1] - idx_start) // LANES) * LANES
    w_t = jnp.swapaxes(w_in, 1, 2)

    def lanes(d):
        inv = ROPE_THETA ** (-jnp.arange(0, d, 2, dtype=F32) / d)
        reps = LANES // d
        f = jnp.tile(jnp.concatenate([inv, inv]), reps).reshape(1, LANES)
        sgn = jnp.tile(jnp.concatenate([-jnp.ones(d // 2, F32), jnp.ones(d // 2, F32)]), reps).reshape(1, LANES)
        return f, sgn

    pos_col = positions.reshape(m, 1)
    tabs = _rope_tables(pos_col, [lanes(dm.dsa_dim), lanes(dm.idx_dim)])

    u = _rmsnorm(h, norm1_w[0], BF16)
    for l in range(dm.depth):
        w2p = jnp.concatenate(
            [gla_gate_w2[l], jnp.zeros((LANES - dm.gla_rank, gla_gate_w2.shape[2]), F32)], axis=0)
        p_gla = _proj(u, w_t, l, gla_start, gla_width, [(None, BF16)], name="in_proj_gla")
        p_hg, w_up_bf16, w_down_bf16 = _proj(u, w_t, l, hg_start, hg_width, [(None, BF16)], vmem=VMEM_LIMIT_LARGE,
                                             casts=((w_mlp_up, 0), (w_mlp_down, 1)), name="in_proj_hg")
        a_low = _proj(u, w_t, l, src["gla_a"][0], LANES, [(LANES, F32)], name="in_proj_gate")
        hd = dm.dsa_heads * dm.dsa_dim
        q_rot, o_dsa_init = _proj(
            u, w_t, l, dsa_start, hd, [(None, BF16), (None, BF16)], tables=tabs[:2], tm_pref=1024,
            name="in_proj_dsa_q",
            epilogue=functools.partial(_rope_epilogue, half=dm.dsa_dim // 2, scale=dm.dsa_dim ** -0.5))
        kv = _proj(u, w_t, l, dsa_start + dsa_offs["dsa_k"], 2 * dm.dsa_dim, [(2 * dm.dsa_dim, BF16)],
                   tables=tabs[:2], tm_pref=1024, tn_pref=2 * dm.dsa_dim, name="in_proj_dsa_kv",
                   epilogue=functools.partial(_rope_epilogue, half=dm.dsa_dim // 2, scale=1.0, copy_groups=(1,)))
        iq_rot, ik_rot, iw = _proj(
            u, w_t, l, idx_start, idx_width, [(dm.idx_heads * dm.idx_dim, BF16), (LANES, BF16), (LANES, F32)],
            tables=tabs[2:], tm_pref=1024, tn_pref=idx_width, name="in_proj_idx",
            epilogue=functools.partial(_idx_epilogue, idx_heads=dm.idx_heads, idx_dim=dm.idx_dim))
        o_gla = _gla(p_gla, a_low, w2p, gla_gate_b[l], gla_onorm_w[l], dm, gla_offs)
        o_hg = _hgrn(p_hg, hgrn_lower_bounds, hgrn_onorm_w[l], l, dm, hg_offs)
        o_dsa = _dsa(q_rot, kv, iq_rot, ik_rot, iw, o_dsa_init, dm)
        merged = _merge(o_gla, o_dsa, o_hg, w_branch_gla, w_branch_dsa, w_branch_hgrn, l, p_hg,
                        (hg_offs["gate_a"], hg_offs["gate_b"], hg_offs["gate_c"]), dm.d_model)
        h = _matmul_residual(merged, w_out, l, h)
        last = l == dm.depth - 1
        res = _mlp(h, norm2_w[l], w_up_bf16, w_down_bf16, final_norm_w if last else norm1_w[l + 1], last)
        if not last:
            h, u = res
    return res.reshape(dm.batch, dm.seq, dm.d_model)


def kernel(x, positions, norm1_w, w_in, gla_gate_w2, gla_gate_b, gla_onorm_w, hgrn_lower_bounds, hgrn_onorm_w,
           w_branch_gla, w_branch_dsa, w_branch_hgrn, w_out, norm2_w, w_mlp_up, w_mlp_down, final_norm_w):
    return _forward(_prod_dims(), x, positions, norm1_w, w_in, gla_gate_w2, gla_gate_b, gla_onorm_w,
                    hgrn_lower_bounds, hgrn_onorm_w, w_branch_gla, w_branch_dsa, w_branch_hgrn, w_out, norm2_w,
                    w_mlp_up, w_mlp_down, final_norm_w)
```

```python
import functools
from typing import NamedTuple

import jax
import jax.numpy as jnp
from jax import lax
from jax.experimental import pallas as pl
from jax.experimental.pallas import tpu as pltpu

F32 = jnp.float32
BF16 = jnp.bfloat16
HIGHEST = lax.Precision.HIGHEST

ROPE_THETA = 10000.0
NORM_EPS = 1e-6
MASK_VALUE = -1e30
F_MIN = 1e-12
GLA_GATE_NORMALIZER = 16.0
LANES = 128
SUBLANES = 8
ROW_BLOCK = 128
INT_MIN = -2 ** 31
SAFE_HALF_DECAY = 80.0
VMEM_LIMIT = 48 * 1024 * 1024
VMEM_LIMIT_LARGE = 56 * 1024 * 1024


class _Dims(NamedTuple):
    d_model: int
    batch: int
    seq: int
    depth: int
    gla_heads: int
    gla_dk: int
    gla_dv: int
    gla_rank: int
    dsa_heads: int
    dsa_dim: int
    idx_heads: int
    idx_dim: int
    topk: int
    hg_heads: int
    hg_dk: int
    hg_dv: int
    d_ff: int
    chunk: int
    q_block: int


def _prod_dims():
    d = 2048
    return _Dims(d_model=d, batch=4, seq=2048, depth=2,
                 gla_heads=4, gla_dk=d // 2 // 4, gla_dv=d // 4, gla_rank=16,
                 dsa_heads=16, dsa_dim=128, idx_heads=8, idx_dim=64, topk=min(256, 2048 // 4),
                 hg_heads=d // 128, hg_dk=128, hg_dv=128, d_ff=4 * d, chunk=128, q_block=256)


def _in_sizes(dm):
    return (dm.gla_heads * dm.gla_dk, dm.gla_heads * dm.gla_dk, dm.gla_heads * dm.gla_dv,
            dm.gla_heads * dm.gla_dv, dm.gla_rank,
            dm.dsa_heads * dm.dsa_dim, dm.dsa_dim, dm.dsa_dim, dm.idx_heads * dm.idx_dim, dm.idx_dim,
            dm.idx_heads,
            dm.hg_heads * dm.hg_dk, dm.hg_heads * dm.hg_dk, dm.hg_heads * dm.hg_dv, dm.hg_heads * dm.hg_dv,
            dm.d_model, dm.d_model, dm.d_model)


_IN_NAMES = ("gla_q", "gla_k", "gla_v", "gla_g", "gla_a", "dsa_q", "dsa_k", "dsa_v", "idx_q", "idx_k",
             "idx_w", "hg_q", "hg_f", "hg_i", "hg_g", "gate_a", "gate_b", "gate_c")
_GROUPS = (("gla_q", "gla_k", "gla_v", "gla_g"), ("dsa_q", "dsa_k", "dsa_v"),
           ("hg_q", "hg_f", "hg_i", "hg_g", "gate_a", "gate_b", "gate_c"))


def _tile(n, pref):
    t = min(n, pref)
    while n % t:
        t //= 2
    return t


def _layout(dm):
    sizes = dict(zip(_IN_NAMES, _in_sizes(dm)))
    src, off = {}, 0
    for name in _IN_NAMES:
        src[name] = (off, sizes[name])
        off += sizes[name]
    groups = []
    for names in _GROUPS:
        start = src[names[0]][0]
        offs = {n: src[n][0] - start for n in names}
        width = src[names[-1]][0] + src[names[-1]][1] - start
        for n in names:
            assert offs[n] % min(sizes[n], 2 * LANES) == 0, n
        groups.append((start, width, offs))
    return src, groups


def _cparams(sem, vmem=VMEM_LIMIT):
    return pltpu.CompilerParams(dimension_semantics=sem, vmem_limit_bytes=vmem)


def _rms(x, w):
    ms = jnp.mean(x * x, axis=-1, keepdims=True)
    return x * lax.rsqrt(ms + NORM_EPS) * w


def _rmsnorm_kernel(x_ref, w_ref, o_ref):
    o_ref[...] = _rms(x_ref[...], w_ref[...]).astype(o_ref.dtype)


def _rmsnorm(x, w, out_dtype):
    m, d = x.shape
    tm = _tile(m, 512)
    return pl.pallas_call(
        _rmsnorm_kernel,
        grid=(m // tm,),
        in_specs=[pl.BlockSpec((tm, d), lambda i: (i, 0)), pl.BlockSpec((1, d), lambda i: (0, 0))],
        out_specs=pl.BlockSpec((tm, d), lambda i: (i, 0)),
        out_shape=jax.ShapeDtypeStruct((m, d), out_dtype),
        compiler_params=_cparams(("parallel",)),
        name="rmsnorm",
    )(x, w.reshape(1, d))


def _w_spec(w, layer, k, tn):
    if w.ndim == 3:
        return pl.BlockSpec((None, k, tn), lambda i, j: (layer, 0, j))
    return pl.BlockSpec((k, tn), lambda i, j: (0, j))


def _store(y, o_ref):
    o_ref[...] = y.astype(o_ref.dtype)


def _proj_kernel(x_ref, *refs, shift, n_w, n_tab, n_cast, tn, epilogue):
    w_refs, rest = refs[:n_w], refs[n_w:]
    tabs, cast_in, rest = rest[:n_tab], rest[n_tab:n_tab + n_cast], rest[n_tab + n_cast:]
    outs, cast_out = rest[:len(rest) - n_cast], rest[len(rest) - n_cast:]
    if n_w == 1:
        w = w_refs[0][...]
    else:
        w = jnp.concatenate([r[...] for r in w_refs], axis=0)[shift:shift + tn, :]
    y = lax.dot_general(x_ref[...], w.astype(BF16), (((1,), (1,)), ((), ())), preferred_element_type=F32)
    epilogue(y, *tabs, *outs)
    for src_ref, dst_ref in zip(cast_in, cast_out):
        dst_ref[...] = src_ref[...].astype(dst_ref.dtype)


def _cast_job(w, layer, gm, gn, axis_i):
    dims = list(w.shape[1:])
    other = dims[1 - axis_i]
    nj = max(d for d in range(1, gn + 1) if other % d == 0 and (other // d) % LANES == 0)
    blk = [0, 0]
    blk[axis_i], blk[1 - axis_i] = dims[axis_i] // gm, other // nj
    assert dims[axis_i] % gm == 0 and blk[axis_i] % LANES == 0

    def imap(i, j):
        jj = jnp.minimum(j, nj - 1)
        return (i, jj) if axis_i == 0 else (jj, i)

    return (pl.BlockSpec((None, *blk), lambda i, j: (layer, *imap(i, j))), pl.BlockSpec(tuple(blk), imap),
            jax.ShapeDtypeStruct(tuple(dims), BF16))


def _proj(x, w_t, layer, start, n, outs, epilogue=_store, tables=(), casts=(), tm_pref=2048, tn_pref=512,
          vmem=VMEM_LIMIT, name="proj"):
    m, k = x.shape
    tm, tn = _tile(m, tm_pref), _tile(n, tn_pref)
    outs = [(tn if cols is None else cols, dtype) for cols, dtype in outs]
    jobs = [_cast_job(w, layer, m // tm, n // tn, axis_i) for w, axis_i in casts]
    if start % tn == 0:
        shift = 0
        specs = [pl.BlockSpec((None, tn, k), lambda i, j: (layer, start // tn + j, 0))]
    else:
        assert tn % ROW_BLOCK == 0 and start % SUBLANES == 0
        base, shift = divmod(start, ROW_BLOCK)
        per_tile = tn // ROW_BLOCK
        blk = lambda b: pl.BlockSpec((None, ROW_BLOCK, k), lambda i, j: (layer, base + j * per_tile + b, 0))
        specs = [blk(b) for b in range(per_tile + 1)]
    res = pl.pallas_call(
        functools.partial(_proj_kernel, shift=shift, n_w=len(specs), n_tab=len(tables), n_cast=len(jobs), tn=tn,
                          epilogue=epilogue),
        grid=(m // tm, n // tn),
        in_specs=[pl.BlockSpec((tm, k), lambda i, j: (i, 0))] + specs
                 + [pl.BlockSpec((tm, LANES), lambda i, j: (i, 0)) for _ in tables] + [job[0] for job in jobs],
        out_specs=[pl.BlockSpec((tm, cols), lambda i, j: (i, j)) for cols, _ in outs] + [job[1] for job in jobs],
        out_shape=[jax.ShapeDtypeStruct((m, cols * (n // tn)), dtype) for cols, dtype in outs]
                  + [job[2] for job in jobs],
        compiler_params=_cparams(("parallel", "arbitrary"), vmem),
        name=name,
    )(x, *([w_t] * len(specs)), *tables, *[w for w, _ in casts])
    return res[0] if len(res) == 1 else res


def _sigmoid(x):
    return 1.0 / (1.0 + jnp.exp(-x))


def _gate_sigmoid(x):
    return 0.5 * jnp.tanh(0.5 * x) + 0.5


def _merge_kernel(oa_ref, ob_ref, oc_ref, wa_ref, wb_ref, wc_ref, ga_ref, gb_ref, gc_ref, o_ref):
    def branch(o_ref_, w_ref_, g_ref_):
        y = jnp.dot(o_ref_[...], w_ref_[...].astype(BF16), preferred_element_type=F32)
        return _gate_sigmoid(g_ref_[...].astype(F32)) * y

    acc = branch(oa_ref, wa_ref, ga_ref) + branch(ob_ref, wb_ref, gb_ref) + branch(oc_ref, wc_ref, gc_ref)
    o_ref[...] = acc.astype(o_ref.dtype)


def _merge(o_gla, o_dsa, o_hg, w_gla, w_dsa, w_hg, layer, gates, gate_offs, d_model):
    m = o_gla.shape[0]
    tm, tn = _tile(m, 1024), _tile(d_model, 256)
    o_spec = lambda a: pl.BlockSpec((tm, a.shape[1]), lambda i, j: (i, 0))

    def g_spec(off):
        assert off % tn == 0
        return pl.BlockSpec((tm, tn), lambda i, j: (i, off // tn + j))

    return pl.pallas_call(
        _merge_kernel,
        grid=(m // tm, d_model // tn),
        in_specs=[o_spec(o_gla), o_spec(o_dsa), o_spec(o_hg),
                  _w_spec(w_gla, layer, w_gla.shape[-2], tn), _w_spec(w_dsa, layer, w_dsa.shape[-2], tn),
                  _w_spec(w_hg, layer, w_hg.shape[-2], tn),
                  g_spec(gate_offs[0]), g_spec(gate_offs[1]), g_spec(gate_offs[2])],
        out_specs=pl.BlockSpec((tm, tn), lambda i, j: (i, j)),
        out_shape=jax.ShapeDtypeStruct((m, d_model), BF16),
        compiler_params=_cparams(("parallel", "arbitrary"), VMEM_LIMIT_LARGE),
        name="merge",
    )(o_gla, o_dsa, o_hg, w_gla, w_dsa, w_hg, gates, gates, gates)


def _matmul_residual_kernel(x_ref, w_ref, r_ref, o_ref):
    o_ref[...] = r_ref[...] + jnp.dot(x_ref[...], w_ref[...].astype(BF16), preferred_element_type=F32)


def _matmul_residual(x, w, layer, res, tm_pref=2048, tn_pref=512):
    m, k = x.shape
    n = w.shape[-1]
    tm, tn = _tile(m, tm_pref), _tile(n, tn_pref)
    return pl.pallas_call(
        _matmul_residual_kernel,
        grid=(m // tm, n // tn),
        in_specs=[pl.BlockSpec((tm, k), lambda i, j: (i, 0)), _w_spec(w, layer, k, tn),
                  pl.BlockSpec((tm, tn), lambda i, j: (i, j))],
        out_specs=pl.BlockSpec((tm, tn), lambda i, j: (i, j)),
        out_shape=jax.ShapeDtypeStruct((m, n), F32),
        compiler_params=_cparams(("parallel", "arbitrary")),
        name="out_proj",
    )(x, w, res)


def _mlp_kernel(h_ref, nw_ref, wu_ref, wd_ref, nnw_ref, o_ref, *rest, last):
    u_ref = rest[-1]
    j = pl.program_id(1)

    @pl.when(j == 0)
    def _():
        x = h_ref[...]
        u_ref[...] = _rms(x, nw_ref[...]).astype(BF16)
        o_ref[...] = x

    a = jnp.dot(u_ref[...], wu_ref[...], preferred_element_type=F32)
    a = jnp.square(jnp.maximum(a, 0.0)).astype(BF16)
    o_ref[...] += jnp.dot(a, wd_ref[...], preferred_element_type=F32)

    @pl.when(j == pl.num_programs(1) - 1)
    def _():
        y = _rms(o_ref[...], nnw_ref[...])
        if last:
            o_ref[...] = y
        else:
            rest[0][...] = y.astype(BF16)


def _mlp(h, norm_w, w_up, w_down, next_norm_w, last):
    m, d = h.shape
    f = w_up.shape[-1]
    tm, tf = _tile(m, 512), _tile(f, 1024)
    row = pl.BlockSpec((tm, d), lambda i, j: (i, 0))
    vec = pl.BlockSpec((1, d), lambda i, j: (0, 0))
    out_shape = [jax.ShapeDtypeStruct((m, d), F32)] + ([] if last else [jax.ShapeDtypeStruct((m, d), BF16)])
    res = pl.pallas_call(
        functools.partial(_mlp_kernel, last=last),
        grid=(m // tm, f // tf),
        in_specs=[row, vec, pl.BlockSpec((d, tf), lambda i, j: (0, j)), pl.BlockSpec((tf, d), lambda i, j: (j, 0)),
                  vec],
        out_specs=[row] * len(out_shape),
        out_shape=out_shape,
        scratch_shapes=[pltpu.VMEM((tm, d), BF16)],
        compiler_params=_cparams(("parallel", "arbitrary"), VMEM_LIMIT_LARGE),
        name="mlp",
    )(h, norm_w.reshape(1, d), w_up, w_down, next_norm_w.reshape(1, d))
    return res[0] if last else res


def _chunk_head(qs, kk, vv, b, st_ref, head, factored, kb_ref):
    c, kdim = qs.shape
    row = lax.broadcasted_iota(jnp.int32, (c, c), 0)
    col = lax.broadcasted_iota(jnp.int32, (c, c), 1)
    b_last = b[c - 1:c, :]
    st = st_ref[head]
    nt = (((1,), (1,)), ((), ()))
    if factored:
        ref_row = b[c // 2 - 1:c // 2, :]
        qd = qs * jnp.exp(b - ref_row)
        kd = kk * jnp.exp(ref_row - b)
        q_in = qd * jnp.exp(ref_row)
        k_dec = kd * jnp.exp(b_last - ref_row)
        att = lax.dot_general(qd.astype(BF16), kd.astype(BF16), nt, preferred_element_type=F32)
    else:
        q_in = qs * jnp.exp(b)
        k_dec = kk * jnp.exp(b_last - b)
        kb_ref[0, :, 0:kdim] = kk
        kb_ref[1, :, 0:kdim] = b

        def body(s, att):
            k_row = kb_ref[0, pl.ds(s, 1), 0:kdim]
            b_row = kb_ref[1, pl.ds(s, 1), 0:kdim]
            w = jnp.sum(qs * k_row * jnp.exp(jnp.minimum(b - b_row, 0.0)), axis=-1, keepdims=True)
            return jnp.where(col == s, w, att)

        att = lax.fori_loop(0, c, body, jnp.zeros((c, c), F32))
    att = jnp.where(col <= row, att, 0.0)
    o = lax.dot_general(q_in.astype(BF16), st.astype(BF16), nt, preferred_element_type=F32)
    o = o + jnp.dot(att.astype(BF16), vv.astype(BF16), preferred_element_type=F32)
    upd = lax.dot_general(vv.astype(BF16), k_dec.astype(BF16), (((0,), (0,)), ((), ())),
                          preferred_element_type=F32)
    st_ref[head] = st * jnp.exp(b_last) + upd
    return o


def _cumsum_chunks(x, nb, c):
    row = lax.broadcasted_iota(jnp.int32, (c, c), 0)
    col = lax.broadcasted_iota(jnp.int32, (c, c), 1)
    tri = jnp.where(col <= row, 1.0, 0.0).astype(F32)
    return [jnp.dot(tri, x[bi * c:(bi + 1) * c, :], preferred_element_type=F32, precision=HIGHEST)
            for bi in range(nb)]


def _chunk_is_safe(b_all):
    c = b_all[0].shape[0]
    worst = None
    for b in b_all:
        ref_row = b[c // 2 - 1:c // 2, :]
        span = jnp.maximum(-ref_row, ref_row - b[c - 1:c, :])
        worst = span if worst is None else jnp.maximum(worst, span)
    return jnp.max(worst) <= SAFE_HALF_DECAY


def _log_sigmoid(x):
    return jnp.minimum(x, 0.0) - jnp.log(1.0 + jnp.exp(-jnp.abs(x)))


def _gla_kernel(q_ref, k_ref, v_ref, g_ref, a_ref, w2_ref, b2_ref, nw_ref, o_ref, st_ref, kb_ref, *,
                heads, dk, dv):
    nb, c = q_ref.shape[0], q_ref.shape[1]

    @pl.when(pl.program_id(0) == 0)
    def _():
        st_ref[...] = jnp.zeros_like(st_ref)

    a = a_ref[...].reshape(nb * c, a_ref.shape[2])
    z = jnp.dot(a, w2_ref[...], preferred_element_type=F32, precision=HIGHEST) + b2_ref[...]
    b_all = _cumsum_chunks(_log_sigmoid(z) * (1.0 / GLA_GATE_NORMALIZER), nb, c)

    def step(factored):
        for bi in range(nb):
            for h in range(heads):
                qs = q_ref[bi, :, h * dk:(h + 1) * dk].astype(F32) * (dk ** -0.5)
                kk = k_ref[bi, :, h * dk:(h + 1) * dk].astype(F32)
                vv = v_ref[bi, :, h * dv:(h + 1) * dv]
                o = _chunk_head(qs, kk, vv, b_all[bi][:, h * dk:(h + 1) * dk], st_ref, bi * heads + h, factored,
                                kb_ref)
                g = g_ref[bi, :, h * dv:(h + 1) * dv].astype(F32)
                o = _rms(o, nw_ref[...]) * (g * _gate_sigmoid(g))
                o_ref[bi, :, h * dv:(h + 1) * dv] = o.astype(o_ref.dtype)

    lax.cond(_chunk_is_safe(b_all), functools.partial(step, True), functools.partial(step, False))


def _chunk_specs(nb, c, arr, offs):
    def seg(name, width):
        assert offs[name] % width == 0
        blk = offs[name] // width
        return pl.BlockSpec((nb, c, width), lambda ci: (0, ci, blk))

    return seg, arr.reshape(nb, -1, arr.shape[-1])


def _gla(p_gla, a_low, w2p, b2, onorm_w, dm, offs):
    nb, c = dm.batch, dm.chunk
    hk, hv = dm.gla_heads * dm.gla_dk, dm.gla_heads * dm.gla_dv
    seg, p3 = _chunk_specs(nb, c, p_gla, offs)
    a3 = a_low.reshape(nb, dm.seq, LANES)
    const = lambda shape: pl.BlockSpec(shape, lambda ci: (0, 0))
    out = pl.pallas_call(
        functools.partial(_gla_kernel, heads=dm.gla_heads, dk=dm.gla_dk, dv=dm.gla_dv),
        grid=(dm.seq // c,),
        in_specs=[seg("gla_q", hk), seg("gla_k", hk), seg("gla_v", hv), seg("gla_g", hv),
                  pl.BlockSpec((nb, c, LANES), lambda ci: (0, ci, 0)),
                  const((LANES, hk)), const((1, hk)), const((1, dm.gla_dv))],
        out_specs=pl.BlockSpec((nb, c, hv), lambda ci: (0, ci, 0)),
        out_shape=jax.ShapeDtypeStruct((nb, dm.seq, hv), BF16),
        scratch_shapes=[pltpu.VMEM((nb * dm.gla_heads, dm.gla_dv, dm.gla_dk), F32),
                        pltpu.VMEM((2, c, dm.gla_dk), F32)],
        compiler_params=_cparams(("arbitrary",)),
        name="gla",
    )(p3, p3, p3, p3, a3, w2p, b2.reshape(1, hk), onorm_w.reshape(1, dm.gla_dv))
    return out.reshape(nb * dm.seq, hv)


def _hgrn_kernel(q_ref, f_ref, i_ref, g_ref, lbp_ref, nw_ref, o_ref, st_ref, kb_ref, *, layer, heads, dk, dv):
    nb, c = q_ref.shape[0], q_ref.shape[1]

    @pl.when(pl.program_id(0) == 0)
    def _():
        st_ref[...] = jnp.zeros_like(st_ref)

    lbp = lbp_ref[...]
    e = jnp.exp(lbp - jnp.max(lbp, axis=0, keepdims=True))
    p = e / jnp.sum(e, axis=0, keepdims=True)
    lb = jnp.zeros_like(p[0:1, :])
    for j in range(1, layer + 1):
        lb = lb + p[j:j + 1, :]
    sig_f = _sigmoid(f_ref[...].reshape(nb * c, f_ref.shape[2]).astype(F32))
    f_gate = lb + (1.0 - lb) * sig_f
    b_all = _cumsum_chunks(jnp.log(jnp.maximum(f_gate, F_MIN)), nb, c)
    k_all = (1.0 - lb) * (1.0 - sig_f)

    def step(factored):
        for bi in range(nb):
            for h in range(heads):
                q = q_ref[bi, :, h * dk:(h + 1) * dk].astype(F32)
                qs = q * _gate_sigmoid(q) * (dk ** -0.5)
                vv = i_ref[bi, :, h * dv:(h + 1) * dv]
                o = _chunk_head(qs, k_all[bi * c:(bi + 1) * c, h * dk:(h + 1) * dk], vv,
                                b_all[bi][:, h * dk:(h + 1) * dk], st_ref, bi * heads + h, factored, kb_ref)
                g = g_ref[bi, :, h * dv:(h + 1) * dv].astype(F32)
                o = _rms(o, nw_ref[...]) * _gate_sigmoid(g)
                o_ref[bi, :, h * dv:(h + 1) * dv] = o.astype(o_ref.dtype)

    lax.cond(_chunk_is_safe(b_all), functools.partial(step, True), functools.partial(step, False))


def _hgrn(p_hg, lower_bounds, onorm_w, layer, dm, offs):
    nb, c = dm.batch, dm.chunk
    hk, hv = dm.hg_heads * dm.hg_dk, dm.hg_heads * dm.hg_dv
    seg, p3 = _chunk_specs(nb, c, p_hg, offs)
    const = lambda shape: pl.BlockSpec(shape, lambda ci: (0, 0))
    out = pl.pallas_call(
        functools.partial(_hgrn_kernel, layer=layer, heads=dm.hg_heads, dk=dm.hg_dk, dv=dm.hg_dv),
        grid=(dm.seq // c,),
        in_specs=[seg("hg_q", hk), seg("hg_f", hk), seg("hg_i", hv), seg("hg_g", hv),
                  const((dm.depth, hk)), const((1, dm.hg_dv))],
        out_specs=pl.BlockSpec((nb, c, hv), lambda ci: (0, ci, 0)),
        out_shape=jax.ShapeDtypeStruct((nb, dm.seq, hv), BF16),
        scratch_shapes=[pltpu.VMEM((nb * dm.hg_heads, dm.hg_dv, dm.hg_dk), F32),
                        pltpu.VMEM((2, c, dm.hg_dk), F32)],
        compiler_params=_cparams(("arbitrary",)),
        name="hgrn",
    )(p3, p3, p3, p3, lower_bounds, onorm_w.reshape(1, dm.hg_dv))
    return out.reshape(nb * dm.seq, hv)


def _rope_table_kernel(pos_ref, *refs):
    n = len(refs) // 4
    pos = pos_ref[...].astype(F32)
    for t in range(n):
        f_ref, sgn_ref, cos_ref, sin_ref = refs[2 * t], refs[2 * t + 1], refs[2 * n + 2 * t], refs[2 * n + 2 * t + 1]
        ang = pos * f_ref[...]
        cos_ref[...] = jnp.cos(ang)
        sin_ref[...] = jnp.sin(ang) * sgn_ref[...]


def _rope_tables(pos_col, lane_params):
    m = pos_col.shape[0]
    tm = _tile(m, 512)
    lane = pl.BlockSpec((1, LANES), lambda i: (0, 0))
    tab = pl.BlockSpec((tm, LANES), lambda i: (i, 0))
    flat = [a for pair in lane_params for a in pair]
    return pl.pallas_call(
        _rope_table_kernel,
        grid=(m // tm,),
        in_specs=[pl.BlockSpec((tm, 1), lambda i: (i, 0))] + [lane] * len(flat),
        out_specs=[tab] * len(flat),
        out_shape=[jax.ShapeDtypeStruct((m, LANES), F32)] * len(flat),
        compiler_params=_cparams(("parallel",)),
        name="rope_tables",
    )(pos_col, *flat)


def _swap_halves(x, half):
    n = x.shape[-1]
    if 2 * half == n:
        return pltpu.roll(x, half, axis=1)
    lane = lax.broadcasted_iota(jnp.int32, x.shape, 1)
    return jnp.where((lane & half) == 0, pltpu.roll(x, n - half, axis=1), pltpu.roll(x, half, axis=1))


def _rope_epilogue(y, cos_ref, sin_ref, o_ref, *zero_refs, half, scale, copy_groups=()):
    for z_ref in zero_refs:
        z_ref[...] = jnp.zeros_like(z_ref)
    c, s = cos_ref[...], sin_ref[...]
    for g in range(y.shape[1] // LANES):
        yg = y[:, g * LANES:(g + 1) * LANES]
        if g not in copy_groups:
            yg = (yg * c + _swap_halves(yg, half) * s) * scale
        o_ref[:, g * LANES:(g + 1) * LANES] = yg.astype(o_ref.dtype)


def _idx_epilogue(y, cos_ref, sin_ref, iq_ref, ik_ref, iw_ref, *, idx_heads, idx_dim):
    c, s = cos_ref[...], sin_ref[...]
    nq = idx_heads * idx_dim // LANES
    for g in range(nq):
        yg = y[:, g * LANES:(g + 1) * LANES]
        r = (yg * c + _swap_halves(yg, idx_dim // 2) * s) * (idx_dim ** -0.5)
        iq_ref[:, g * LANES:(g + 1) * LANES] = r.astype(iq_ref.dtype)
    x = y[:, nq * LANES:(nq + 1) * LANES]
    r = x * c + _swap_halves(x, idx_dim // 2) * s
    lane = lax.broadcasted_iota(jnp.int32, x.shape, 1)
    ik_ref[...] = jnp.where(lane < idx_dim, r, pltpu.roll(r, idx_dim, axis=1)).astype(ik_ref.dtype)
    iw_ref[...] = jnp.where(lane < idx_heads, pltpu.roll(x, LANES - idx_dim, axis=1), 0.0) * (idx_heads ** -0.5)


def _ind(mask):
    return jnp.where(mask, 1.0, 0.0)


def _row_sum(x):
    return jnp.sum(x, axis=-1, keepdims=True)


def _float_of_ordered(u):
    k = u ^ jnp.int32(INT_MIN)
    return pltpu.bitcast(jnp.where(k < 0, k ^ jnp.int32(0x7FFFFFFF), k), F32)


def _dsa_kernel(q_ref, iq_ref, iw_ref, k_ref, v_ref, ik_ref, o_ref, *, heads, dim, idx_heads, idx_dim, topk,
                q_start, n_hidden):
    nbb, tq = q_ref.shape[0], q_ref.shape[1]
    tk = k_ref.shape[1]
    rows = nbb * tq
    nt = (((1,), (1,)), ((), ()))
    stack = lambda parts: parts[0] if len(parts) == 1 else jnp.concatenate(parts, axis=0)
    lane = lax.broadcasted_iota(jnp.int32, (tq, LANES), 1)
    scores = []
    for bi in range(nbb):
        ik = ik_ref[bi]
        score = jnp.zeros((tq, tk), F32)
        for h in range(idx_heads):
            g = (h * idx_dim) // LANES
            lo = (h * idx_dim) % LANES
            x = iq_ref[bi, :, g * LANES:(g + 1) * LANES]
            x = jnp.where(lane >= lo, jnp.where(lane < lo + idx_dim, x, jnp.zeros_like(x)), jnp.zeros_like(x))
            rel = lax.dot_general(x, ik, nt, preferred_element_type=F32)
            score = score + iw_ref[bi, :, h:h + 1] * jnp.maximum(rel, 0.0)
        scores.append(score)
    qpos = stack([q_start + lax.broadcasted_iota(jnp.int32, (tq, 1), 0)] * nbb)
    kpos = lax.broadcasted_iota(jnp.int32, (1, tk), 1)
    allowed = kpos <= qpos
    score = jnp.where(allowed, stack(scores), MASK_VALUE)
    hidden = float(n_hidden)

    def thr_body(it, prefix):
        trial = prefix | lax.shift_left(jnp.int32(1), 31 - it)
        cand = _float_of_ordered(trial)
        cnt = _row_sum(_ind(score >= cand)) + jnp.where(MASK_VALUE >= cand, hidden, 0.0)
        return jnp.where(cnt >= topk, trial, prefix)

    thr = _float_of_ordered(lax.fori_loop(0, 32, thr_body, jnp.zeros((rows, 1), jnp.int32)))
    above = _ind(score > thr)
    need = topk - _row_sum(above) - jnp.where(MASK_VALUE > thr, hidden, 0.0)
    tie = jnp.where(allowed, _ind(score == thr), 0.0)

    def all_ties():
        return jnp.where(allowed, _ind(score >= thr), 0.0)

    def ordered_ties():
        nbits = (tk - 1).bit_length()

        def tie_body(it, j):
            trial = j | lax.shift_left(jnp.int32(1), nbits - 1 - it)
            cnt = _row_sum(jnp.where(kpos < trial, tie, 0.0))
            return jnp.where(cnt < need, trial, j)

        j_last = lax.fori_loop(0, nbits, tie_body, jnp.zeros((rows, 1), jnp.int32))
        return jnp.where(allowed, above, 0.0) + jnp.where(kpos <= j_last, tie, 0.0)

    selected = lax.cond(jnp.max(_row_sum(tie) - need) > 0.0, ordered_ties, all_ties)

    for bi in range(nbb):
        valid = selected[bi * tq:(bi + 1) * tq, :] > 0.0
        k = k_ref[bi]
        v_ones = jnp.concatenate([v_ref[bi], jnp.ones((tk, dim), BF16)], axis=1)
        for h in range(heads):
            s = lax.dot_general(q_ref[bi, :, h * dim:(h + 1) * dim], k, nt, preferred_element_type=F32)
            s = jnp.where(valid, s, MASK_VALUE)
            p = jnp.exp(s - jnp.max(s, axis=-1, keepdims=True)).astype(BF16)
            o = jnp.dot(p, v_ones, preferred_element_type=F32)
            o_ref[bi, :, h * dim:(h + 1) * dim] = (o[:, 0:dim] / o[:, dim:dim + 1]).astype(o_ref.dtype)


def _with_carried_output(kern, q_ref, iq_ref, iw_ref, k_ref, v_ref, ik_ref, carried_ref, o_ref):
    del carried_ref
    kern(q_ref, iq_ref, iw_ref, k_ref, v_ref, ik_ref, o_ref)


def _dsa(q_rot, kv, iq_rot, ik_rot, iw, out_init, dm):
    tq, t, nb = dm.q_block, dm.seq, dm.batch
    hd = dm.dsa_heads * dm.dsa_dim
    assert t % tq == 0
    per_batch = lambda a: a.reshape(nb, t, a.shape[-1])
    q3, kv3, iq3, ik3, iw3, out = map(per_batch, (q_rot, kv, iq_rot, ik_rot, iw, out_init))
    for g in range(t // tq):
        tk = (g + 1) * tq
        nbb = 2 if (2 * tk <= t and nb % 2 == 0) else 1
        qrow = lambda w, g=g, nbb=nbb: pl.BlockSpec((nbb, tq, w), lambda bi: (bi, g, 0))
        krow = lambda w, blk=0, tk=tk, nbb=nbb: pl.BlockSpec((nbb, tk, w), lambda bi: (bi, 0, blk))
        kern = functools.partial(_dsa_kernel, heads=dm.dsa_heads, dim=dm.dsa_dim, idx_heads=dm.idx_heads,
                                 idx_dim=dm.idx_dim, topk=dm.topk, q_start=g * tq, n_hidden=t - tk)
        out = pl.pallas_call(
            functools.partial(_with_carried_output, kern),
            grid=(nb // nbb,),
            in_specs=[qrow(hd), qrow(iq3.shape[-1]), qrow(LANES), krow(dm.dsa_dim), krow(dm.dsa_dim, 1),
                      krow(LANES), pl.BlockSpec(memory_space=pl.ANY)],
            out_specs=qrow(hd),
            out_shape=jax.ShapeDtypeStruct((nb, t, hd), BF16),
            input_output_aliases={6: 0},
            compiler_params=_cparams(("parallel",)),
            name=f"dsa_q{g}",
        )(q3, iq3, iw3, kv3, kv3, ik3, out)
    return out.reshape(nb * t, hd)


def _forward(dm, x, positions, norm1_w, w_in, gla_gate_w2, gla_gate_b, gla_onorm_w, hgrn_lower_bounds,
             hgrn_onorm_w, w_branch_gla, w_branch_dsa, w_branch_hgrn, w_out, norm2_w, w_mlp_up, w_mlp_down,
             final_norm_w):
    m = dm.batch * dm.seq
    h = x.reshape(m, dm.d_model)
    src, ((gla_start, gla_width, gla_offs), (dsa_start, dsa_width, dsa_offs), (hg_start, hg_width, hg_offs)) = \
        _layout(dm)
    assert dm.dsa_dim == LANES and 2 * dm.idx_dim == LANES and dm.idx_heads <= dm.idx_dim
    idx_start = src["idx_q"][0]
    idx_width = -(-(src["idx_w"][0] + src["idx_w"][1] - idx_start) // LANES) * LANES
    w_t = jnp.swapaxes(w_in, 1, 2)

    def lanes(d):
        inv = ROPE_THETA ** (-jnp.arange(0, d, 2, dtype=F32) / d)
        reps = LANES // d
        f = jnp.tile(jnp.concatenate([inv, inv]), reps).reshape(1, LANES)
        sgn = jnp.tile(jnp.concatenate([-jnp.ones(d // 2, F32), jnp.ones(d // 2, F32)]), reps).reshape(1, LANES)
        return f, sgn

    pos_col = positions.reshape(m, 1)
    tabs = _rope_tables(pos_col, [lanes(dm.dsa_dim), lanes(dm.idx_dim)])

    u = _rmsnorm(h, norm1_w[0], BF16)
    for l in range(dm.depth):
        w2p = jnp.concatenate(
            [gla_gate_w2[l], jnp.zeros((LANES - dm.gla_rank, gla_gate_w2.shape[2]), F32)], axis=0)
        p_gla = _proj(u, w_t, l, gla_start, gla_width, [(None, BF16)], name="in_proj_gla")
        p_hg, w_up_bf16, w_down_bf16 = _proj(u, w_t, l, hg_start, hg_width, [(None, BF16)], vmem=VMEM_LIMIT_LARGE,
                                             casts=((w_mlp_up, 0), (w_mlp_down, 1)), name="in_proj_hg")
        a_low = _proj(u, w_t, l, src["gla_a"][0], LANES, [(LANES, F32)], name="in_proj_gate")
        hd = dm.dsa_heads * dm.dsa_dim
        q_rot, o_dsa_init = _proj(
            u, w_t, l, dsa_start, hd, [(None, BF16), (None, BF16)], tables=tabs[:2], tm_pref=1024,
            name="in_proj_dsa_q",
            epilogue=functools.partial(_rope_epilogue, half=dm.dsa_dim // 2, scale=dm.dsa_dim ** -0.5))
        kv = _proj(u, w_t, l, dsa_start + dsa_offs["dsa_k"], 2 * dm.dsa_dim, [(2 * dm.dsa_dim, BF16)],
                   tables=tabs[:2], tm_pref=1024, tn_pref=2 * dm.dsa_dim, name="in_proj_dsa_kv",
                   epilogue=functools.partial(_rope_epilogue, half=dm.dsa_dim // 2, scale=1.0, copy_groups=(1,)))
        iq_rot, ik_rot, iw = _proj(
            u, w_t, l, idx_start, idx_width, [(dm.idx_heads * dm.idx_dim, BF16), (LANES, BF16), (LANES, F32)],
            tables=tabs[2:], tm_pref=1024, tn_pref=idx_width, name="in_proj_idx",
            epilogue=functools.partial(_idx_epilogue, idx_heads=dm.idx_heads, idx_dim=dm.idx_dim))
        o_gla = _gla(p_gla, a_low, w2p, gla_gate_b[l], gla_onorm_w[l], dm, gla_offs)
        o_hg = _hgrn(p_hg, hgrn_lower_bounds, hgrn_onorm_w[l], l, dm, hg_offs)
        o_dsa = _dsa(q_rot, kv, iq_rot, ik_rot, iw, o_dsa_init, dm)
        merged = _merge(o_gla, o_dsa, o_hg, w_branch_gla, w_branch_dsa, w_branch_hgrn, l, p_hg,
                        (hg_offs["gate_a"], hg_offs["gate_b"], hg_offs["gate_c"]), dm.d_model)
        h = _matmul_residual(merged, w_out, l, h)
        last = l == dm.depth - 1
        res = _mlp(h, norm2_w[l], w_up_bf16, w_down_bf16, final_norm_w if last else norm1_w[l + 1], last)
        if not last:
            h, u = res
    return res.reshape(dm.batch, dm.seq, dm.d_model)


def kernel(x, positions, norm1_w, w_in, gla_gate_w2, gla_gate_b, gla_onorm_w, hgrn_lower_bounds, hgrn_onorm_w,
           w_branch_gla, w_branch_dsa, w_branch_hgrn, w_out, norm2_w, w_mlp_up, w_mlp_down, final_norm_w):
    return _forward(_prod_dims(), x, positions, norm1_w, w_in, gla_gate_w2, gla_gate_b, gla_onorm_w,
                    hgrn_lower_bounds, hgrn_onorm_w, w_branch_gla, w_branch_dsa, w_branch_hgrn, w_out, norm2_w,
                    w_mlp_up, w_mlp_down, final_norm_w)
```
